```python
import math
import jax, jax.numpy as jnp
from jax import lax
import numpy as np

D_MODEL = 2048
BATCH = 2
SEQ = 4096
DEPTH = 4

N_MIXERS = 4
N_HEADS = 16
HEAD_DIM = D_MODEL // N_HEADS
MLA_HEADS = 16
MLA_Q_LORA = 512
MLA_KV_LORA = 512
MLA_NOPE = 128
MLA_ROPE = 64
MLA_V = 128
ROPE_THETA = 10000.0
HGRN_EXPAND = 128
HGRN_HEADS = D_MODEL // HGRN_EXPAND
HGRN_KDIM = HGRN_EXPAND
HGRN_VDIM = D_MODEL // HGRN_HEADS
HGRN_CHUNK = 64
Q_BLOCK = 128
MOBA_BLOCK = 256
MOBA_TOPK = 3
MOBA_Q_CHUNK = 16
D_FF = 4 * D_MODEL
ALPHA = float((2 * DEPTH) ** 0.25)
BETA = float((8 * DEPTH) ** -0.25)
LN_EPS = 1e-5
RMS_EPS = 1e-6
N_MLA_LAYERS = len(range(0, DEPTH, N_MIXERS))
N_HGRN_LAYERS = len(range(1, DEPTH, N_MIXERS))
N_SB_LAYERS = len(range(2, DEPTH, N_MIXERS))
N_MOBA_LAYERS = len(range(3, DEPTH, N_MIXERS))

kernel_name = "hybrid_mla_hgrn2_stickbreak_moba_deepnorm"

F32 = jnp.float32


def layer_norm(x, g, b):
    xf = x.astype(F32)
    mu = jnp.mean(xf, axis=-1, keepdims=True)
    var = jnp.mean(jnp.square(xf - mu), axis=-1, keepdims=True)
    y = (xf - mu) * lax.rsqrt(var + LN_EPS) * g.astype(F32) + b.astype(F32)
    return y.astype(x.dtype)


def rms_norm(x, g):
    xf = x.astype(F32)
    y = xf * lax.rsqrt(jnp.mean(jnp.square(xf), axis=-1, keepdims=True) + RMS_EPS) * g.astype(F32)
    return y.astype(x.dtype)


def rope_cos_sin(S):
    inv = 1.0 / (ROPE_THETA ** (jnp.arange(0, MLA_ROPE, 2, dtype=F32) / MLA_ROPE))
    ang = jnp.arange(S, dtype=F32)[:, None] * inv[None, :]
    return jnp.cos(ang), jnp.sin(ang)


def apply_rope(x, cos, sin):
    x1, x2 = x[..., : MLA_ROPE // 2], x[..., MLA_ROPE // 2:]
    y = jnp.concatenate([x1 * cos - x2 * sin, x2 * cos + x1 * sin], axis=-1)
    return y.astype(x.dtype)


def causal_softmax_attention(q, k, v, scale):
    B, H, S, dk = q.shape
    nq = S // Q_BLOCK
    qb = q.reshape(B, H, nq, Q_BLOCK, dk).transpose(2, 0, 1, 3, 4)
    key_pos = jnp.arange(S)

    def one_block(args):
        q_blk, i = args
        s = jnp.einsum('bhqd,bhkd->bhqk', q_blk, k).astype(F32) * scale
        q_pos = i * Q_BLOCK + jnp.arange(Q_BLOCK)
        s = jnp.where(key_pos[None, :] <= q_pos[:, None], s, -jnp.inf)
        p = jax.nn.softmax(s, axis=-1).astype(v.dtype)
        return jnp.einsum('bhqk,bhkd->bhqd', p, v)

    o = lax.map(one_block, (qb, jnp.arange(nq)))
    return o.transpose(1, 0, 3, 2, 4).reshape(B, S, H, v.shape[-1])


def mla_mixer(x, w_in, q_norm, kv_norm, w_uq, w_ukv, w_o):
    B, S, _ = x.shape
    h = x @ w_in
    cq = h[..., :MLA_Q_LORA]
    ckv = h[..., MLA_Q_LORA:MLA_Q_LORA + MLA_KV_LORA]
    k_rope = h[..., MLA_Q_LORA + MLA_KV_LORA:]
    cq = rms_norm(cq, q_norm)
    ckv = rms_norm(ckv, kv_norm)
    q = (cq @ w_uq).reshape(B, S, MLA_HEADS, MLA_NOPE + MLA_ROPE)
    kv = (ckv @ w_ukv).reshape(B, S, MLA_HEADS, MLA_NOPE + MLA_V)
    q_nope, q_rope = q[..., :MLA_NOPE], q[..., MLA_NOPE:]
    k_nope, v = kv[..., :MLA_NOPE], kv[..., MLA_NOPE:]
    cos, sin = rope_cos_sin(S)
    q_rope = apply_rope(q_rope, cos[:, None, :], sin[:, None, :])
    k_rope = apply_rope(k_rope, cos, sin)
    k_rope = jnp.broadcast_to(k_rope[:, :, None, :], (B, S, MLA_HEADS, MLA_ROPE))
    q = jnp.concatenate([q_nope, q_rope], axis=-1).transpose(0, 2, 1, 3)
    k = jnp.concatenate([k_nope, k_rope], axis=-1).transpose(0, 2, 1, 3)
    v = v.transpose(0, 2, 1, 3)
    o = causal_softmax_attention(q, k, v, (MLA_NOPE + MLA_ROPE) ** -0.5)
    return o.reshape(B, S, MLA_HEADS * MLA_V) @ w_o


def hgrn2_mixer(x, w_in, lb, o_norm, w_o):
    B, S, D = x.shape
    q, f_pre, i_in, g = jnp.split(x @ w_in, 4, axis=-1)
    lb = lb.astype(F32)
    log_f = jnp.logaddexp(jnp.log(lb), jnp.log1p(-lb) + jax.nn.log_sigmoid(f_pre.astype(F32)))
    k = -jnp.expm1(log_f)
    n = S // HGRN_CHUNK

    def chunks(t, dh):
        return t.astype(F32).reshape(B, n, HGRN_CHUNK, HGRN_HEADS, dh).transpose(1, 0, 3, 2, 4)

    xs = (chunks(q, HGRN_KDIM), chunks(k, HGRN_KDIM), chunks(i_in, HGRN_VDIM), chunks(log_f, HGRN_KDIM))
    causal = jnp.tril(jnp.ones((HGRN_CHUNK, HGRN_CHUNK), dtype=bool))

    def step(state, inp):
        qc, kc, vc, lf = inp
        b = jnp.cumsum(lf, axis=2)
        diff = b[:, :, :, None, :] - b[:, :, None, :, :]
        decay = jnp.exp(jnp.where(causal[:, :, None], diff, -jnp.inf))
        a = jnp.sum(qc[:, :, :, None, :] * kc[:, :, None, :, :] * decay, axis=-1)
        o = (jnp.einsum('bhts,bhsv->bhtv', a, vc)
             + jnp.einsum('bhtd,bhdv->bhtv', qc * jnp.exp(b), state))
        b_end = b[:, :, -1, :]
        state = (jnp.exp(b_end)[..., None] * state
                 + jnp.einsum('bhsd,bhsv->bhdv', kc * jnp.exp(b_end[:, :, None, :] - b), vc))
        return state, o

    state0 = jnp.zeros((B, HGRN_HEADS, HGRN_KDIM, HGRN_VDIM), F32)
    _, o = lax.scan(step, state0, xs)
    o = o.transpose(1, 0, 3, 2, 4).reshape(B, S, HGRN_HEADS, HGRN_VDIM)
    o = rms_norm(o, o_norm.reshape(HGRN_HEADS, HGRN_VDIM))
    o = o.reshape(B, S, D).astype(x.dtype) * jax.nn.silu(g)
    return o @ w_o


def stick_breaking_attention(q, k, v):
    B, H, S, d = q.shape
    scale = d ** -0.5
    nq = S // Q_BLOCK
    qb = q.reshape(B, H, nq, Q_BLOCK, d).transpose(2, 0, 1, 3, 4)
    key_pos = jnp.arange(S)

    def one_block(args):
        q_blk, i = args
        z = jnp.einsum('bhqd,bhkd->bhqk', q_blk, k).astype(F32) * scale
        q_pos = i * Q_BLOCK + jnp.arange(Q_BLOCK)
        strict = key_pos[None, :] < q_pos[:, None]
        log_1m = jnp.where(strict, jax.nn.log_sigmoid(-z), 0.0)
        after = lax.cumsum(log_1m, axis=3, reverse=True) - log_1m
        w = jnp.where(strict, jnp.exp(jax.nn.log_sigmoid(z) + after), 0.0).astype(v.dtype)
        return jnp.einsum('bhqk,bhkd->bhqd', w, v)

    o = lax.map(one_block, (qb, jnp.arange(nq)))
    return o.transpose(1, 0, 3, 2, 4).reshape(B, S, H, d)


def stick_breaking_mixer(x, w_in, w_o):
    B, S, D = x.shape
    q, k, v = jnp.split(x @ w_in, 3, axis=-1)
    heads = lambda t: t.reshape(B, S, N_HEADS, HEAD_DIM).transpose(0, 2, 1, 3)
    o = stick_breaking_attention(heads(q), heads(k), heads(v))
    return o.reshape(B, S, D) @ w_o


def moba_attention(q, k, v):
    B, H, S, d = q.shape
    s_pad = -(-S // MOBA_BLOCK) * MOBA_BLOCK
    pad = ((0, 0), (0, 0), (0, s_pad - S), (0, 0))
    q, k, v = jnp.pad(q, pad), jnp.pad(k, pad), jnp.pad(v, pad)
    nb = s_pad // MOBA_BLOCK
    k_sel = min(MOBA_TOPK, nb)
    scale = d ** -0.5
    k_blocks = k.reshape(B, H, nb, MOBA_BLOCK, d)
    v_blocks = v.reshape(B, H, nb, MOBA_BLOCK, d)
    k_mean = jnp.mean(k_blocks.astype(F32), axis=3)
    gate = jnp.einsum('bhsd,bhnd->bhsn', q.astype(F32), k_mean)
    q_blk = jnp.arange(s_pad) // MOBA_BLOCK
    fully_past = jnp.arange(nb)[None, :] < q_blk[:, None]
    gate = jnp.where(fully_past, gate, -jnp.inf)
    _, sel = lax.top_k(gate, k_sel)
    n_chunks = s_pad // MOBA_Q_CHUNK
    q_c = q.reshape(B, H, n_chunks, MOBA_Q_CHUNK, d).transpose(2, 0, 1, 3, 4)
    sel_c = sel.reshape(B, H, n_chunks, MOBA_Q_CHUNK, k_sel).transpose(2, 0, 1, 3, 4)
    b_idx = jnp.arange(B)[:, None, None, None]
    h_idx = jnp.arange(H)[None, :, None, None]
    n_g = k_sel * MOBA_BLOCK

    def one_chunk(args):
        qb, sb, ci = args
        start = ci * MOBA_Q_CHUNK
        blk = start // MOBA_BLOCK
        q_pos = start + jnp.arange(MOBA_Q_CHUNK)
        k_g = k_blocks[b_idx, h_idx, sb]
        v_g = v_blocks[b_idx, h_idx, sb]
        s_g = jnp.einsum('bhqd,bhqrkd->bhqrk', qb, k_g).astype(F32) * scale
        valid = jnp.arange(k_sel) < blk
        s_g = jnp.where(valid[None, None, None, :, None], s_g, -jnp.inf)
        k_own = lax.dynamic_slice_in_dim(k, blk * MOBA_BLOCK, MOBA_BLOCK, axis=2)
        v_own = lax.dynamic_slice_in_dim(v, blk * MOBA_BLOCK, MOBA_BLOCK, axis=2)
        s_own = jnp.einsum('bhqd,bhkd->bhqk', qb, k_own).astype(F32) * scale
        own_pos = blk * MOBA_BLOCK + jnp.arange(MOBA_BLOCK)
        s_own = jnp.where(own_pos[None, :] <= q_pos[:, None], s_own, -jnp.inf)
        logits = jnp.concatenate([s_g.reshape(B, H, MOBA_Q_CHUNK, n_g), s_own], axis=-1)
        p = jax.nn.softmax(logits, axis=-1).astype(v.dtype)
        p_g = p[..., :n_g].reshape(B, H, MOBA_Q_CHUNK, k_sel, MOBA_BLOCK)
        return (jnp.einsum('bhqrk,bhqrkd->bhqd', p_g, v_g)
                + jnp.einsum('bhqk,bhkd->bhqd', p[..., n_g:], v_own))

    o = lax.map(one_chunk, (q_c, sel_c, jnp.arange(n_chunks)))
    o = o.transpose(1, 0, 3, 2, 4).reshape(B, s_pad, H, d)
    return o[:, :S]


def moba_mixer(x, w_in, w_o):
    B, S, D = x.shape
    q, k, v = jnp.split(x @ w_in, 3, axis=-1)
    heads = lambda t: t.reshape(B, S, N_HEADS, HEAD_DIM).transpose(0, 2, 1, 3)
    o = moba_attention(heads(q), heads(k), heads(v))
    return o.reshape(B, S, D) @ w_o


def sq_relu_mlp(x, w1, w2):
    return jnp.square(jax.nn.relu(x @ w1)) @ w2


def setup_inputs(seed: int = 0) -> dict:
    key = jax.random.key(seed)
    ks = jax.random.split(key, 20)

    def dense(k, shape, fan_in, scale=1.0):
        return jax.random.normal(k, shape, F32) * (scale * fan_in ** -0.5)

    def gain(k, shape):
        return 1.0 + 0.05 * jax.random.normal(k, shape, F32)

    D = D_MODEL
    mla_in_w = MLA_Q_LORA + MLA_KV_LORA + MLA_ROPE
    return {
        "x": jax.random.normal(ks[0], (BATCH, SEQ, D), F32),
        "mla_w_in": dense(ks[1], (N_MLA_LAYERS, D, mla_in_w), D),
        "mla_q_norm": gain(ks[2], (N_MLA_LAYERS, MLA_Q_LORA)),
        "mla_kv_norm": gain(ks[3], (N_MLA_LAYERS, MLA_KV_LORA)),
        "mla_w_uq": dense(ks[4], (N_MLA_LAYERS, MLA_Q_LORA, MLA_HEADS * (MLA_NOPE + MLA_ROPE)), MLA_Q_LORA),
        "mla_w_ukv": dense(ks[5], (N_MLA_LAYERS, MLA_KV_LORA, MLA_HEADS * (MLA_NOPE + MLA_V)), MLA_KV_LORA),
        "mla_w_o": dense(ks[6], (N_MLA_LAYERS, MLA_HEADS * MLA_V, D), MLA_HEADS * MLA_V, BETA),
        "hgrn_w_in": dense(ks[7], (N_HGRN_LAYERS, D, 4 * D), D),
        "hgrn_lb_logits": 0.1 * jax.random.normal(ks[8], (DEPTH, D), F32),
        "hgrn_o_norm": gain(ks[9], (N_HGRN_LAYERS, D)),
        "hgrn_w_o": dense(ks[10], (N_HGRN_LAYERS, D, D), D, BETA),
        "sb_w_in": dense(ks[11], (N_SB_LAYERS, D, 3 * D), D),
        "sb_w_o": dense(ks[12], (N_SB_LAYERS, D, D), D, BETA),
        "moba_w_in": dense(ks[13], (N_MOBA_LAYERS, D, 3 * D), D),
        "moba_w_o": dense(ks[14], (N_MOBA_LAYERS, D, D), D, BETA),
        "ln_g": gain(ks[15], (DEPTH, 2, D)),
        "ln_b": 0.02 * jax.random.normal(ks[16], (DEPTH, 2, D), F32),
        "mlp_w1": dense(ks[17], (DEPTH, D, D_FF), D),
        "mlp_w2": dense(ks[18], (DEPTH, D_FF, D), D_FF, BETA),
    }


def reference(x, mla_w_in, mla_q_norm, mla_kv_norm, mla_w_uq, mla_w_ukv, mla_w_o,
              hgrn_w_in, hgrn_lb_logits, hgrn_o_norm, hgrn_w_o,
              sb_w_in, sb_w_o, moba_w_in, moba_w_o,
              ln_g, ln_b, mlp_w1, mlp_w2):
    lb_p = jax.nn.softmax(hgrn_lb_logits.astype(F32), axis=0)
    lower_bounds = jnp.cumsum(lb_p, axis=0) - lb_p[0]
    h = x
    for i in range(DEPTH):
        kind, slot = i % N_MIXERS, i // N_MIXERS
        if kind == 0:
            y = mla_mixer(h, mla_w_in[slot], mla_q_norm[slot], mla_kv_norm[slot],
                          mla_w_uq[slot], mla_w_ukv[slot], mla_w_o[slot])
        elif kind == 1:
            y = hgrn2_mixer(h, hgrn_w_in[slot], lower_bounds[i], hgrn_o_norm[slot], hgrn_w_o[slot])
        elif kind == 2:
            y = stick_breaking_mixer(h, sb_w_in[slot], sb_w_o[slot])
        else:
            y = moba_mixer(h, moba_w_in[slot], moba_w_o[slot])
        h = layer_norm(ALPHA * h + y, ln_g[i, 0], ln_b[i, 0])
        h = layer_norm(ALPHA * h + sq_relu_mlp(h, mlp_w1[i], mlp_w2[i]), ln_g[i, 1], ln_b[i, 1])
    return h
```

```python
import functools

import jax
import jax.numpy as jnp
from jax import lax
from jax.experimental import pallas as pl
from jax.experimental.pallas import tpu as pltpu

F32 = jnp.float32
BF16 = jnp.bfloat16

D_MODEL = 2048
DEPTH = 4
N_HEADS = 16
HEAD_DIM = 128
MLA_Q_LORA = 512
MLA_KV_LORA = 512
MLA_NOPE = 128
MLA_ROPE = 64
MLA_V = 128
ROPE_THETA = 10000.0
HGRN_CHUNK = 64
HGRN_SUB = 8
MOBA_BLOCK = 256
MOBA_TOPK = 3
ALPHA = float((2 * DEPTH) ** 0.25)
LN_EPS = 1e-5
RMS_EPS = 1e-6

V7X_VMEM_BYTES = 64 * 1024 * 1024
VMEM_LIMIT = V7X_VMEM_BYTES - 8 * 1024 * 1024
LANES = 128
MASKED = -1e30

_NT = (((1,), (1,)), ((), ()))


def _params(*sem):
    return pltpu.CompilerParams(dimension_semantics=sem, vmem_limit_bytes=VMEM_LIMIT)


def _layer_norm_rows(y, g, b):
    mu = jnp.mean(y, axis=-1, keepdims=True)
    d = y - mu
    var = jnp.mean(d * d, axis=-1, keepdims=True)
    return d * lax.rsqrt(var + LN_EPS) * g + b


def _rms_rows(x, g):
    return x * lax.rsqrt(jnp.mean(x * x, axis=-1, keepdims=True) + RMS_EPS) * g


def _proj_kernel(x_ref, w_ref, o_ref, *, act, scaled_tiles, scale):
    acc = jnp.dot(x_ref[...].astype(BF16), w_ref[...], preferred_element_type=F32)
    if act == "relu2":
        r = jnp.maximum(acc, 0.0)
        acc = r * r
    if scaled_tiles:
        acc = acc * jnp.where(pl.program_id(1) < scaled_tiles, scale, 1.0)
    o_ref[...] = acc.astype(o_ref.dtype)


def _proj(x, w, out_dtype, *, tm=512, tn=1024, act=None, scaled_cols=0, scale=1.0):
    M, K = x.shape
    N = w.shape[1]
    tn = min(tn, N)
    assert scaled_cols % tn == 0
    return pl.pallas_call(
        functools.partial(_proj_kernel, act=act, scaled_tiles=scaled_cols // tn, scale=scale),
        grid=(M // tm, N // tn),
        in_specs=[pl.BlockSpec((tm, K), lambda i, j: (i, 0)),
                  pl.BlockSpec((K, tn), lambda i, j: (0, j))],
        out_specs=pl.BlockSpec((tm, tn), lambda i, j: (i, j)),
        out_shape=jax.ShapeDtypeStruct((M, N), out_dtype),
        compiler_params=_params("parallel", "parallel"),
        name="proj",
    )(x, w)


def _proj_res_ln_kernel(x_ref, w_ref, h_ref, g_ref, b_ref, o_ref, acc_ref, *, nk):
    k = pl.program_id(1)
    part = jnp.dot(x_ref[...], w_ref[...], preferred_element_type=F32)

    def finish(y):
        o_ref[...] = _layer_norm_rows(ALPHA * h_ref[...] + y, g_ref[...], b_ref[...])

    if nk == 1:
        finish(part)
    else:
        @pl.when(k == 0)
        def _():
            acc_ref[...] = part

        @pl.when(jnp.logical_and(k > 0, k < nk - 1))
        def _():
            acc_ref[...] += part

        @pl.when(k == nk - 1)
        def _():
            finish(acc_ref[...] + part)


def _proj_res_ln(x, w, h, g, b, *, tm=512, tk=2048):
    M, K = x.shape
    N = w.shape[1]
    nk = K // tk
    return pl.pallas_call(
        functools.partial(_proj_res_ln_kernel, nk=nk),
        grid=(M // tm, nk),
        in_specs=[pl.BlockSpec((tm, tk), lambda i, k: (i, k)),
                  pl.BlockSpec((tk, N), lambda i, k: (k, 0)),
                  pl.BlockSpec((tm, N), lambda i, k: (i, 0)),
                  pl.BlockSpec((1, N), lambda i, k: (0, 0)),
                  pl.BlockSpec((1, N), lambda i, k: (0, 0))],
        out_specs=pl.BlockSpec((tm, N), lambda i, k: (i, 0)),
        out_shape=jax.ShapeDtypeStruct((M, N), F32),
        scratch_shapes=[pltpu.VMEM((tm, N), F32)],
        compiler_params=_params("parallel", "arbitrary"),
        name="proj_res_ln",
    )(x, w, h, g.reshape(1, N), b.reshape(1, N))


def _mla_in_kernel(h_ref, w_ref, qg_ref, kvg_ref, ct_ref, st_ref, cq_ref, ckv_ref, kr_ref):
    acc = jnp.dot(h_ref[...].astype(BF16), w_ref[...], preferred_element_type=F32)
    ql, kvl = MLA_Q_LORA, MLA_KV_LORA
    cq_ref[...] = _rms_rows(acc[:, :ql], qg_ref[...]).astype(cq_ref.dtype)
    ckv_ref[...] = _rms_rows(acc[:, ql:ql + kvl], kvg_ref[...]).astype(ckv_ref.dtype)
    a = acc[:, ql + kvl:ql + kvl + LANES]
    a_sw = acc[:, ql + kvl + LANES:]
    kr_ref[...] = (a * ct_ref[...] + a_sw * st_ref[...]).astype(kr_ref.dtype)


def _mla_in(h, w_ext, qg, kvg, ct, st, S, *, tm=512):
    M, K = h.shape
    N = w_ext.shape[1]
    ns = S // tm
    return pl.pallas_call(
        _mla_in_kernel,
        grid=(M // tm,),
        in_specs=[pl.BlockSpec((tm, K), lambda i: (i, 0)),
                  pl.BlockSpec((K, N), lambda i: (0, 0)),
                  pl.BlockSpec((1, MLA_Q_LORA), lambda i: (0, 0)),
                  pl.BlockSpec((1, MLA_KV_LORA), lambda i: (0, 0)),
                  pl.BlockSpec((tm, LANES), lambda i: (i % ns, 0)),
                  pl.BlockSpec((tm, LANES), lambda i: (i % ns, 0))],
        out_specs=[pl.BlockSpec((tm, MLA_Q_LORA), lambda i: (i, 0)),
                   pl.BlockSpec((tm, MLA_KV_LORA), lambda i: (i, 0)),
                   pl.BlockSpec((tm, LANES), lambda i: (i, 0))],
        out_shape=[jax.ShapeDtypeStruct((M, MLA_Q_LORA), BF16),
                   jax.ShapeDtypeStruct((M, MLA_KV_LORA), BF16),
                   jax.ShapeDtypeStruct((M, LANES), BF16)],
        compiler_params=_params("parallel"),
        name="mla_in",
    )(h, w_ext, qg.reshape(1, -1), kvg.reshape(1, -1), ct, st)


def _mla_uq_kernel(cq_ref, wm_ref, ws_ref, ct_ref, st_ref, q_ref, *, heads, scale):
    x = cq_ref[...]
    a = jnp.dot(x, wm_ref[...], preferred_element_type=F32)
    a_sw = jnp.dot(x, ws_ref[...], preferred_element_type=F32)
    ct = ct_ref[...]
    st = st_ref[...]
    for hh in range(heads):
        lo = hh * 2 * LANES
        q_ref[:, lo:lo + LANES] = (a[:, lo:lo + LANES] * scale).astype(q_ref.dtype)
        rot = a[:, lo + LANES:lo + 2 * LANES] * ct + a_sw[:, hh * LANES:(hh + 1) * LANES] * st
        q_ref[:, lo + LANES:lo + 2 * LANES] = (rot * scale).astype(q_ref.dtype)


def _mla_uq(cq, w_main, w_sw, ct, st, S, scale, *, tm=512, heads_per_step=4):
    M, K = cq.shape
    hp = heads_per_step
    ns = S // tm
    return pl.pallas_call(
        functools.partial(_mla_uq_kernel, heads=hp, scale=scale),
        grid=(M // tm, N_HEADS // hp),
        in_specs=[pl.BlockSpec((tm, K), lambda i, j: (i, 0)),
                  pl.BlockSpec((K, hp * 2 * LANES), lambda i, j: (0, j)),
                  pl.BlockSpec((K, hp * LANES), lambda i, j: (0, j)),
                  pl.BlockSpec((tm, LANES), lambda i, j: (i % ns, 0)),
                  pl.BlockSpec((tm, LANES), lambda i, j: (i % ns, 0))],
        out_specs=pl.BlockSpec((tm, hp * 2 * LANES), lambda i, j: (i, j)),
        out_shape=jax.ShapeDtypeStruct((M, N_HEADS * 2 * LANES), BF16),
        compiler_params=_params("parallel", "parallel"),
        name="mla_uq",
    )(cq, w_main, w_sw, ct, st)


def _mla_attn_kernel(q_ref, kn_ref, kr_ref, v_ref, o_ref, *, t):
    qi = pl.program_id(2)
    q = q_ref[...]

    def tile(ki, carry, diagonal):
        m, l, acc = carry
        off = pl.multiple_of(ki * t, t)
        k = jnp.concatenate([kn_ref[pl.ds(off, t), :], kr_ref[pl.ds(off, t), :]], axis=1)
        s = lax.dot_general(q, k, _NT, preferred_element_type=F32)
        if diagonal:
            row = lax.broadcasted_iota(jnp.int32, (t, t), 0)
            col = lax.broadcasted_iota(jnp.int32, (t, t), 1)
            s = jnp.where(col <= row, s, MASKED)
        m_new = jnp.maximum(m, jnp.max(s, axis=-1, keepdims=True))
        alpha = jnp.exp(m - m_new)
        p = jnp.exp(s - m_new)
        l = alpha * l + jnp.sum(p, axis=-1, keepdims=True)
        acc = alpha * acc + jnp.dot(p.astype(BF16), v_ref[pl.ds(off, t), :], preferred_element_type=F32)
        return m_new, l, acc

    init = (jnp.full((t, 1), MASKED, F32), jnp.zeros((t, 1), F32), jnp.zeros((t, MLA_V), F32))
    carry = lax.fori_loop(0, qi, lambda ki, c: tile(ki, c, False), init)
    _, l, acc = tile(qi, carry, True)
    o_ref[...] = (acc / l).astype(o_ref.dtype)


def _mla_attn(q, kv, kr, B, S, *, t=256):
    nq = S // t
    H = N_HEADS
    return pl.pallas_call(
        functools.partial(_mla_attn_kernel, t=t),
        grid=(B, H, nq),
        in_specs=[pl.BlockSpec((t, 2 * LANES), lambda b, h, i: (b * nq + i, h)),
                  pl.BlockSpec((S, LANES), lambda b, h, i: (b, h)),
                  pl.BlockSpec((S, LANES), lambda b, h, i: (b, 0)),
                  pl.BlockSpec((S, LANES), lambda b, h, i: (b, H + h))],
        out_specs=pl.BlockSpec((t, LANES), lambda b, h, i: (b * nq + i, h)),
        out_shape=jax.ShapeDtypeStruct((B * S, H * MLA_V), BF16),
        compiler_params=_params("parallel", "parallel", "arbitrary"),
        name="mla_attn",
    )(q, kv, kr, kv)


def _rope_tables(S):
    half = MLA_ROPE // 2
    inv = 1.0 / (ROPE_THETA ** (jnp.arange(0, MLA_ROPE, 2, dtype=F32) / MLA_ROPE))
    ang = jnp.arange(S, dtype=F32)[:, None] * inv[None, :]
    cos, sin = jnp.cos(ang), jnp.sin(ang)
    zeros = jnp.zeros((S, LANES - 2 * half), F32)
    return (jnp.concatenate([cos, cos, zeros], axis=1), jnp.concatenate([-sin, sin, zeros], axis=1))


def _mla_weights(w_in, w_uq, w_ukv):
    D = w_in.shape[0]
    half = MLA_ROPE // 2
    base = MLA_Q_LORA + MLA_KV_LORA
    x1, x2 = w_in[:, base:base + half], w_in[:, base + half:base + 2 * half]
    pad = jnp.zeros((D, LANES - 2 * half), w_in.dtype)
    w_in_ext = jnp.concatenate([w_in[:, :base], x1, x2, pad, x2, x1, pad], axis=1).astype(BF16)

    wq = w_uq.reshape(MLA_Q_LORA, N_HEADS, MLA_NOPE + MLA_ROPE)
    nope, r1, r2 = wq[..., :MLA_NOPE], wq[..., MLA_NOPE:MLA_NOPE + half], wq[..., MLA_NOPE + half:]
    padq = jnp.zeros((MLA_Q_LORA, N_HEADS, LANES - 2 * half), w_uq.dtype)
    w_main = jnp.concatenate([nope, r1, r2, padq], axis=-1).reshape(MLA_Q_LORA, N_HEADS * 2 * LANES).astype(BF16)
    w_sw = jnp.concatenate([r2, r1, padq], axis=-1).reshape(MLA_Q_LORA, N_HEADS * LANES).astype(BF16)

    wkv = w_ukv.reshape(MLA_KV_LORA, N_HEADS, MLA_NOPE + MLA_V)
    w_kv = jnp.concatenate([wkv[..., :MLA_NOPE].reshape(MLA_KV_LORA, -1),
                            wkv[..., MLA_NOPE:].reshape(MLA_KV_LORA, -1)], axis=1).astype(BF16)
    return w_in_ext, w_main, w_sw, w_kv


def _mla_mixer(h, B, S, w_in, q_norm, kv_norm, w_uq, w_ukv):
    w_in_ext, w_main, w_sw, w_kv = _mla_weights(w_in, w_uq, w_ukv)
    ct, st = _rope_tables(S)
    cq, ckv, kr = _mla_in(h, w_in_ext, q_norm, kv_norm, ct, st, S)
    scale = float((MLA_NOPE + MLA_ROPE) ** -0.5)
    q = _mla_uq(cq, w_main, w_sw, ct, st, S, scale)
    kv = _proj(ckv, w_kv, BF16)
    return _mla_attn(q, kv, kr, B, S)


def _hgrn_kernel(q_ref, f_ref, i_ref, g_ref, lbl_ref, on_ref, o_ref, state_ref, *, layer, tile, chunk):
    C, SB = chunk, HGRN_SUB
    nb = C // SB

    @pl.when(pl.program_id(2) == 0)
    def _():
        state_ref[...] = jnp.zeros_like(state_ref)

    lg = lbl_ref[...]
    e = jnp.exp(lg - jnp.max(lg, axis=0, keepdims=True))
    p = e / jnp.sum(e, axis=0, keepdims=True)
    cs = p[0:1]
    for r in range(1, layer + 1):
        cs = cs + p[r:r + 1]
    lb = cs - p[0:1]
    log_lb = jnp.log(lb)
    log1m_lb = jnp.log1p(-lb)
    one_m_lb = 1.0 - lb
    onorm = on_ref[...]

    row = lax.broadcasted_iota(jnp.int32, (C, C), 0)
    col = lax.broadcasted_iota(jnp.int32, (C, C), 1)
    tri = (col <= row).astype(F32)
    bdiff = row // SB - col // SB
    tio = lax.broadcasted_iota(jnp.int32, (nb, SB, LANES), 1)

    def chunk_step(c, carry):
        off = pl.multiple_of(c * C, C)
        q = q_ref[pl.ds(off, C), :]
        fp = f_ref[pl.ds(off, C), :]
        v = i_ref[pl.ds(off, C), :]
        g = g_ref[pl.ds(off, C), :]

        ls = jnp.minimum(fp, 0.0) - jnp.log1p(jnp.exp(-jnp.abs(fp)))
        cc = log1m_lb + ls
        lf = jnp.maximum(log_lb, cc) + jnp.log1p(jnp.exp(-jnp.abs(log_lb - cc)))
        kk = one_m_lb * jax.nn.sigmoid(-fp)

        b = jnp.dot(tri, lf, precision=lax.Precision.HIGHEST, preferred_element_type=F32)
        bend = b[C - 1:C, :]

        st_t = state_ref[...]
        qe = q * jnp.exp(b)
        o = lax.dot_general(qe.astype(BF16), st_t.astype(BF16), _NT, preferred_element_type=F32)
        kd = kk * jnp.exp(bend - b)
        state_ref[...] = st_t * jnp.exp(bend) + jnp.dot(v.T.astype(BF16), kd.astype(BF16),
                                                       preferred_element_type=F32)

        b3 = b.reshape(nb, SB, LANES)
        q3 = q.reshape(nb, SB, LANES)
        k3 = kk.reshape(nb, SB, LANES)
        v3 = v.reshape(nb, SB, LANES)

        r = b3[:, 0:1, :]
        r_next = jnp.concatenate([r[1:], bend.reshape(1, 1, LANES)], axis=0)
        kt = (k3 * jnp.exp(r_next - b3)).reshape(C, LANES)
        gdec = jnp.exp(r_next - r)
        ql = q3 * jnp.exp(b3 - r)
        levels = [ql]
        for lvl in range(1, nb - 1):
            fac = jnp.concatenate([jnp.zeros((lvl, 1, LANES), F32), gdec[:nb - lvl]], axis=0)
            ql = ql * fac
            levels.append(ql)
        qs = jnp.concatenate([x.reshape(C, LANES) for x in levels], axis=0).astype(BF16)
        rl = lax.dot_general(qs, kt.astype(BF16), _NT, preferred_element_type=F32)
        a = jnp.zeros((C, C), F32)
        for lvl in range(1, nb):
            a = a + jnp.where(bdiff == lvl, rl[(lvl - 1) * C:lvl * C, :], 0.0)
        o = o + jnp.dot(a.astype(BF16), v.astype(BF16), preferred_element_type=F32)

        od = jnp.zeros((nb, SB, LANES), F32)
        for s in range(SB):
            dec = jnp.exp(jnp.minimum(b3 - b3[:, s:s + 1, :], 0.0))
            pr = jnp.where(tio >= s, q3 * k3[:, s:s + 1, :] * dec, 0.0)
            od = od + jnp.sum(pr, axis=-1, keepdims=True) * v3[:, s:s + 1, :]
        o = o + od.reshape(C, LANES)

        y = _rms_rows(o, onorm)
        o_ref[pl.ds(off, C), :] = (y * (g * jax.nn.sigmoid(g))).astype(o_ref.dtype)
        return carry

    lax.fori_loop(0, tile // C, chunk_step, 0)


def _hgrn_mixer_core(proj, lb_logits, o_norm, B, S, layer, *, tile=512):
    nt = S // tile
    H = N_HEADS
    blk = lambda sec: pl.BlockSpec((tile, LANES), lambda b, h, t, sec=sec: (b * nt + t, sec * H + h))
    return pl.pallas_call(
        functools.partial(_hgrn_kernel, layer=layer, tile=tile, chunk=HGRN_CHUNK),
        grid=(B, H, nt),
        in_specs=[blk(0), blk(1), blk(2), blk(3),
                  pl.BlockSpec((DEPTH, LANES), lambda b, h, t: (0, h)),
                  pl.BlockSpec((1, LANES), lambda b, h, t: (0, h))],
        out_specs=pl.BlockSpec((tile, LANES), lambda b, h, t: (b * nt + t, h)),
        out_shape=jax.ShapeDtypeStruct((B * S, D_MODEL), BF16),
        scratch_shapes=[pltpu.VMEM((LANES, LANES), F32)],
        compiler_params=_params("parallel", "parallel", "arbitrary"),
        name="hgrn",
    )(proj, proj, proj, proj, lb_logits, o_norm.reshape(1, -1))


def _sb_kernel(q_ref, k_ref, v_ref, o_ref, *, t):
    qi = pl.program_id(2)
    q = q_ref[...]
    row = lax.broadcasted_iota(jnp.int32, (t, t), 0)
    col = lax.broadcasted_iota(jnp.int32, (t, t), 1)
    later = (row > col).astype(BF16)

    def tile(ki, carry, diagonal):
        rsum, acc = carry
        off = pl.multiple_of(ki * t, t)
        z = lax.dot_general(q, k_ref[pl.ds(off, t), :], _NT, preferred_element_type=F32)
        l1m = -(jnp.maximum(z, 0.0) + jnp.log1p(jnp.exp(-jnp.abs(z))))
        if diagonal:
            l1m = jnp.where(col < row, l1m, 0.0)
        hi = l1m.astype(BF16)
        lo = (l1m - hi.astype(F32)).astype(BF16)
        after = (jnp.dot(hi, later, preferred_element_type=F32)
                 + jnp.dot(lo, later, preferred_element_type=F32))
        w = jnp.exp(z + l1m + after + rsum)
        if diagonal:
            w = jnp.where(col < row, w, 0.0)
        acc = acc + jnp.dot(w.astype(BF16), v_ref[pl.ds(off, t), :], preferred_element_type=F32)
        rsum = rsum + jnp.sum(l1m, axis=-1, keepdims=True)
        return rsum, acc

    carry = tile(qi, (jnp.zeros((t, 1), F32), jnp.zeros((t, HEAD_DIM), F32)), True)
    _, acc = lax.fori_loop(0, qi, lambda j, c: tile(qi - 1 - j, c, False), carry)
    o_ref[...] = acc.astype(o_ref.dtype)


def _sb_attn(qkv, B, S, *, t=256):
    nq = S // t
    H = N_HEADS
    return pl.pallas_call(
        functools.partial(_sb_kernel, t=t),
        grid=(B, H, nq),
        in_specs=[pl.BlockSpec((t, LANES), lambda b, h, i: (b * nq + i, h)),
                  pl.BlockSpec((S, LANES), lambda b, h, i: (b, H + h)),
                  pl.BlockSpec((S, LANES), lambda b, h, i: (b, 2 * H + h))],
        out_specs=pl.BlockSpec((t, LANES), lambda b, h, i: (b * nq + i, h)),
        out_shape=jax.ShapeDtypeStruct((B * S, D_MODEL), BF16),
        compiler_params=_params("parallel", "parallel", "arbitrary"),
        name="sb_attn",
    )(qkv, qkv, qkv)


def _moba_kernel(q_ref, k_ref, v_ref, o_ref, kmean_ref, *, nblk):
    t = MOBA_BLOCK
    qi = pl.program_id(2)

    @pl.when(qi == 0)
    def _():
        kmean_ref[...] = jnp.zeros_like(kmean_ref)
        kmean_ref[0:nblk, :] = jnp.mean(k_ref[...].astype(F32).reshape(nblk, t, LANES), axis=1)

    q = q_ref[...]

    off = pl.multiple_of(qi * t, t)
    s = lax.dot_general(q, k_ref[pl.ds(off, t), :], _NT, preferred_element_type=F32)
    row = lax.broadcasted_iota(jnp.int32, (t, t), 0)
    col = lax.broadcasted_iota(jnp.int32, (t, t), 1)
    s = jnp.where(col <= row, s, MASKED)
    m = jnp.max(s, axis=-1, keepdims=True)
    p = jnp.exp(s - m)
    l = jnp.sum(p, axis=-1, keepdims=True)
    acc = jnp.dot(p.astype(BF16), v_ref[pl.ds(off, t), :], preferred_element_type=F32)

    gate = lax.dot_general(kmean_ref[...], q.astype(F32), _NT, precision=lax.Precision.HIGHEST,
                           preferred_element_type=F32)
    blk = lax.broadcasted_iota(jnp.int32, (LANES, t), 0)
    neg_inf = jnp.float32(-jnp.inf)
    gate = jnp.where(blk < qi, gate, neg_inf)
    sel = jnp.zeros((LANES, t), F32)
    for _ in range(MOBA_TOPK):
        mx = jnp.max(gate, axis=0, keepdims=True)
        first = jnp.min(jnp.where(gate == mx, blk, LANES), axis=0, keepdims=True)
        pick = jnp.logical_and(blk == first, mx > neg_inf)
        sel = jnp.where(pick, 1.0, sel)
        gate = jnp.where(pick, neg_inf, gate)
    sel_q = sel.T.astype(BF16)
    pick_row = lax.broadcasted_iota(jnp.int32, (LANES, t), 0)

    def past(n, carry):
        m, l, acc = carry
        offn = pl.multiple_of(n * t, t)
        s = lax.dot_general(q, k_ref[pl.ds(offn, t), :], _NT, preferred_element_type=F32)
        onehot = (pick_row == n).astype(BF16)
        chosen = jnp.dot(sel_q, onehot, preferred_element_type=F32)
        s = jnp.where(chosen > 0.5, s, MASKED)
        m_new = jnp.maximum(m, jnp.max(s, axis=-1, keepdims=True))
        alpha = jnp.exp(m - m_new)
        p = jnp.exp(s - m_new)
        l = alpha * l + jnp.sum(p, axis=-1, keepdims=True)
        acc = alpha * acc + jnp.dot(p.astype(BF16), v_ref[pl.ds(offn, t), :], preferred_element_type=F32)
        return m_new, l, acc

    _, l, acc = lax.fori_loop(0, qi, past, (m, l, acc))
    o_ref[...] = (acc / l).astype(o_ref.dtype)


def _moba_attn(qkv, B, S):
    t = MOBA_BLOCK
    nq = S // t
    H = N_HEADS
    return pl.pallas_call(
        functools.partial(_moba_kernel, nblk=nq),
        grid=(B, H, nq),
        in_specs=[pl.BlockSpec((t, LANES), lambda b, h, i: (b * nq + i, h)),
                  pl.BlockSpec((S, LANES), lambda b, h, i: (b, H + h)),
                  pl.BlockSpec((S, LANES), lambda b, h, i: (b, 2 * H + h))],
        out_specs=pl.BlockSpec((t, LANES), lambda b, h, i: (b * nq + i, h)),
        out_shape=jax.ShapeDtypeStruct((B * S, D_MODEL), BF16),
        scratch_shapes=[pltpu.VMEM((LANES, LANES), F32)],
        compiler_params=_params("parallel", "parallel", "arbitrary"),
        name="moba_attn",
    )(qkv, qkv, qkv)


def _qkv_proj(h, w_in):
    return _proj(h, w_in.astype(BF16), BF16, scaled_cols=D_MODEL, scale=float(HEAD_DIM ** -0.5))


def _sb_mixer(h, B, S, w_in):
    return _sb_attn(_qkv_proj(h, w_in), B, S)


def _moba_mixer(h, B, S, w_in):
    return _moba_attn(_qkv_proj(h, w_in), B, S)


def kernel(x, mla_w_in, mla_q_norm, mla_kv_norm, mla_w_uq, mla_w_ukv, mla_w_o, hgrn_w_in, hgrn_lb_logits, hgrn_o_norm, hgrn_w_o, sb_w_in, sb_w_o, moba_w_in, moba_w_o, ln_g, ln_b, mlp_w1, mlp_w2):
    B, S, D = x.shape
    assert D == D_MODEL and S % MOBA_BLOCK == 0 and S % 512 == 0
    h = x.reshape(B * S, D)
    n_mixers = 4
    for i in range(DEPTH):
        kind, slot = i % n_mixers, i // n_mixers
        if kind == 0:
            o = _mla_mixer(h, B, S, mla_w_in[slot], mla_q_norm[slot], mla_kv_norm[slot],
                           mla_w_uq[slot], mla_w_ukv[slot])
            w_o = mla_w_o[slot]
        elif kind == 1:
            proj = _proj(h, hgrn_w_in[slot].astype(BF16), F32)
            o = _hgrn_mixer_core(proj, hgrn_lb_logits, hgrn_o_norm[slot], B, S, i)
            w_o = hgrn_w_o[slot]
        elif kind == 2:
            o = _sb_mixer(h, B, S, sb_w_in[slot])
            w_o = sb_w_o[slot]
        else:
            o = _moba_mixer(h, B, S, moba_w_in[slot])
            w_o = moba_w_o[slot]
        h = _proj_res_ln(o, w_o.astype(BF16), h, ln_g[i, 0], ln_b[i, 0])
        a = _proj(h, mlp_w1[i].astype(BF16), BF16, act="relu2")
        h = _proj_res_ln(a, mlp_w2[i].astype(BF16), h, ln_g[i, 1], ln_b[i, 1])
    return h.reshape(B, S, D)
```

```python
import functools

import jax
import jax.numpy as jnp
from jax import lax
from jax.experimental import pallas as pl
from jax.experimental.pallas import tpu as pltpu

F32 = jnp.float32
BF16 = jnp.bfloat16

D_MODEL = 2048
DEPTH = 4
N_HEADS = 16
HEAD_DIM = 128
MLA_Q_LORA = 512
MLA_KV_LORA = 512
MLA_NOPE = 128
MLA_ROPE = 64
MLA_V = 128
ROPE_THETA = 10000.0
HGRN_CHUNK = 64
HGRN_SUB = 8
MOBA_BLOCK = 256
MOBA_TOPK = 3
ALPHA = float((2 * DEPTH) ** 0.25)
LN_EPS = 1e-5
RMS_EPS = 1e-6

V7X_VMEM_BYTES = 64 * 1024 * 1024
VMEM_LIMIT = V7X_VMEM_BYTES - 8 * 1024 * 1024
LANES = 128
MASKED = -1e30

_NT = (((1,), (1,)), ((), ()))


def _params(*sem):
    return pltpu.CompilerParams(dimension_semantics=sem, vmem_limit_bytes=VMEM_LIMIT)


def _layer_norm_rows(y, g, b):
    mu = jnp.mean(y, axis=-1, keepdims=True)
    d = y - mu
    var = jnp.mean(d * d, axis=-1, keepdims=True)
    return d * lax.rsqrt(var + LN_EPS) * g + b


def _rms_rows(x, g):
    return x * lax.rsqrt(jnp.mean(x * x, axis=-1, keepdims=True) + RMS_EPS) * g


def _proj_kernel(x_ref, w_ref, o_ref, *, act, scaled_tiles, scale):
    acc = jnp.dot(x_ref[...].astype(BF16), w_ref[...], preferred_element_type=F32)
    if act == "relu2":
        r = jnp.maximum(acc, 0.0)
        acc = r * r
    if scaled_tiles:
        acc = acc * jnp.where(pl.program_id(1) < scaled_tiles, scale, 1.0)
    o_ref[...] = acc.astype(o_ref.dtype)


def _proj(x, w, out_dtype, *, tm=512, tn=1024, act=None, scaled_cols=0, scale=1.0):
    M, K = x.shape
    N = w.shape[1]
    tn = min(tn, N)
    assert scaled_cols % tn == 0
    return pl.pallas_call(
        functools.partial(_proj_kernel, act=act, scaled_tiles=scaled_cols // tn, scale=scale),
        grid=(M // tm, N // tn),
        in_specs=[pl.BlockSpec((tm, K), lambda i, j: (i, 0)),
                  pl.BlockSpec((K, tn), lambda i, j: (0, j))],
        out_specs=pl.BlockSpec((tm, tn), lambda i, j: (i, j)),
        out_shape=jax.ShapeDtypeStruct((M, N), out_dtype),
        compiler_params=_params("parallel", "parallel"),
        name="proj",
    )(x, w)


def _proj_res_ln_kernel(x_ref, w_ref, h_ref, g_ref, b_ref, o_ref, acc_ref, *, nk):
    k = pl.program_id(1)
    part = jnp.dot(x_ref[...], w_ref[...], preferred_element_type=F32)

    def finish(y):
        o_ref[...] = _layer_norm_rows(ALPHA * h_ref[...] + y, g_ref[...], b_ref[...])

    if nk == 1:
        finish(part)
    else:
        @pl.when(k == 0)
        def _():
            acc_ref[...] = part

        @pl.when(jnp.logical_and(k > 0, k < nk - 1))
        def _():
            acc_ref[...] += part

        @pl.when(k == nk - 1)
        def _():
            finish(acc_ref[...] + part)


def _proj_res_ln(x, w, h, g, b, *, tm=512, tk=2048):
    M, K = x.shape
    N = w.shape[1]
    nk = K // tk
    return pl.pallas_call(
        functools.partial(_proj_res_ln_kernel, nk=nk),
        grid=(M // tm, nk),
        in_specs=[pl.BlockSpec((tm, tk), lambda i, k: (i, k)),
                  pl.BlockSpec((tk, N), lambda i, k: (k, 0)),
                  pl.BlockSpec((tm, N), lambda i, k: (i, 0)),
                  pl.BlockSpec((1, N), lambda i, k: (0, 0)),
                  pl.BlockSpec((1, N), lambda i, k: (0, 0))],
        out_specs=pl.BlockSpec((tm, N), lambda i, k: (i, 0)),
        out_shape=jax.ShapeDtypeStruct((M, N), F32),
        scratch_shapes=[pltpu.VMEM((tm, N), F32)],
        compiler_params=_params("parallel", "arbitrary"),
        name="proj_res_ln",
    )(x, w, h, g.reshape(1, N), b.reshape(1, N))


def _mla_in_kernel(h_ref, w_ref, qg_ref, kvg_ref, ct_ref, st_ref, cq_ref, ckv_ref, kr_ref):
    acc = jnp.dot(h_ref[...].astype(BF16), w_ref[...], preferred_element_type=F32)
    ql, kvl = MLA_Q_LORA, MLA_KV_LORA
    cq_ref[...] = _rms_rows(acc[:, :ql], qg_ref[...]).astype(cq_ref.dtype)
    ckv_ref[...] = _rms_rows(acc[:, ql:ql + kvl], kvg_ref[...]).astype(ckv_ref.dtype)
    a = acc[:, ql + kvl:ql + kvl + LANES]
    a_sw = acc[:, ql + kvl + LANES:]
    kr_ref[...] = (a * ct_ref[...] + a_sw * st_ref[...]).astype(kr_ref.dtype)


def _mla_in(h, w_ext, qg, kvg, ct, st, S, *, tm=512):
    M, K = h.shape
    N = w_ext.shape[1]
    ns = S // tm
    return pl.pallas_call(
        _mla_in_kernel,
        grid=(M // tm,),
        in_specs=[pl.BlockSpec((tm, K), lambda i: (i, 0)),
                  pl.BlockSpec((K, N), lambda i: (0, 0)),
                  pl.BlockSpec((1, MLA_Q_LORA), lambda i: (0, 0)),
                  pl.BlockSpec((1, MLA_KV_LORA), lambda i: (0, 0)),
                  pl.BlockSpec((tm, LANES), lambda i: (i % ns, 0)),
                  pl.BlockSpec((tm, LANES), lambda i: (i % ns, 0))],
        out_specs=[pl.BlockSpec((tm, MLA_Q_LORA), lambda i: (i, 0)),
                   pl.BlockSpec((tm, MLA_KV_LORA), lambda i: (i, 0)),
                   pl.BlockSpec((tm, LANES), lambda i: (i, 0))],
        out_shape=[jax.ShapeDtypeStruct((M, MLA_Q_LORA), BF16),
                   jax.ShapeDtypeStruct((M, MLA_KV_LORA), BF16),
                   jax.ShapeDtypeStruct((M, LANES), BF16)],
        compiler_params=_params("parallel"),
        name="mla_in",
    )(h, w_ext, qg.reshape(1, -1), kvg.reshape(1, -1), ct, st)


def _mla_uq_kernel(cq_ref, wm_ref, ws_ref, ct_ref, st_ref, q_ref, *, heads, scale):
    x = cq_ref[...]
    a = jnp.dot(x, wm_ref[...], preferred_element_type=F32)
    a_sw = jnp.dot(x, ws_ref[...], preferred_element_type=F32)
    ct = ct_ref[...]
    st = st_ref[...]
    for hh in range(heads):
        lo = hh * 2 * LANES
        q_ref[:, lo:lo + LANES] = (a[:, lo:lo + LANES] * scale).astype(q_ref.dtype)
        rot = a[:, lo + LANES:lo + 2 * LANES] * ct + a_sw[:, hh * LANES:(hh + 1) * LANES] * st
        q_ref[:, lo + LANES:lo + 2 * LANES] = (rot * scale).astype(q_ref.dtype)


def _mla_uq(cq, w_main, w_sw, ct, st, S, scale, *, tm=512, heads_per_step=4):
    M, K = cq.shape
    hp = heads_per_step
    ns = S // tm
    return pl.pallas_call(
        functools.partial(_mla_uq_kernel, heads=hp, scale=scale),
        grid=(M // tm, N_HEADS // hp),
        in_specs=[pl.BlockSpec((tm, K), lambda i, j: (i, 0)),
                  pl.BlockSpec((K, hp * 2 * LANES), lambda i, j: (0, j)),
                  pl.BlockSpec((K, hp * LANES), lambda i, j: (0, j)),
                  pl.BlockSpec((tm, LANES), lambda i, j: (i % ns, 0)),
                  pl.BlockSpec((tm, LANES), lambda i, j: (i % ns, 0))],
        out_specs=pl.BlockSpec((tm, hp * 2 * LANES), lambda i, j: (i, j)),
        out_shape=jax.ShapeDtypeStruct((M, N_HEADS * 2 * LANES), BF16),
        compiler_params=_params("parallel", "parallel"),
        name="mla_uq",
    )(cq, w_main, w_sw, ct, st)


def _causal_sweep(qi, sa, sb, scores, update, init):
    sa[...] = scores(0)

    def pair(p, state):
        k0 = 2 * p
        sb[...] = scores(k0 + 1)
        state = update(k0, sa[...], state, False)
        sa[...] = scores(k0 + 2)
        return update(k0 + 1, sb[...], state, False)

    state = lax.fori_loop(0, qi // 2, pair, init)

    def odd(state):
        sb[...] = scores(qi)
        state = update(qi - 1, sa[...], state, False)
        return update(qi, sb[...], state, True)

    def even(state):
        return update(qi, sa[...], state, True)

    return lax.cond(qi % 2 == 1, odd, even, state)


def _softmax_update(s, state, v):
    m, l, acc = state
    m_new = jnp.maximum(m, jnp.max(s, axis=-1, keepdims=True))
    alpha = jnp.exp(m - m_new)
    p = jnp.exp(s - m_new)
    l = alpha * l + jnp.sum(p, axis=-1, keepdims=True)
    acc = alpha * acc + jnp.dot(p.astype(BF16), v, preferred_element_type=F32)
    return m_new, l, acc


def _softmax_init(t, dv):
    return (jnp.full((t, 1), MASKED, F32), jnp.zeros((t, 1), F32), jnp.zeros((t, dv), F32))


def _mla_attn_kernel(q_ref, kn_ref, kr_ref, v_ref, o_ref, sa_ref, sb_ref, *, t):
    qi = pl.program_id(2)
    q = q_ref[...]
    row = lax.broadcasted_iota(jnp.int32, (t, t), 0)
    col = lax.broadcasted_iota(jnp.int32, (t, t), 1)

    def scores(ki):
        off = pl.multiple_of(ki * t, t)
        k = jnp.concatenate([kn_ref[pl.ds(off, t), :], kr_ref[pl.ds(off, t), :]], axis=1)
        return lax.dot_general(q, k, _NT, preferred_element_type=F32)

    def update(ki, s, state, diagonal):
        if diagonal:
            s = jnp.where(col <= row, s, -jnp.inf)
        off = pl.multiple_of(ki * t, t)
        return _softmax_update(s, state, v_ref[pl.ds(off, t), :])

    _, l, acc = _causal_sweep(qi, sa_ref, sb_ref, scores, update, _softmax_init(t, MLA_V))
    o_ref[...] = (acc / l).astype(o_ref.dtype)


def _mla_attn(q, kv, kr, B, S, *, t=512):
    nq = S // t
    H = N_HEADS
    return pl.pallas_call(
        functools.partial(_mla_attn_kernel, t=t),
        grid=(B, H, nq),
        in_specs=[pl.BlockSpec((t, 2 * LANES), lambda b, h, i: (b * nq + i, h)),
                  pl.BlockSpec((S, LANES), lambda b, h, i: (b, h)),
                  pl.BlockSpec((S, LANES), lambda b, h, i: (b, 0)),
                  pl.BlockSpec((S, LANES), lambda b, h, i: (b, H + h))],
        out_specs=pl.BlockSpec((t, LANES), lambda b, h, i: (b * nq + i, h)),
        out_shape=jax.ShapeDtypeStruct((B * S, H * MLA_V), BF16),
        scratch_shapes=[pltpu.VMEM((t, t), F32), pltpu.VMEM((t, t), F32)],
        compiler_params=_params("parallel", "parallel", "arbitrary"),
        name="mla_attn",
    )(q, kv, kr, kv)


def _rope_tables(S):
    half = MLA_ROPE // 2
    inv = 1.0 / (ROPE_THETA ** (jnp.arange(0, MLA_ROPE, 2, dtype=F32) / MLA_ROPE))
    ang = jnp.arange(S, dtype=F32)[:, None] * inv[None, :]
    cos, sin = jnp.cos(ang), jnp.sin(ang)
    zeros = jnp.zeros((S, LANES - 2 * half), F32)
    return (jnp.concatenate([cos, cos, zeros], axis=1), jnp.concatenate([-sin, sin, zeros], axis=1))


def _mla_weights(w_in, w_uq, w_ukv):
    D = w_in.shape[0]
    half = MLA_ROPE // 2
    base = MLA_Q_LORA + MLA_KV_LORA
    x1, x2 = w_in[:, base:base + half], w_in[:, base + half:base + 2 * half]
    pad = jnp.zeros((D, LANES - 2 * half), w_in.dtype)
    w_in_ext = jnp.concatenate([w_in[:, :base], x1, x2, pad, x2, x1, pad], axis=1).astype(BF16)

    wq = w_uq.reshape(MLA_Q_LORA, N_HEADS, MLA_NOPE + MLA_ROPE)
    nope, r1, r2 = wq[..., :MLA_NOPE], wq[..., MLA_NOPE:MLA_NOPE + half], wq[..., MLA_NOPE + half:]
    padq = jnp.zeros((MLA_Q_LORA, N_HEADS, LANES - 2 * half), w_uq.dtype)
    w_main = jnp.concatenate([nope, r1, r2, padq], axis=-1).reshape(MLA_Q_LORA, N_HEADS * 2 * LANES).astype(BF16)
    w_sw = jnp.concatenate([r2, r1, padq], axis=-1).reshape(MLA_Q_LORA, N_HEADS * LANES).astype(BF16)

    wkv = w_ukv.reshape(MLA_KV_LORA, N_HEADS, MLA_NOPE + MLA_V)
    w_kv = jnp.concatenate([wkv[..., :MLA_NOPE].reshape(MLA_KV_LORA, -1),
                            wkv[..., MLA_NOPE:].reshape(MLA_KV_LORA, -1)], axis=1).astype(BF16)
    return w_in_ext, w_main, w_sw, w_kv


def _mla_mixer(h, B, S, w_in, q_norm, kv_norm, w_uq, w_ukv):
    w_in_ext, w_main, w_sw, w_kv = _mla_weights(w_in, w_uq, w_ukv)
    ct, st = _rope_tables(S)
    cq, ckv, kr = _mla_in(h, w_in_ext, q_norm, kv_norm, ct, st, S)
    scale = float((MLA_NOPE + MLA_ROPE) ** -0.5)
    q = _mla_uq(cq, w_main, w_sw, ct, st, S, scale)
    kv = _proj(ckv, w_kv, BF16)
    return _mla_attn(q, kv, kr, B, S)


def _hgrn_kernel(q_ref, f_ref, i_ref, g_ref, lbl_ref, on_ref, o_ref, state_ref, *, layer, tile, chunk):
    C, SB = chunk, HGRN_SUB
    nb = C // SB

    @pl.when(pl.program_id(2) == 0)
    def _():
        state_ref[...] = jnp.zeros_like(state_ref)

    lg = lbl_ref[...]
    e = jnp.exp(lg - jnp.max(lg, axis=0, keepdims=True))
    p = e / jnp.sum(e, axis=0, keepdims=True)
    cs = p[0:1]
    for r in range(1, layer + 1):
        cs = cs + p[r:r + 1]
    lb = cs - p[0:1]
    log_lb = jnp.log(lb)
    log1m_lb = jnp.log1p(-lb)
    one_m_lb = 1.0 - lb
    onorm = on_ref[...]

    row = lax.broadcasted_iota(jnp.int32, (C, C), 0)
    col = lax.broadcasted_iota(jnp.int32, (C, C), 1)
    tri = (col <= row).astype(F32)
    bdiff = row // SB - col // SB
    tio = lax.broadcasted_iota(jnp.int32, (nb, SB, LANES), 1)

    def chunk_step(c, carry):
        off = pl.multiple_of(c * C, C)
        q = q_ref[pl.ds(off, C), :]
        fp = f_ref[pl.ds(off, C), :]
        v = i_ref[pl.ds(off, C), :]
        g = g_ref[pl.ds(off, C), :]

        ls = jnp.minimum(fp, 0.0) - jnp.log(1.0 + jnp.exp(-jnp.abs(fp)))
        cc = log1m_lb + ls
        lf = jnp.maximum(log_lb, cc) + jnp.log(1.0 + jnp.exp(-jnp.abs(log_lb - cc)))
        kk = one_m_lb * jax.nn.sigmoid(-fp)

        b = jnp.dot(tri, lf, precision=lax.Precision.HIGHEST, preferred_element_type=F32)
        bend = b[C - 1:C, :]

        st_t = state_ref[...]
        qe = q * jnp.exp(b)
        o = lax.dot_general(qe.astype(BF16), st_t.astype(BF16), _NT, preferred_element_type=F32)
        kd = kk * jnp.exp(bend - b)
        state_ref[...] = st_t * jnp.exp(bend) + jnp.dot(v.T.astype(BF16), kd.astype(BF16),
                                                       preferred_element_type=F32)

        b3 = b.reshape(nb, SB, LANES)
        q3 = q.reshape(nb, SB, LANES)
        k3 = kk.reshape(nb, SB, LANES)
        v3 = v.reshape(nb, SB, LANES)

        r = b3[:, 0:1, :]
        r_next = jnp.concatenate([r[1:], bend.reshape(1, 1, LANES)], axis=0)
        kt = (k3 * jnp.exp(r_next - b3)).reshape(C, LANES)
        gdec = jnp.exp(r_next - r)
        ql = q3 * jnp.exp(b3 - r)
        levels = [ql]
        for lvl in range(1, nb - 1):
            fac = jnp.concatenate([jnp.zeros((lvl, 1, LANES), F32), gdec[:nb - lvl]], axis=0)
            ql = ql * fac
            levels.append(ql)
        qs = jnp.concatenate([x.reshape(C, LANES) for x in levels], axis=0).astype(BF16)
        rl = lax.dot_general(qs, kt.astype(BF16), _NT, preferred_element_type=F32)
        a = jnp.zeros((C, C), F32)
        for lvl in range(1, nb):
            a = a + jnp.where(bdiff == lvl, rl[(lvl - 1) * C:lvl * C, :], 0.0)
        o = o + jnp.dot(a.astype(BF16), v.astype(BF16), preferred_element_type=F32)

        od = jnp.zeros((nb, SB, LANES), F32)
        for s in range(SB):
            dec = jnp.exp(jnp.minimum(b3 - b3[:, s:s + 1, :], 0.0))
            pr = jnp.where(tio >= s, q3 * k3[:, s:s + 1, :] * dec, 0.0)
            od = od + jnp.sum(pr, axis=-1, keepdims=True) * v3[:, s:s + 1, :]
        o = o + od.reshape(C, LANES)

        y = _rms_rows(o, onorm)
        o_ref[pl.ds(off, C), :] = (y * (g * jax.nn.sigmoid(g))).astype(o_ref.dtype)
        return carry

    lax.fori_loop(0, tile // C, chunk_step, 0, unroll=2)


def _hgrn_mixer_core(proj, lb_logits, o_norm, B, S, layer, *, tile=512):
    nt = S // tile
    H = N_HEADS
    blk = lambda sec: pl.BlockSpec((tile, LANES), lambda b, h, t, sec=sec: (b * nt + t, sec * H + h))
    return pl.pallas_call(
        functools.partial(_hgrn_kernel, layer=layer, tile=tile, chunk=HGRN_CHUNK),
        grid=(B, H, nt),
        in_specs=[blk(0), blk(1), blk(2), blk(3),
                  pl.BlockSpec((DEPTH, LANES), lambda b, h, t: (0, h)),
                  pl.BlockSpec((1, LANES), lambda b, h, t: (0, h))],
        out_specs=pl.BlockSpec((tile, LANES), lambda b, h, t: (b * nt + t, h)),
        out_shape=jax.ShapeDtypeStruct((B * S, D_MODEL), BF16),
        scratch_shapes=[pltpu.VMEM((LANES, LANES), F32)],
        compiler_params=_params("parallel", "parallel", "arbitrary"),
        name="hgrn",
    )(proj, proj, proj, proj, lb_logits, o_norm.reshape(1, -1))


SB_CUMSUM_BLOCK = 256


def _sb_kernel(q_ref, k_ref, v_ref, o_ref, sa_ref, sb_ref, *, t):
    cb = SB_CUMSUM_BLOCK
    qi = pl.program_id(2)
    q = q_ref[...]
    row = lax.broadcasted_iota(jnp.int32, (t, t), 0)
    col = lax.broadcasted_iota(jnp.int32, (t, t), 1)
    strict = col < row
    jj = lax.broadcasted_iota(jnp.int32, (2 * cb, cb), 0) % cb
    ss = lax.broadcasted_iota(jnp.int32, (2 * cb, cb), 1)
    later2 = (jj > ss).astype(BF16)

    def scores(ki):
        off = pl.multiple_of(ki * t, t)
        return lax.dot_general(q, k_ref[pl.ds(off, t), :], _NT, preferred_element_type=F32)

    def update(ki, z, acc, diagonal):
        l1m = -(jnp.maximum(z, 0.0) + jnp.log(1.0 + jnp.exp(-jnp.abs(z))))
        if diagonal:
            l1m = jnp.where(strict, l1m, 0.0)
        parts = []
        tail = jnp.zeros((t, 1), F32)
        for j in reversed(range(t // cb)):
            lj = l1m[:, j * cb:(j + 1) * cb]
            hi = lj.astype(BF16)
            lo = (lj - hi.astype(F32)).astype(BF16)
            parts.insert(0, jnp.dot(jnp.concatenate([hi, lo], axis=1), later2,
                                    preferred_element_type=F32) + tail)
            tail = tail + jnp.sum(lj, axis=-1, keepdims=True)
        after = jnp.concatenate(parts, axis=1)
        w = jnp.exp(z + l1m + after)
        if diagonal:
            w = jnp.where(strict, w, 0.0)
        off = pl.multiple_of(ki * t, t)
        return acc * jnp.exp(tail) + jnp.dot(w.astype(BF16), v_ref[pl.ds(off, t), :],
                                             preferred_element_type=F32)

    acc = _causal_sweep(qi, sa_ref, sb_ref, scores, update, jnp.zeros((t, HEAD_DIM), F32))
    o_ref[...] = acc.astype(o_ref.dtype)


def _sb_attn(qkv, B, S, *, t=512):
    nq = S // t
    H = N_HEADS
    return pl.pallas_call(
        functools.partial(_sb_kernel, t=t),
        grid=(B, H, nq),
        in_specs=[pl.BlockSpec((t, LANES), lambda b, h, i: (b * nq + i, h)),
                  pl.BlockSpec((S, LANES), lambda b, h, i: (b, H + h)),
                  pl.BlockSpec((S, LANES), lambda b, h, i: (b, 2 * H + h))],
        out_specs=pl.BlockSpec((t, LANES), lambda b, h, i: (b * nq + i, h)),
        out_shape=jax.ShapeDtypeStruct((B * S, D_MODEL), BF16),
        scratch_shapes=[pltpu.VMEM((t, t), F32), pltpu.VMEM((t, t), F32)],
        compiler_params=_params("parallel", "parallel", "arbitrary"),
        name="sb_attn",
    )(qkv, qkv, qkv)


def _moba_kernel(q_ref, k_ref, v_ref, o_ref, kmean_ref, sa_ref, sb_ref, *, nblk, t):
    bpt = t // MOBA_BLOCK
    qi = pl.program_id(2)

    @pl.when(qi == 0)
    def _():
        kmean_ref[...] = jnp.zeros_like(kmean_ref)
        kmean_ref[0:nblk, :] = jnp.mean(k_ref[...].astype(F32).reshape(nblk, MOBA_BLOCK, LANES), axis=1)

    q = q_ref[...]

    gate = lax.dot_general(kmean_ref[...], q.astype(F32), _NT, precision=lax.Precision.HIGHEST,
                           preferred_element_type=F32)
    blk = lax.broadcasted_iota(jnp.int32, (LANES, t), 0)
    tile_blk = lax.broadcasted_iota(jnp.int32, (LANES, t), 1) // MOBA_BLOCK
    neg_inf = jnp.float32(-jnp.inf)
    gate = jnp.where(blk < qi * bpt + tile_blk, gate, neg_inf)
    sel = jnp.zeros((LANES, t), F32)
    for _ in range(MOBA_TOPK):
        mx = jnp.max(gate, axis=0, keepdims=True)
        first = jnp.min(jnp.where(gate == mx, blk, LANES), axis=0, keepdims=True)
        pick = jnp.logical_and(blk == first, mx > neg_inf)
        sel = jnp.where(pick, 1.0, sel)
        gate = jnp.where(pick, neg_inf, gate)
    sel_q = sel.T.astype(BF16)

    row = lax.broadcasted_iota(jnp.int32, (t, t), 0)
    col = lax.broadcasted_iota(jnp.int32, (t, t), 1)
    own_causal = jnp.logical_and(row // MOBA_BLOCK == col // MOBA_BLOCK, col <= row)

    def scores(kc):
        off = pl.multiple_of(kc * t, t)
        return lax.dot_general(q, k_ref[pl.ds(off, t), :], _NT, preferred_element_type=F32)

    def update(kc, s, state, diagonal):
        spread = (blk == kc * bpt + tile_blk).astype(BF16)
        chosen = jnp.dot(sel_q, spread, preferred_element_type=F32) > 0.5
        if diagonal:
            chosen = jnp.logical_or(chosen, own_causal)
        s = jnp.where(chosen, s, -jnp.inf)
        off = pl.multiple_of(kc * t, t)
        return _softmax_update(s, state, v_ref[pl.ds(off, t), :])

    _, l, acc = _causal_sweep(qi, sa_ref, sb_ref, scores, update, _softmax_init(t, HEAD_DIM))
    o_ref[...] = (acc / l).astype(o_ref.dtype)


def _moba_attn(qkv, B, S, *, t=512):
    nq = S // t
    H = N_HEADS
    return pl.pallas_call(
        functools.partial(_moba_kernel, nblk=S // MOBA_BLOCK, t=t),
        grid=(B, H, nq),
        in_specs=[pl.BlockSpec((t, LANES), lambda b, h, i: (b * nq + i, h)),
                  pl.BlockSpec((S, LANES), lambda b, h, i: (b, H + h)),
                  pl.BlockSpec((S, LANES), lambda b, h, i: (b, 2 * H + h))],
        out_specs=pl.BlockSpec((t, LANES), lambda b, h, i: (b * nq + i, h)),
        out_shape=jax.ShapeDtypeStruct((B * S, D_MODEL), BF16),
        scratch_shapes=[pltpu.VMEM((LANES, LANES), F32), pltpu.VMEM((t, t), F32), pltpu.VMEM((t, t), F32)],
        compiler_params=_params("parallel", "parallel", "arbitrary"),
        name="moba_attn",
    )(qkv, qkv, qkv)


def _qkv_proj(h, w_in):
    return _proj(h, w_in.astype(BF16), BF16, scaled_cols=D_MODEL, scale=float(HEAD_DIM ** -0.5))


def _sb_mixer(h, B, S, w_in):
    return _sb_attn(_qkv_proj(h, w_in), B, S)


def _moba_mixer(h, B, S, w_in):
    return _moba_attn(_qkv_proj(h, w_in), B, S)


def kernel(x, mla_w_in, mla_q_norm, mla_kv_norm, mla_w_uq, mla_w_ukv, mla_w_o, hgrn_w_in, hgrn_lb_logits, hgrn_o_norm, hgrn_w_o, sb_w_in, sb_w_o, moba_w_in, moba_w_o, ln_g, ln_b, mlp_w1, mlp_w2):
    B, S, D = x.shape
    assert D == D_MODEL and S % MOBA_BLOCK == 0 and S % 512 == 0
    h = x.reshape(B * S, D)
    n_mixers = 4
    for i in range(DEPTH):
        kind, slot = i % n_mixers, i // n_mixers
        if kind == 0:
            o = _mla_mixer(h, B, S, mla_w_in[slot], mla_q_norm[slot], mla_kv_norm[slot],
                           mla_w_uq[slot], mla_w_ukv[slot])
            w_o = mla_w_o[slot]
        elif kind == 1:
            proj = _proj(h, hgrn_w_in[slot].astype(BF16), F32)
            o = _hgrn_mixer_core(proj, hgrn_lb_logits, hgrn_o_norm[slot], B, S, i)
            w_o = hgrn_w_o[slot]
        elif kind == 2:
            o = _sb_mixer(h, B, S, sb_w_in[slot])
            w_o = sb_w_o[slot]
        else:
            o = _moba_mixer(h, B, S, moba_w_in[slot])
            w_o = moba_w_o[slot]
        h = _proj_res_ln(o, w_o.astype(BF16), h, ln_g[i, 0], ln_b[i, 0])
        a = _proj(h, mlp_w1[i].astype(BF16), BF16, act="relu2")
        h = _proj_res_ln(a, mlp_w2[i].astype(BF16), h, ln_g[i, 1], ln_b[i, 1])
    return h.reshape(B, S, D)
```

```python
import functools

import jax
import jax.numpy as jnp
from jax import lax
from jax.experimental import pallas as pl
from jax.experimental.pallas import tpu as pltpu

F32 = jnp.float32
BF16 = jnp.bfloat16

D_MODEL = 2048
DEPTH = 4
N_HEADS = 16
HEAD_DIM = 128
MLA_Q_LORA = 512
MLA_KV_LORA = 512
MLA_NOPE = 128
MLA_ROPE = 64
MLA_V = 128
ROPE_THETA = 10000.0
HGRN_CHUNK = 64
HGRN_SUB = 8
HGRN_GROUP = 8
MOBA_BLOCK = 256
MOBA_TOPK = 3
ALPHA = float((2 * DEPTH) ** 0.25)
LN_EPS = 1e-5
RMS_EPS = 1e-6

V7X_VMEM_BYTES = 64 * 1024 * 1024
VMEM_LIMIT = V7X_VMEM_BYTES - 8 * 1024 * 1024
LANES = 128
MASKED = -1e30

_NT = (((1,), (1,)), ((), ()))


def _params(*sem):
    return pltpu.CompilerParams(dimension_semantics=sem, vmem_limit_bytes=VMEM_LIMIT)


def _layer_norm_rows(y, g, b):
    mu = jnp.mean(y, axis=-1, keepdims=True)
    d = y - mu
    var = jnp.mean(d * d, axis=-1, keepdims=True)
    return d * lax.rsqrt(var + LN_EPS) * g + b


def _rms_rows(x, g):
    return x * lax.rsqrt(jnp.mean(x * x, axis=-1, keepdims=True) + RMS_EPS) * g


def _proj_kernel(x_ref, w_ref, o_ref, wbf_ref, *, act, scaled_tiles, scale):
    @pl.when(pl.program_id(1) == 0)
    def _():
        wbf_ref[...] = w_ref[...].astype(BF16)

    acc = jnp.dot(x_ref[...].astype(BF16), wbf_ref[...], preferred_element_type=F32)
    if act == "relu2":
        r = jnp.maximum(acc, 0.0)
        acc = r * r
    if scaled_tiles:
        acc = acc * jnp.where(pl.program_id(0) < scaled_tiles, scale, 1.0)
    o_ref[...] = acc.astype(o_ref.dtype)


def _proj(x, w, out_dtype, *, tm=1024, tn=1024, act=None, scaled_cols=0, scale=1.0):
    M, K = x.shape
    N = w.shape[1]
    tm, tn = min(tm, M), min(tn, N)
    assert scaled_cols % tn == 0 and M % tm == 0 and N % tn == 0
    return pl.pallas_call(
        functools.partial(_proj_kernel, act=act, scaled_tiles=scaled_cols // tn, scale=scale),
        grid=(N // tn, M // tm),
        in_specs=[pl.BlockSpec((tm, K), lambda j, i: (i, 0)),
                  pl.BlockSpec((K, tn), lambda j, i: (0, j))],
        out_specs=pl.BlockSpec((tm, tn), lambda j, i: (i, j)),
        out_shape=jax.ShapeDtypeStruct((M, N), out_dtype),
        scratch_shapes=[pltpu.VMEM((K, tn), BF16)],
        compiler_params=_params("parallel", "arbitrary"),
        name="proj",
    )(x, w)


def _proj_res_ln_kernel(x_ref, w_ref, h_ref, g_ref, b_ref, o_ref, ob_ref, acc_ref, *, nk):
    k = pl.program_id(1)
    part = jnp.dot(x_ref[...], w_ref[...], preferred_element_type=F32)

    def finish(y):
        out = _layer_norm_rows(ALPHA * h_ref[...] + y, g_ref[...], b_ref[...])
        o_ref[...] = out
        ob_ref[...] = out.astype(BF16)

    if nk == 1:
        finish(part)
    else:
        @pl.when(k == 0)
        def _():
            acc_ref[...] = part

        @pl.when(jnp.logical_and(k > 0, k < nk - 1))
        def _():
            acc_ref[...] += part

        @pl.when(k == nk - 1)
        def _():
            finish(acc_ref[...] + part)


def _proj_res_ln(x, w, h, g, b, *, tm=512, tk=2048):
    M, K = x.shape
    N = w.shape[1]
    nk = K // tk
    return pl.pallas_call(
        functools.partial(_proj_res_ln_kernel, nk=nk),
        grid=(M // tm, nk),
        in_specs=[pl.BlockSpec((tm, tk), lambda i, k: (i, k)),
                  pl.BlockSpec((tk, N), lambda i, k: (k, 0)),
                  pl.BlockSpec((tm, N), lambda i, k: (i, 0)),
                  pl.BlockSpec((1, N), lambda i, k: (0, 0)),
                  pl.BlockSpec((1, N), lambda i, k: (0, 0))],
        out_specs=[pl.BlockSpec((tm, N), lambda i, k: (i, 0)),
                   pl.BlockSpec((tm, N), lambda i, k: (i, 0))],
        out_shape=[jax.ShapeDtypeStruct((M, N), F32), jax.ShapeDtypeStruct((M, N), BF16)],
        scratch_shapes=[pltpu.VMEM((tm, N), F32)],
        compiler_params=_params("parallel", "arbitrary"),
        name="proj_res_ln",
    )(x, w, h, g.reshape(1, N), b.reshape(1, N))


def _mla_in_kernel(h_ref, w_ref, qg_ref, kvg_ref, ct_ref, st_ref, cq_ref, ckv_ref, kr_ref):
    acc = jnp.dot(h_ref[...].astype(BF16), w_ref[...], preferred_element_type=F32)
    ql, kvl = MLA_Q_LORA, MLA_KV_LORA
    cq_ref[...] = _rms_rows(acc[:, :ql], qg_ref[...]).astype(cq_ref.dtype)
    ckv_ref[...] = _rms_rows(acc[:, ql:ql + kvl], kvg_ref[...]).astype(ckv_ref.dtype)
    a = acc[:, ql + kvl:ql + kvl + LANES]
    a_sw = acc[:, ql + kvl + LANES:]
    kr_ref[...] = (a * ct_ref[...] + a_sw * st_ref[...]).astype(kr_ref.dtype)


def _mla_in(h, w_ext, qg, kvg, ct, st, S, *, tm=512):
    M, K = h.shape
    N = w_ext.shape[1]
    ns = S // tm
    return pl.pallas_call(
        _mla_in_kernel,
        grid=(M // tm,),
        in_specs=[pl.BlockSpec((tm, K), lambda i: (i, 0)),
                  pl.BlockSpec((K, N), lambda i: (0, 0)),
                  pl.BlockSpec((1, MLA_Q_LORA), lambda i: (0, 0)),
                  pl.BlockSpec((1, MLA_KV_LORA), lambda i: (0, 0)),
                  pl.BlockSpec((tm, LANES), lambda i: (i % ns, 0)),
                  pl.BlockSpec((tm, LANES), lambda i: (i % ns, 0))],
        out_specs=[pl.BlockSpec((tm, MLA_Q_LORA), lambda i: (i, 0)),
                   pl.BlockSpec((tm, MLA_KV_LORA), lambda i: (i, 0)),
                   pl.BlockSpec((tm, LANES), lambda i: (i, 0))],
        out_shape=[jax.ShapeDtypeStruct((M, MLA_Q_LORA), BF16),
                   jax.ShapeDtypeStruct((M, MLA_KV_LORA), BF16),
                   jax.ShapeDtypeStruct((M, LANES), BF16)],
        compiler_params=_params("parallel"),
        name="mla_in",
    )(h, w_ext, qg.reshape(1, -1), kvg.reshape(1, -1), ct, st)


def _mla_uq_kernel(cq_ref, wm_ref, ws_ref, ct_ref, st_ref, q_ref, *, heads, scale):
    x = cq_ref[...]
    a = jnp.dot(x, wm_ref[...], preferred_element_type=F32)
    a_sw = jnp.dot(x, ws_ref[...], preferred_element_type=F32)
    ct = ct_ref[...]
    st = st_ref[...]
    for hh in range(heads):
        lo = hh * 2 * LANES
        q_ref[:, lo:lo + LANES] = (a[:, lo:lo + LANES] * scale).astype(q_ref.dtype)
        rot = a[:, lo + LANES:lo + 2 * LANES] * ct + a_sw[:, hh * LANES:(hh + 1) * LANES] * st
        q_ref[:, lo + LANES:lo + 2 * LANES] = (rot * scale).astype(q_ref.dtype)


def _mla_uq(cq, w_main, w_sw, ct, st, S, scale, *, tm=512, heads_per_step=4):
    M, K = cq.shape
    hp = heads_per_step
    ns = S // tm
    return pl.pallas_call(
        functools.partial(_mla_uq_kernel, heads=hp, scale=scale),
        grid=(M // tm, N_HEADS // hp),
        in_specs=[pl.BlockSpec((tm, K), lambda i, j: (i, 0)),
                  pl.BlockSpec((K, hp * 2 * LANES), lambda i, j: (0, j)),
                  pl.BlockSpec((K, hp * LANES), lambda i, j: (0, j)),
                  pl.BlockSpec((tm, LANES), lambda i, j: (i % ns, 0)),
                  pl.BlockSpec((tm, LANES), lambda i, j: (i % ns, 0))],
        out_specs=pl.BlockSpec((tm, hp * 2 * LANES), lambda i, j: (i, j)),
        out_shape=jax.ShapeDtypeStruct((M, N_HEADS * 2 * LANES), BF16),
        compiler_params=_params("parallel", "parallel"),
        name="mla_uq",
    )(cq, w_main, w_sw, ct, st)


def _causal_sweep(qi, sa, sb, scores, update, init):
    sa[...] = scores(0)

    def pair(p, state):
        k0 = 2 * p
        sb[...] = scores(k0 + 1)
        state = update(k0, sa[...], state, False)
        sa[...] = scores(k0 + 2)
        return update(k0 + 1, sb[...], state, False)

    state = lax.fori_loop(0, qi // 2, pair, init)

    def odd(state):
        sb[...] = scores(qi)
        state = update(qi - 1, sa[...], state, False)
        return update(qi, sb[...], state, True)

    def even(state):
        return update(qi, sa[...], state, True)

    return lax.cond(qi % 2 == 1, odd, even, state)


def _softmax_update(s, state, v):
    m, l, acc = state
    m_new = jnp.maximum(m, jnp.max(s, axis=-1, keepdims=True))
    alpha = jnp.exp(m - m_new)
    p = jnp.exp(s - m_new)
    l = alpha * l + jnp.sum(p, axis=-1, keepdims=True)
    acc = alpha * acc + jnp.dot(p.astype(BF16), v, preferred_element_type=F32)
    return m_new, l, acc


def _softmax_init(t, dv):
    return (jnp.full((t, 1), MASKED, F32), jnp.zeros((t, 1), F32), jnp.zeros((t, dv), F32))


def _mla_attn_kernel(q_ref, kn_ref, kr_ref, v_ref, o_ref, sa_ref, sb_ref, *, t):
    qi = pl.program_id(2)
    q = q_ref[...]
    row = lax.broadcasted_iota(jnp.int32, (t, t), 0)
    col = lax.broadcasted_iota(jnp.int32, (t, t), 1)

    def scores(ki):
        off = pl.multiple_of(ki * t, t)
        k = jnp.concatenate([kn_ref[pl.ds(off, t), :], kr_ref[pl.ds(off, t), :]], axis=1)
        return lax.dot_general(q, k, _NT, preferred_element_type=F32)

    def update(ki, s, state, diagonal):
        if diagonal:
            s = jnp.where(col <= row, s, -jnp.inf)
        off = pl.multiple_of(ki * t, t)
        return _softmax_update(s, state, v_ref[pl.ds(off, t), :])

    _, l, acc = _causal_sweep(qi, sa_ref, sb_ref, scores, update, _softmax_init(t, MLA_V))
    o_ref[...] = (acc / l).astype(o_ref.dtype)


def _mla_attn(q, kv, kr, B, S, *, t=512):
    nq = S // t
    H = N_HEADS
    return pl.pallas_call(
        functools.partial(_mla_attn_kernel, t=t),
        grid=(B, H, nq),
        in_specs=[pl.BlockSpec((t, 2 * LANES), lambda b, h, i: (b * nq + i, h)),
                  pl.BlockSpec((S, LANES), lambda b, h, i: (b, h)),
                  pl.BlockSpec((S, LANES), lambda b, h, i: (b, 0)),
                  pl.BlockSpec((S, LANES), lambda b, h, i: (b, H + h))],
        out_specs=pl.BlockSpec((t, LANES), lambda b, h, i: (b * nq + i, h)),
        out_shape=jax.ShapeDtypeStruct((B * S, H * MLA_V), BF16),
        scratch_shapes=[pltpu.VMEM((t, t), F32), pltpu.VMEM((t, t), F32)],
        compiler_params=_params("parallel", "parallel", "arbitrary"),
        name="mla_attn",
    )(q, kv, kr, kv)


def _rope_tables(S):
    half = MLA_ROPE // 2
    inv = 1.0 / (ROPE_THETA ** (jnp.arange(0, MLA_ROPE, 2, dtype=F32) / MLA_ROPE))
    ang = jnp.arange(S, dtype=F32)[:, None] * inv[None, :]
    cos, sin = jnp.cos(ang), jnp.sin(ang)
    zeros = jnp.zeros((S, LANES - 2 * half), F32)
    return (jnp.concatenate([cos, cos, zeros], axis=1), jnp.concatenate([-sin, sin, zeros], axis=1))


def _mla_weights(w_in, w_uq, w_ukv):
    D = w_in.shape[0]
    half = MLA_ROPE // 2
    base = MLA_Q_LORA + MLA_KV_LORA
    x1, x2 = w_in[:, base:base + half], w_in[:, base + half:base + 2 * half]
    pad = jnp.zeros((D, LANES - 2 * half), w_in.dtype)
    w_in_ext = jnp.concatenate([w_in[:, :base], x1, x2, pad, x2, x1, pad], axis=1).astype(BF16)

    wq = w_uq.reshape(MLA_Q_LORA, N_HEADS, MLA_NOPE + MLA_ROPE)
    nope, r1, r2 = wq[..., :MLA_NOPE], wq[..., MLA_NOPE:MLA_NOPE + half], wq[..., MLA_NOPE + half:]
    padq = jnp.zeros((MLA_Q_LORA, N_HEADS, LANES - 2 * half), w_uq.dtype)
    w_main = jnp.concatenate([nope, r1, r2, padq], axis=-1).reshape(MLA_Q_LORA, N_HEADS * 2 * LANES).astype(BF16)
    w_sw = jnp.concatenate([r2, r1, padq], axis=-1).reshape(MLA_Q_LORA, N_HEADS * LANES).astype(BF16)

    wkv = w_ukv.reshape(MLA_KV_LORA, N_HEADS, MLA_NOPE + MLA_V)
    w_kv = jnp.concatenate([wkv[..., :MLA_NOPE].reshape(MLA_KV_LORA, -1),
                            wkv[..., MLA_NOPE:].reshape(MLA_KV_LORA, -1)], axis=1).astype(BF16)
    return w_in_ext, w_main, w_sw, w_kv


def _mla_mixer(h, B, S, w_in, q_norm, kv_norm, w_uq, w_ukv):
    w_in_ext, w_main, w_sw, w_kv = _mla_weights(w_in, w_uq, w_ukv)
    ct, st = _rope_tables(S)
    cq, ckv, kr = _mla_in(h, w_in_ext, q_norm, kv_norm, ct, st, S)
    scale = float((MLA_NOPE + MLA_ROPE) ** -0.5)
    q = _mla_uq(cq, w_main, w_sw, ct, st, S, scale)
    kv = _proj(ckv, w_kv, BF16)
    return _mla_attn(q, kv, kr, B, S)


def _hgrn_kernel(q_ref, f_ref, i_ref, g_ref, lbl_ref, on_ref, o_ref, state_ref, b_scr, k_scr,
                 *, layer, tile, chunk):
    C, SB = chunk, HGRN_SUB
    nb = C // SB

    @pl.when(pl.program_id(2) == 0)
    def _():
        state_ref[...] = jnp.zeros_like(state_ref)

    lg = lbl_ref[...]
    e = jnp.exp(lg - jnp.max(lg, axis=0, keepdims=True))
    p = e / jnp.sum(e, axis=0, keepdims=True)
    cs = p[0:1]
    for r in range(1, layer + 1):
        cs = cs + p[r:r + 1]
    lb = cs - p[0:1]
    log_lb = jnp.log(lb)
    log1m_lb = jnp.log1p(-lb)
    one_m_lb = 1.0 - lb
    onorm = on_ref[...]

    row = lax.broadcasted_iota(jnp.int32, (C, 3 * C), 0)
    col = lax.broadcasted_iota(jnp.int32, (C, 3 * C), 1) % C
    blk0 = (row // SB) * SB
    tri = jnp.concatenate([col <= row, col <= blk0, col <= jnp.minimum(blk0 + SB, C - 1)],
                          axis=0).astype(BF16)
    brow = lax.broadcasted_iota(jnp.int32, (C, C), 0) // SB
    bcol = lax.broadcasted_iota(jnp.int32, (C, C), 1) // SB
    bdiff = brow - bcol
    sub = lax.broadcasted_iota(jnp.int32, (SB, LANES), 0)
    causal_cap = [jnp.where(sub >= s, 0.0, -jnp.inf).astype(F32) for s in range(SB)]

    def chunk_step(c, slot):
        off = pl.multiple_of(c * C, C)
        q = q_ref[pl.ds(off, C), :]
        fp = f_ref[pl.ds(off, C), :]
        v = i_ref[pl.ds(off, C), :]
        g = g_ref[pl.ds(off, C), :]

        ls = jnp.minimum(fp, 0.0) - jnp.log(1.0 + jnp.exp(-jnp.abs(fp)))
        cc = log1m_lb + ls
        lf = jnp.maximum(log_lb, cc) + jnp.log(1.0 + jnp.exp(-jnp.abs(log_lb - cc)))
        kk = one_m_lb * jax.nn.sigmoid(-fp)

        p1 = lf.astype(BF16)
        r1 = lf - p1.astype(F32)
        p2 = r1.astype(BF16)
        p3 = (r1 - p2.astype(F32)).astype(BF16)
        cums = jnp.dot(tri, jnp.concatenate([p1, p2, p3], axis=0), preferred_element_type=F32)
        b, r, r_next = cums[:C], cums[C:2 * C], cums[2 * C:]
        bend = b[C - 1:C, :]
        b_scr[slot] = b
        k_scr[slot] = kk

        st_t = state_ref[...]
        qe = q * jnp.exp(b)
        o = lax.dot_general(qe.astype(BF16), st_t.astype(BF16), _NT, preferred_element_type=F32)
        kd = kk * jnp.exp(bend - b)
        state_ref[...] = st_t * jnp.exp(bend) + jnp.dot(v.T.astype(BF16), kd.astype(BF16),
                                                       preferred_element_type=F32)

        kt = kk * jnp.exp(r_next - b)
        gdec = jnp.exp(r_next - r)
        ql = q * jnp.exp(b - r)
        levels = [ql]
        for lvl in range(1, nb - 1):
            fac = jnp.concatenate([jnp.zeros((lvl * SB, LANES), F32), gdec[:C - lvl * SB]], axis=0)
            ql = ql * fac
            levels.append(ql)
        qs = jnp.concatenate(levels, axis=0).astype(BF16)
        rl = lax.dot_general(qs, kt.astype(BF16), _NT, preferred_element_type=F32)
        a = jnp.zeros((C, C), F32)
        for lvl in range(1, nb):
            a = a + jnp.where(bdiff == lvl, rl[(lvl - 1) * C:lvl * C, :], 0.0)
        o = o + jnp.dot(a.astype(BF16), v.astype(BF16), preferred_element_type=F32)

        diag = []
        for blk in range(nb):
            bq = b[blk * SB:(blk + 1) * SB]
            qq = q[blk * SB:(blk + 1) * SB]
            acc = jnp.zeros((SB, LANES), F32)
            for s in range(SB):
                rr = blk * SB + s
                dec = jnp.exp(jnp.minimum(bq - b_scr[slot, pl.ds(rr, 1), :], causal_cap[s]))
                a_ts = jnp.sum(qq * k_scr[slot, pl.ds(rr, 1), :] * dec, axis=-1, keepdims=True)
                acc = acc + a_ts * i_ref[pl.ds(off + rr, 1), :]
            diag.append(acc)
        o = o + jnp.concatenate(diag, axis=0)

        y = _rms_rows(o, onorm)
        o_ref[pl.ds(off, C), :] = (y * (g * jax.nn.sigmoid(g))).astype(o_ref.dtype)

    def chunk_group(grp, carry):
        for slot in range(HGRN_GROUP):
            chunk_step(HGRN_GROUP * grp + slot, slot)
        return carry

    lax.fori_loop(0, tile // (HGRN_GROUP * C), chunk_group, 0)


def _hgrn_mixer_core(proj, lb_logits, o_norm, B, S, layer, *, tile=512):
    nt = S // tile
    H = N_HEADS
    blk = lambda sec: pl.BlockSpec((tile, LANES), lambda b, h, t, sec=sec: (b * nt + t, sec * H + h))
    return pl.pallas_call(
        functools.partial(_hgrn_kernel, layer=layer, tile=tile, chunk=HGRN_CHUNK),
        grid=(B, H, nt),
        in_specs=[blk(0), blk(1), blk(2), blk(3),
                  pl.BlockSpec((DEPTH, LANES), lambda b, h, t: (0, h)),
                  pl.BlockSpec((1, LANES), lambda b, h, t: (0, h))],
        out_specs=pl.BlockSpec((tile, LANES), lambda b, h, t: (b * nt + t, h)),
        out_shape=jax.ShapeDtypeStruct((B * S, D_MODEL), BF16),
        scratch_shapes=[pltpu.VMEM((LANES, LANES), F32),
                        pltpu.VMEM((HGRN_GROUP, HGRN_CHUNK, LANES), F32),
                        pltpu.VMEM((HGRN_GROUP, HGRN_CHUNK, LANES), F32)],
        compiler_params=_params("parallel", "parallel", "arbitrary"),
        name="hgrn",
    )(proj, proj, proj, proj, lb_logits, o_norm.reshape(1, -1))


SB_CUMSUM_BLOCK = 256


def _sb_kernel(q_ref, k_ref, v_ref, o_ref, sa_ref, sb_ref, *, t):
    cb = SB_CUMSUM_BLOCK
    qi = pl.program_id(2)
    q = q_ref[...]
    row = lax.broadcasted_iota(jnp.int32, (t, t), 0)
    col = lax.broadcasted_iota(jnp.int32, (t, t), 1)
    strict = col < row
    jj = lax.broadcasted_iota(jnp.int32, (2 * cb, cb), 0) % cb
    ss = lax.broadcasted_iota(jnp.int32, (2 * cb, cb), 1)
    later2 = (jj > ss).astype(BF16)

    def scores(ki):
        off = pl.multiple_of(ki * t, t)
        return lax.dot_general(q, k_ref[pl.ds(off, t), :], _NT, preferred_element_type=F32)

    def update(ki, z, acc, diagonal):
        l1m = -(jnp.maximum(z, 0.0) + jnp.log(1.0 + jnp.exp(-jnp.abs(z))))
        if diagonal:
            l1m = jnp.where(strict, l1m, 0.0)
        parts = []
        tail = jnp.zeros((t, 1), F32)
        for j in reversed(range(t // cb)):
            lj = l1m[:, j * cb:(j + 1) * cb]
            hi = lj.astype(BF16)
            lo = (lj - hi.astype(F32)).astype(BF16)
            parts.insert(0, jnp.dot(jnp.concatenate([hi, lo], axis=1), later2,
                                    preferred_element_type=F32) + tail)
            tail = tail + jnp.sum(lj, axis=-1, keepdims=True)
        after = jnp.concatenate(parts, axis=1)
        w = jnp.exp(z + l1m + after)
        if diagonal:
            w = jnp.where(strict, w, 0.0)
        off = pl.multiple_of(ki * t, t)
        return acc * jnp.exp(tail) + jnp.dot(w.astype(BF16), v_ref[pl.ds(off, t), :],
                                             preferred_element_type=F32)

    acc = _causal_sweep(qi, sa_ref, sb_ref, scores, update, jnp.zeros((t, HEAD_DIM), F32))
    o_ref[...] = acc.astype(o_ref.dtype)


def _sb_attn(qkv, B, S, *, t=512):
    nq = S // t
    H = N_HEADS
    return pl.pallas_call(
        functools.partial(_sb_kernel, t=t),
        grid=(B, H, nq),
        in_specs=[pl.BlockSpec((t, LANES), lambda b, h, i: (b * nq + i, h)),
                  pl.BlockSpec((S, LANES), lambda b, h, i: (b, H + h)),
                  pl.BlockSpec((S, LANES), lambda b, h, i: (b, 2 * H + h))],
        out_specs=pl.BlockSpec((t, LANES), lambda b, h, i: (b * nq + i, h)),
        out_shape=jax.ShapeDtypeStruct((B * S, D_MODEL), BF16),
        scratch_shapes=[pltpu.VMEM((t, t), F32), pltpu.VMEM((t, t), F32)],
        compiler_params=_params("parallel", "parallel", "arbitrary"),
        name="sb_attn",
    )(qkv, qkv, qkv)


def _moba_kernel(q_ref, k_ref, v_ref, o_ref, kmean_ref, sa_ref, sb_ref, *, nblk, t):
    bpt = t // MOBA_BLOCK
    qi = pl.program_id(2)

    @pl.when(qi == 0)
    def _():
        kmean_ref[...] = jnp.zeros_like(kmean_ref)
        kmean_ref[0:nblk, :] = jnp.mean(k_ref[...].astype(F32).reshape(nblk, MOBA_BLOCK, LANES), axis=1)

    q = q_ref[...]

    gate = lax.dot_general(kmean_ref[...], q.astype(F32), _NT, precision=lax.Precision.HIGHEST,
                           preferred_element_type=F32)
    blk = lax.broadcasted_iota(jnp.int32, (LANES, t), 0)
    tile_blk = lax.broadcasted_iota(jnp.int32, (LANES, t), 1) // MOBA_BLOCK
    neg_inf = jnp.float32(-jnp.inf)
    gate = jnp.where(blk < qi * bpt + tile_blk, gate, neg_inf)
    sel = jnp.zeros((LANES, t), F32)
    for _ in range(MOBA_TOPK):
        mx = jnp.max(gate, axis=0, keepdims=True)
        first = jnp.min(jnp.where(gate == mx, blk, LANES), axis=0, keepdims=True)
        pick = jnp.logical_and(blk == first, mx > neg_inf)
        sel = jnp.where(pick, 1.0, sel)
        gate = jnp.where(pick, neg_inf, gate)
    sel_q = sel.T.astype(BF16)

    row = lax.broadcasted_iota(jnp.int32, (t, t), 0)
    col = lax.broadcasted_iota(jnp.int32, (t, t), 1)
    own_causal = jnp.logical_and(row // MOBA_BLOCK == col // MOBA_BLOCK, col <= row)

    def scores(kc):
        off = pl.multiple_of(kc * t, t)
        return lax.dot_general(q, k_ref[pl.ds(off, t), :], _NT, preferred_element_type=F32)

    def update(kc, s, state, diagonal):
        spread = (blk == kc * bpt + tile_blk).astype(BF16)
        chosen = jnp.dot(sel_q, spread, preferred_element_type=F32) > 0.5
        if diagonal:
            chosen = jnp.logical_or(chosen, own_causal)
        s = jnp.where(chosen, s, -jnp.inf)
        off = pl.multiple_of(kc * t, t)
        return _softmax_update(s, state, v_ref[pl.ds(off, t), :])

    _, l, acc = _causal_sweep(qi, sa_ref, sb_ref, scores, update, _softmax_init(t, HEAD_DIM))
    o_ref[...] = (acc / l).astype(o_ref.dtype)


def _moba_attn(qkv, B, S, *, t=512):
    nq = S // t
    H = N_HEADS
    return pl.pallas_call(
        functools.partial(_moba_kernel, nblk=S // MOBA_BLOCK, t=t),
        grid=(B, H, nq),
        in_specs=[pl.BlockSpec((t, LANES), lambda b, h, i: (b * nq + i, h)),
                  pl.BlockSpec((S, LANES), lambda b, h, i: (b, H + h)),
                  pl.BlockSpec((S, LANES), lambda b, h, i: (b, 2 * H + h))],
        out_specs=pl.BlockSpec((t, LANES), lambda b, h, i: (b * nq + i, h)),
        out_shape=jax.ShapeDtypeStruct((B * S, D_MODEL), BF16),
        scratch_shapes=[pltpu.VMEM((LANES, LANES), F32), pltpu.VMEM((t, t), F32), pltpu.VMEM((t, t), F32)],
        compiler_params=_params("parallel", "parallel", "arbitrary"),
        name="moba_attn",
    )(qkv, qkv, qkv)


def _qkv_proj(h, w_in):
    return _proj(h, w_in, BF16, scaled_cols=D_MODEL, scale=float(HEAD_DIM ** -0.5))


def _sb_mixer(h, B, S, w_in):
    return _sb_attn(_qkv_proj(h, w_in), B, S)


def _moba_mixer(h, B, S, w_in):
    return _moba_attn(_qkv_proj(h, w_in), B, S)


def kernel(x, mla_w_in, mla_q_norm, mla_kv_norm, mla_w_uq, mla_w_ukv, mla_w_o, hgrn_w_in, hgrn_lb_logits, hgrn_o_norm, hgrn_w_o, sb_w_in, sb_w_o, moba_w_in, moba_w_o, ln_g, ln_b, mlp_w1, mlp_w2):
    B, S, D = x.shape
    assert D == D_MODEL and S % MOBA_BLOCK == 0 and S % 512 == 0
    h = x.reshape(B * S, D)
    hb = h
    n_mixers = 4
    for i in range(DEPTH):
        kind, slot = i % n_mixers, i // n_mixers
        if kind == 0:
            o = _mla_mixer(hb, B, S, mla_w_in[slot], mla_q_norm[slot], mla_kv_norm[slot],
                           mla_w_uq[slot], mla_w_ukv[slot])
            w_o = mla_w_o[slot]
        elif kind == 1:
            proj = _proj(hb, hgrn_w_in[slot], F32)
            o = _hgrn_mixer_core(proj, hgrn_lb_logits, hgrn_o_norm[slot], B, S, i)
            w_o = hgrn_w_o[slot]
        elif kind == 2:
            o = _sb_mixer(hb, B, S, sb_w_in[slot])
            w_o = sb_w_o[slot]
        else:
            o = _moba_mixer(hb, B, S, moba_w_in[slot])
            w_o = moba_w_o[slot]
        h, hb = _proj_res_ln(o, w_o.astype(BF16), h, ln_g[i, 0], ln_b[i, 0])
        a = _proj(hb, mlp_w1[i], BF16, act="relu2")
        h, hb = _proj_res_ln(a, mlp_w2[i].astype(BF16), h, ln_g[i, 1], ln_b[i, 1])
    return h.reshape(B, S, D)
```

```python
import functools

import jax
import jax.numpy as jnp
from jax import lax
from jax.experimental import pallas as pl
from jax.experimental.pallas import tpu as pltpu

F32 = jnp.float32
BF16 = jnp.bfloat16

D_MODEL = 2048
DEPTH = 4
N_HEADS = 16
HEAD_DIM = 128
MLA_Q_LORA = 512
MLA_KV_LORA = 512
MLA_NOPE = 128
MLA_ROPE = 64
MLA_V = 128
ROPE_THETA = 10000.0
HGRN_CHUNK = 64
HGRN_SUB = 8
HGRN_GROUP = 8
MOBA_BLOCK = 256
MOBA_TOPK = 3
ALPHA = float((2 * DEPTH) ** 0.25)
LN_EPS = 1e-5
RMS_EPS = 1e-6

V7X_VMEM_BYTES = 64 * 1024 * 1024
VMEM_LIMIT = V7X_VMEM_BYTES - 8 * 1024 * 1024
LANES = 128
MASKED = -1e30
LOG2E = 1.4426950408889634

_NT = (((1,), (1,)), ((), ()))


def _params(*sem):
    return pltpu.CompilerParams(dimension_semantics=sem, vmem_limit_bytes=VMEM_LIMIT)


def _layer_norm_rows(y, g, b):
    mu = jnp.mean(y, axis=-1, keepdims=True)
    d = y - mu
    var = jnp.mean(d * d, axis=-1, keepdims=True)
    return d * lax.rsqrt(var + LN_EPS) * g + b


def _rms_rows(x, g):
    return x * lax.rsqrt(jnp.mean(x * x, axis=-1, keepdims=True) + RMS_EPS) * g


def _proj_kernel(x_ref, w_ref, o_ref, wbf_ref, *, act, scaled_tiles, scale):
    @pl.when(pl.program_id(1) == 0)
    def _():
        wbf_ref[...] = w_ref[...].astype(BF16)

    acc = jnp.dot(x_ref[...].astype(BF16), wbf_ref[...], preferred_element_type=F32)
    if act == "relu2":
        r = jnp.maximum(acc, 0.0)
        acc = r * r
    if scaled_tiles:
        acc = acc * jnp.where(pl.program_id(0) < scaled_tiles, scale, 1.0)
    o_ref[...] = acc.astype(o_ref.dtype)


def _proj(x, w, out_dtype, *, layer=0, tm=1024, tn=1024, act=None, scaled_cols=0, scale=1.0):
    M, K = x.shape
    N = w.shape[-1]
    tm, tn = min(tm, M), min(tn, N)
    assert scaled_cols % tn == 0 and M % tm == 0 and N % tn == 0
    if w.ndim == 3:
        w_spec = pl.BlockSpec((None, K, tn), lambda j, i: (layer, 0, j))
    else:
        w_spec = pl.BlockSpec((K, tn), lambda j, i: (0, j))
    return pl.pallas_call(
        functools.partial(_proj_kernel, act=act, scaled_tiles=scaled_cols // tn, scale=scale),
        grid=(N // tn, M // tm),
        in_specs=[pl.BlockSpec((tm, K), lambda j, i: (i, 0)), w_spec],
        out_specs=pl.BlockSpec((tm, tn), lambda j, i: (i, j)),
        out_shape=jax.ShapeDtypeStruct((M, N), out_dtype),
        scratch_shapes=[pltpu.VMEM((K, tn), BF16)],
        compiler_params=_params("parallel", "arbitrary"),
        name="proj",
    )(x, w)


def _proj_res_ln_kernel(x_ref, w_ref, h_ref, g_ref, b_ref, o_ref, ob_ref, acc_ref, *, nk):
    k = pl.program_id(1)
    def part():
        return jnp.dot(x_ref[...], w_ref[...], preferred_element_type=F32)

    def finish(y):
        out = _layer_norm_rows(ALPHA * h_ref[...] + y, g_ref[...], b_ref[...])
        o_ref[...] = out
        ob_ref[...] = out.astype(BF16)

    if nk == 1:
        finish(part())
    else:
        @pl.when(k == 0)
        def _():
            acc_ref[...] = part()

        @pl.when(jnp.logical_and(k > 0, k < nk - 1))
        def _():
            acc_ref[...] += part()

        @pl.when(k == nk - 1)
        def _():
            finish(acc_ref[...] + part())


def _proj_res_ln(x, w, h, g, b, *, tm=512, tk=2048):
    M, K = x.shape
    N = w.shape[1]
    nk = K // tk
    return pl.pallas_call(
        functools.partial(_proj_res_ln_kernel, nk=nk),
        grid=(M // tm, nk),
        in_specs=[pl.BlockSpec((tm, tk), lambda i, k: (i, k)),
                  pl.BlockSpec((tk, N), lambda i, k: (k, 0)),
                  pl.BlockSpec((tm, N), lambda i, k: (i, 0)),
                  pl.BlockSpec((1, N), lambda i, k: (0, 0)),
                  pl.BlockSpec((1, N), lambda i, k: (0, 0))],
        out_specs=[pl.BlockSpec((tm, N), lambda i, k: (i, 0)),
                   pl.BlockSpec((tm, N), lambda i, k: (i, 0))],
        out_shape=[jax.ShapeDtypeStruct((M, N), F32), jax.ShapeDtypeStruct((M, N), BF16)],
        scratch_shapes=[pltpu.VMEM((tm, N), F32)],
        compiler_params=_params("parallel", "arbitrary"),
        name="proj_res_ln",
    )(x, w, h, g.reshape(1, N), b.reshape(1, N))


def _mla_in_kernel(h_ref, w_ref, qg_ref, kvg_ref, ct_ref, st_ref, cq_ref, ckv_ref, kr_ref):
    acc = jnp.dot(h_ref[...].astype(BF16), w_ref[...], preferred_element_type=F32)
    ql, kvl = MLA_Q_LORA, MLA_KV_LORA
    cq_ref[...] = _rms_rows(acc[:, :ql], qg_ref[...]).astype(cq_ref.dtype)
    ckv_ref[...] = _rms_rows(acc[:, ql:ql + kvl], kvg_ref[...]).astype(ckv_ref.dtype)
    a = acc[:, ql + kvl:ql + kvl + LANES]
    a_sw = acc[:, ql + kvl + LANES:]
    kr_ref[...] = (a * ct_ref[...] + a_sw * st_ref[...]).astype(kr_ref.dtype)


def _mla_in(h, w_ext, qg, kvg, ct, st, S, *, tm=512):
    M, K = h.shape
    N = w_ext.shape[1]
    ns = S // tm
    return pl.pallas_call(
        _mla_in_kernel,
        grid=(M // tm,),
        in_specs=[pl.BlockSpec((tm, K), lambda i: (i, 0)),
                  pl.BlockSpec((K, N), lambda i: (0, 0)),
                  pl.BlockSpec((1, MLA_Q_LORA), lambda i: (0, 0)),
                  pl.BlockSpec((1, MLA_KV_LORA), lambda i: (0, 0)),
                  pl.BlockSpec((tm, LANES), lambda i: (i % ns, 0)),
                  pl.BlockSpec((tm, LANES), lambda i: (i % ns, 0))],
        out_specs=[pl.BlockSpec((tm, MLA_Q_LORA), lambda i: (i, 0)),
                   pl.BlockSpec((tm, MLA_KV_LORA), lambda i: (i, 0)),
                   pl.BlockSpec((tm, LANES), lambda i: (i, 0))],
        out_shape=[jax.ShapeDtypeStruct((M, MLA_Q_LORA), BF16),
                   jax.ShapeDtypeStruct((M, MLA_KV_LORA), BF16),
                   jax.ShapeDtypeStruct((M, LANES), BF16)],
        compiler_params=_params("parallel"),
        name="mla_in",
    )(h, w_ext, qg.reshape(1, -1), kvg.reshape(1, -1), ct, st)


def _mla_uq_kernel(cq_ref, wm_ref, ws_ref, ct_ref, st_ref, q_ref, *, heads, scale):
    x = cq_ref[...]
    a = jnp.dot(x, wm_ref[...], preferred_element_type=F32)
    a_sw = jnp.dot(x, ws_ref[...], preferred_element_type=F32)
    ct = ct_ref[...]
    st = st_ref[...]
    for hh in range(heads):
        lo = hh * 2 * LANES
        q_ref[:, lo:lo + LANES] = (a[:, lo:lo + LANES] * scale).astype(q_ref.dtype)
        rot = a[:, lo + LANES:lo + 2 * LANES] * ct + a_sw[:, hh * LANES:(hh + 1) * LANES] * st
        q_ref[:, lo + LANES:lo + 2 * LANES] = (rot * scale).astype(q_ref.dtype)


def _mla_uq(cq, w_main, w_sw, ct, st, S, scale, *, tm=512, heads_per_step=4):
    M, K = cq.shape
    hp = heads_per_step
    ns = S // tm
    return pl.pallas_call(
        functools.partial(_mla_uq_kernel, heads=hp, scale=scale),
        grid=(M // tm, N_HEADS // hp),
        in_specs=[pl.BlockSpec((tm, K), lambda i, j: (i, 0)),
                  pl.BlockSpec((K, hp * 2 * LANES), lambda i, j: (0, j)),
                  pl.BlockSpec((K, hp * LANES), lambda i, j: (0, j)),
                  pl.BlockSpec((tm, LANES), lambda i, j: (i % ns, 0)),
                  pl.BlockSpec((tm, LANES), lambda i, j: (i % ns, 0))],
        out_specs=pl.BlockSpec((tm, hp * 2 * LANES), lambda i, j: (i, j)),
        out_shape=jax.ShapeDtypeStruct((M, N_HEADS * 2 * LANES), BF16),
        compiler_params=_params("parallel", "parallel"),
        name="mla_uq",
    )(cq, w_main, w_sw, ct, st)


def _causal_sweep(qi, sa, sb, scores, update, init):
    sa[...] = scores(0)

    def pair(p, state):
        k0 = 2 * p
        sb[...] = scores(k0 + 1)
        state = update(k0, sa[...], state, False)
        sa[...] = scores(k0 + 2)
        return update(k0 + 1, sb[...], state, False)

    state = lax.fori_loop(0, qi // 2, pair, init)

    def odd(state):
        sb[...] = scores(qi)
        state = update(qi - 1, sa[...], state, False)
        return update(qi, sb[...], state, True)

    def even(state):
        return update(qi, sa[...], state, True)

    return lax.cond(qi % 2 == 1, odd, even, state)


def _softmax_update(s, state, v):
    m, l, acc = state
    m_new = jnp.maximum(m, jnp.max(s, axis=-1, keepdims=True))
    alpha = jnp.exp(m - m_new)
    p = jnp.exp(s - m_new)
    l = alpha * l + jnp.sum(p, axis=-1, keepdims=True)
    acc = alpha * acc + jnp.dot(p.astype(BF16), v, preferred_element_type=F32)
    return m_new, l, acc


def _softmax_init(t, dv):
    return (jnp.full((t, 1), MASKED, F32), jnp.zeros((t, 1), F32), jnp.zeros((t, dv), F32))


def _mla_attn_kernel(q_ref, kn_ref, kr_ref, v_ref, o_ref, sa_ref, sb_ref, *, t):
    qi = pl.program_id(2)
    q = q_ref[...]
    row = lax.broadcasted_iota(jnp.int32, (t, t), 0)
    col = lax.broadcasted_iota(jnp.int32, (t, t), 1)

    def scores(ki):
        off = pl.multiple_of(ki * t, t)
        k = jnp.concatenate([kn_ref[pl.ds(off, t), :], kr_ref[pl.ds(off, t), :]], axis=1)
        return lax.dot_general(q, k, _NT, preferred_element_type=F32)

    def update(ki, s, state, diagonal):
        if diagonal:
            s = jnp.where(col <= row, s, -jnp.inf)
        off = pl.multiple_of(ki * t, t)
        return _softmax_update(s, state, v_ref[pl.ds(off, t), :])

    _, l, acc = _causal_sweep(qi, sa_ref, sb_ref, scores, update, _softmax_init(t, MLA_V))
    o_ref[...] = (acc / l).astype(o_ref.dtype)


def _mla_attn(q, kv, kr, B, S, *, t=512):
    nq = S // t
    H = N_HEADS
    return pl.pallas_call(
        functools.partial(_mla_attn_kernel, t=t),
        grid=(B, H, nq),
        in_specs=[pl.BlockSpec((t, 2 * LANES), lambda b, h, i: (b * nq + i, h)),
                  pl.BlockSpec((S, LANES), lambda b, h, i: (b, h)),
                  pl.BlockSpec((S, LANES), lambda b, h, i: (b, 0)),
                  pl.BlockSpec((S, LANES), lambda b, h, i: (b, H + h))],
        out_specs=pl.BlockSpec((t, LANES), lambda b, h, i: (b * nq + i, h)),
        out_shape=jax.ShapeDtypeStruct((B * S, H * MLA_V), BF16),
        scratch_shapes=[pltpu.VMEM((t, t), F32), pltpu.VMEM((t, t), F32)],
        compiler_params=_params("parallel", "parallel", "arbitrary"),
        name="mla_attn",
    )(q, kv, kr, kv)


def _rope_tables(S):
    half = MLA_ROPE // 2
    inv = 1.0 / (ROPE_THETA ** (jnp.arange(0, MLA_ROPE, 2, dtype=F32) / MLA_ROPE))
    ang = jnp.arange(S, dtype=F32)[:, None] * inv[None, :]
    cos, sin = jnp.cos(ang), jnp.sin(ang)
    zeros = jnp.zeros((S, LANES - 2 * half), F32)
    return (jnp.concatenate([cos, cos, zeros], axis=1), jnp.concatenate([-sin, sin, zeros], axis=1))


def _mla_weights(w_in, w_uq, w_ukv):
    D = w_in.shape[0]
    half = MLA_ROPE // 2
    base = MLA_Q_LORA + MLA_KV_LORA
    x1, x2 = w_in[:, base:base + half], w_in[:, base + half:base + 2 * half]
    pad = jnp.zeros((D, LANES - 2 * half), w_in.dtype)
    w_in_ext = jnp.concatenate([w_in[:, :base], x1, x2, pad, x2, x1, pad], axis=1).astype(BF16)

    wq = w_uq.reshape(MLA_Q_LORA, N_HEADS, MLA_NOPE + MLA_ROPE)
    nope, r1, r2 = wq[..., :MLA_NOPE], wq[..., MLA_NOPE:MLA_NOPE + half], wq[..., MLA_NOPE + half:]
    padq = jnp.zeros((MLA_Q_LORA, N_HEADS, LANES - 2 * half), w_uq.dtype)
    w_main = jnp.concatenate([nope, r1, r2, padq], axis=-1).reshape(MLA_Q_LORA, N_HEADS * 2 * LANES).astype(BF16)
    w_sw = jnp.concatenate([r2, r1, padq], axis=-1).reshape(MLA_Q_LORA, N_HEADS * LANES).astype(BF16)

    wkv = w_ukv.reshape(MLA_KV_LORA, N_HEADS, MLA_NOPE + MLA_V)
    w_kv = jnp.concatenate([wkv[..., :MLA_NOPE].reshape(MLA_KV_LORA, -1),
                            wkv[..., MLA_NOPE:].reshape(MLA_KV_LORA, -1)], axis=1).astype(BF16)
    return w_in_ext, w_main, w_sw, w_kv


def _mla_mixer(h, B, S, w_in, q_norm, kv_norm, w_uq, w_ukv):
    w_in_ext, w_main, w_sw, w_kv = _mla_weights(w_in, w_uq, w_ukv)
    ct, st = _rope_tables(S)
    cq, ckv, kr = _mla_in(h, w_in_ext, q_norm, kv_norm, ct, st, S)
    scale = float((MLA_NOPE + MLA_ROPE) ** -0.5)
    q = _mla_uq(cq, w_main, w_sw, ct, st, S, scale)
    kv = _proj(ckv, w_kv, BF16)
    return _mla_attn(q, kv, kr, B, S)


def _hgrn_kernel(q_ref, f_ref, i_ref, g_ref, lbl_ref, on_ref, o_ref, state_ref, b_scr, k_scr,
                 *, layer, tile, chunk):
    C, SB = chunk, HGRN_SUB
    nb = C // SB

    @pl.when(pl.program_id(2) == 0)
    def _():
        state_ref[...] = jnp.zeros_like(state_ref)

    lg = lbl_ref[...]
    e = jnp.exp(lg - jnp.max(lg, axis=0, keepdims=True))
    p = e / jnp.sum(e, axis=0, keepdims=True)
    cs = p[0:1]
    for r in range(1, layer + 1):
        cs = cs + p[r:r + 1]
    lb = cs - p[0:1]
    log_lb = jnp.log(lb)
    log1m_lb = jnp.log1p(-lb)
    one_m_lb = 1.0 - lb
    onorm = on_ref[...]

    row = lax.broadcasted_iota(jnp.int32, (C, 3 * C), 0)
    col = lax.broadcasted_iota(jnp.int32, (C, 3 * C), 1) % C
    blk0 = (row // SB) * SB
    tri = jnp.concatenate([col <= row, col <= blk0, col <= jnp.minimum(blk0 + SB, C - 1)],
                          axis=0).astype(BF16)
    brow = lax.broadcasted_iota(jnp.int32, (C, C), 0) // SB
    bcol = lax.broadcasted_iota(jnp.int32, (C, C), 1) // SB
    bdiff = brow - bcol
    sub = lax.broadcasted_iota(jnp.int32, (SB, LANES), 0)
    causal_cap = [jnp.where(sub >= s, 0.0, -jnp.inf).astype(F32) for s in range(SB)]

    def chunk_step(c, slot):
        off = pl.multiple_of(c * C, C)
        q = q_ref[pl.ds(off, C), :]
        fp = f_ref[pl.ds(off, C), :]
        v = i_ref[pl.ds(off, C), :]
        g = g_ref[pl.ds(off, C), :]

        ls = jnp.minimum(fp, 0.0) - jnp.log(1.0 + jnp.exp(-jnp.abs(fp)))
        cc = log1m_lb + ls
        lf = jnp.maximum(log_lb, cc) + jnp.log(1.0 + jnp.exp(-jnp.abs(log_lb - cc)))
        kk = one_m_lb * jax.nn.sigmoid(-fp)

        p1 = lf.astype(BF16)
        r1 = lf - p1.astype(F32)
        p2 = r1.astype(BF16)
        p3 = (r1 - p2.astype(F32)).astype(BF16)
        cums = jnp.dot(tri, jnp.concatenate([p1, p2, p3], axis=0), preferred_element_type=F32)
        b, r, r_next = cums[:C], cums[C:2 * C], cums[2 * C:]
        bend = b[C - 1:C, :]
        b_scr[slot] = b
        k_scr[slot] = kk

        st_t = state_ref[...]
        qe = q * jnp.exp(b)
        o = lax.dot_general(qe.astype(BF16), st_t.astype(BF16), _NT, preferred_element_type=F32)
        kd = kk * jnp.exp(bend - b)
        state_ref[...] = st_t * jnp.exp(bend) + jnp.dot(v.T.astype(BF16), kd.astype(BF16),
                                                       preferred_element_type=F32)

        kt = kk * jnp.exp(r_next - b)
        gdec = jnp.exp(r_next - r)
        ql = q * jnp.exp(b - r)
        levels = [ql]
        for lvl in range(1, nb - 1):
            fac = jnp.concatenate([jnp.zeros((lvl * SB, LANES), F32), gdec[:C - lvl * SB]], axis=0)
            ql = ql * fac
            levels.append(ql)
        qs = jnp.concatenate(levels, axis=0).astype(BF16)
        rl = lax.dot_general(qs, kt.astype(BF16), _NT, preferred_element_type=F32)
        a = jnp.zeros((C, C), F32)
        for lvl in range(1, nb):
            a = a + jnp.where(bdiff == lvl, rl[(lvl - 1) * C:lvl * C, :], 0.0)
        o = o + jnp.dot(a.astype(BF16), v.astype(BF16), preferred_element_type=F32)

        diag = []
        for blk in range(nb):
            bq = b[blk * SB:(blk + 1) * SB]
            qq = q[blk * SB:(blk + 1) * SB]
            acc = jnp.zeros((SB, LANES), F32)
            for s in range(SB):
                rr = blk * SB + s
                dec = jnp.exp(jnp.minimum(bq - b_scr[slot, pl.ds(rr, 1), :], causal_cap[s]))
                a_ts = jnp.sum(qq * k_scr[slot, pl.ds(rr, 1), :] * dec, axis=-1, keepdims=True)
                acc = acc + a_ts * i_ref[pl.ds(off + rr, 1), :]
            diag.append(acc)
        o = o + jnp.concatenate(diag, axis=0)

        y = _rms_rows(o, onorm)
        o_ref[pl.ds(off, C), :] = (y * (g * jax.nn.sigmoid(g))).astype(o_ref.dtype)

    def chunk_group(grp, carry):
        for slot in range(HGRN_GROUP):
            chunk_step(HGRN_GROUP * grp + slot, slot)
        return carry

    lax.fori_loop(0, tile // (HGRN_GROUP * C), chunk_group, 0)


def _hgrn_mixer_core(proj, lb_logits, o_norm, B, S, layer, *, tile=512):
    nt = S // tile
    H = N_HEADS
    blk = lambda sec: pl.BlockSpec((tile, LANES), lambda b, h, t, sec=sec: (b * nt + t, sec * H + h))
    return pl.pallas_call(
        functools.partial(_hgrn_kernel, layer=layer, tile=tile, chunk=HGRN_CHUNK),
        grid=(B, H, nt),
        in_specs=[blk(0), blk(1), blk(2), blk(3),
                  pl.BlockSpec((DEPTH, LANES), lambda b, h, t: (0, h)),
                  pl.BlockSpec((1, LANES), lambda b, h, t: (0, h))],
        out_specs=pl.BlockSpec((tile, LANES), lambda b, h, t: (b * nt + t, h)),
        out_shape=jax.ShapeDtypeStruct((B * S, D_MODEL), BF16),
        scratch_shapes=[pltpu.VMEM((LANES, LANES), F32),
                        pltpu.VMEM((HGRN_GROUP, HGRN_CHUNK, LANES), F32),
                        pltpu.VMEM((HGRN_GROUP, HGRN_CHUNK, LANES), F32)],
        compiler_params=_params("parallel", "parallel", "arbitrary"),
        name="hgrn",
    )(proj, proj, proj, proj, lb_logits, o_norm.reshape(1, -1))


SB_CUMSUM_BLOCK = 256


def _sb_kernel(q_ref, k_ref, v_ref, o_ref, sa_ref, sb_ref, *, t):
    cb = SB_CUMSUM_BLOCK
    qi = pl.program_id(2)
    q = q_ref[...]
    row = lax.broadcasted_iota(jnp.int32, (t, t), 0)
    col = lax.broadcasted_iota(jnp.int32, (t, t), 1)
    strict = col < row
    jj = lax.broadcasted_iota(jnp.int32, (2 * cb, cb), 0) % cb
    ss = lax.broadcasted_iota(jnp.int32, (2 * cb, cb), 1)
    later2 = (jj > ss).astype(BF16)

    def scores(ki):
        off = pl.multiple_of(ki * t, t)
        return lax.dot_general(q, k_ref[pl.ds(off, t), :], _NT, preferred_element_type=F32)

    def update(ki, zn, acc, diagonal):
        l1m = jnp.minimum(zn, 0.0) - jnp.log(1.0 + jnp.exp2(jnp.abs(zn) * (-LOG2E)))
        if diagonal:
            l1m = jnp.where(strict, l1m, 0.0)
        parts = []
        tail = jnp.zeros((t, 1), F32)
        for j in reversed(range(t // cb)):
            lj = l1m[:, j * cb:(j + 1) * cb]
            hi = lj.astype(BF16)
            lo = (lj - hi.astype(F32)).astype(BF16)
            parts.insert(0, jnp.dot(jnp.concatenate([hi, lo], axis=1), later2,
                                    preferred_element_type=F32) + tail)
            tail = tail + jnp.sum(lj, axis=-1, keepdims=True)
        after = jnp.concatenate(parts, axis=1)
        w = jnp.exp((l1m - zn) + after)
        if diagonal:
            w = jnp.where(strict, w, 0.0)
        off = pl.multiple_of(ki * t, t)
        return acc * jnp.exp(tail) + jnp.dot(w.astype(BF16), v_ref[pl.ds(off, t), :],
                                             preferred_element_type=F32)

    acc = _causal_sweep(qi, sa_ref, sb_ref, scores, update, jnp.zeros((t, HEAD_DIM), F32))
    o_ref[...] = acc.astype(o_ref.dtype)


def _sb_attn(qkv, B, S, *, t=512):
    nq = S // t
    H = N_HEADS
    return pl.pallas_call(
        functools.partial(_sb_kernel, t=t),
        grid=(B, H, nq),
        in_specs=[pl.BlockSpec((t, LANES), lambda b, h, i: (b * nq + i, h)),
                  pl.BlockSpec((S, LANES), lambda b, h, i: (b, H + h)),
                  pl.BlockSpec((S, LANES), lambda b, h, i: (b, 2 * H + h))],
        out_specs=pl.BlockSpec((t, LANES), lambda b, h, i: (b * nq + i, h)),
        out_shape=jax.ShapeDtypeStruct((B * S, D_MODEL), BF16),
        scratch_shapes=[pltpu.VMEM((t, t), F32), pltpu.VMEM((t, t), F32)],
        compiler_params=_params("parallel", "parallel", "arbitrary"),
        name="sb_attn",
    )(qkv, qkv, qkv)


MOBA_VETO = 2.0 ** 100


def _moba_kernel(q_ref, k_ref, v_ref, o_ref, kmean_ref, sa_ref, sb_ref, *, nblk, t):
    bpt = t // MOBA_BLOCK
    qi = pl.program_id(2)

    @pl.when(qi == 0)
    def _():
        kmean_ref[...] = jnp.mean(k_ref[...].astype(F32).reshape(nblk, MOBA_BLOCK, LANES), axis=1)

    q = q_ref[...]

    gate = lax.dot_general(kmean_ref[...], q.astype(F32), _NT, precision=lax.Precision.HIGHEST,
                           preferred_element_type=F32)
    blk = lax.broadcasted_iota(jnp.int32, (nblk, t), 0)
    own = qi * bpt + lax.broadcasted_iota(jnp.int32, (nblk, t), 1) // MOBA_BLOCK
    neg_inf = jnp.float32(-jnp.inf)
    gate = jnp.where(blk < own, gate, neg_inf)
    sel = (blk == own).astype(F32)
    for _ in range(MOBA_TOPK):
        mx = jnp.max(gate, axis=0, keepdims=True)
        first = jnp.min(jnp.where(gate == mx, blk, nblk), axis=0, keepdims=True)
        pick = jnp.logical_and(blk == first, mx > neg_inf)
        sel = jnp.where(pick, 1.0, sel)
        gate = jnp.where(pick, neg_inf, gate)

    veto = jnp.concatenate([sel - 1.0, jnp.zeros((LANES - nblk, t), F32)], axis=0)
    q_ext = jnp.concatenate([q, veto.T.astype(BF16)], axis=1)
    lane_blk = lax.broadcasted_iota(jnp.int32, (t, LANES), 1)
    key_blk = lax.broadcasted_iota(jnp.int32, (t, LANES), 0) // MOBA_BLOCK

    row = lax.broadcasted_iota(jnp.int32, (t, t), 0)
    col = lax.broadcasted_iota(jnp.int32, (t, t), 1)
    own_future = jnp.logical_and(row // MOBA_BLOCK == col // MOBA_BLOCK, col > row)

    def scores(kc):
        off = pl.multiple_of(kc * t, t)
        hot = jnp.where(lane_blk == kc * bpt + key_blk, MOBA_VETO, 0.0).astype(BF16)
        k_ext = jnp.concatenate([k_ref[pl.ds(off, t), :], hot], axis=1)
        return lax.dot_general(q_ext, k_ext, _NT, preferred_element_type=F32)

    def update(kc, s, state, diagonal):
        if diagonal:
            s = jnp.where(own_future, -jnp.inf, s)
        off = pl.multiple_of(kc * t, t)
        return _softmax_update(s, state, v_ref[pl.ds(off, t), :])

    _, l, acc = _causal_sweep(qi, sa_ref, sb_ref, scores, update, _softmax_init(t, HEAD_DIM))
    o_ref[...] = (acc / l).astype(o_ref.dtype)


def _moba_attn(qkv, B, S, *, t=512):
    nq = S // t
    H = N_HEADS
    return pl.pallas_call(
        functools.partial(_moba_kernel, nblk=S // MOBA_BLOCK, t=t),
        grid=(B, H, nq),
        in_specs=[pl.BlockSpec((t, LANES), lambda b, h, i: (b * nq + i, h)),
                  pl.BlockSpec((S, LANES), lambda b, h, i: (b, H + h)),
                  pl.BlockSpec((S, LANES), lambda b, h, i: (b, 2 * H + h))],
        out_specs=pl.BlockSpec((t, LANES), lambda b, h, i: (b * nq + i, h)),
        out_shape=jax.ShapeDtypeStruct((B * S, D_MODEL), BF16),
        scratch_shapes=[pltpu.VMEM((S // MOBA_BLOCK, LANES), F32), pltpu.VMEM((t, t), F32),
                        pltpu.VMEM((t, t), F32)],
        compiler_params=_params("parallel", "parallel", "arbitrary"),
        name="moba_attn",
    )(qkv, qkv, qkv)


def _qkv_proj(h, w_in, layer, q_scale):
    return _proj(h, w_in, BF16, layer=layer, scaled_cols=D_MODEL, scale=q_scale)


def _sb_mixer(h, B, S, w_in, layer=0):
    return _sb_attn(_qkv_proj(h, w_in, layer, -float(HEAD_DIM ** -0.5)), B, S)


def _moba_mixer(h, B, S, w_in, layer=0):
    return _moba_attn(_qkv_proj(h, w_in, layer, float(HEAD_DIM ** -0.5)), B, S)


def kernel(x, mla_w_in, mla_q_norm, mla_kv_norm, mla_w_uq, mla_w_ukv, mla_w_o, hgrn_w_in, hgrn_lb_logits, hgrn_o_norm, hgrn_w_o, sb_w_in, sb_w_o, moba_w_in, moba_w_o, ln_g, ln_b, mlp_w1, mlp_w2):
    B, S, D = x.shape
    assert D == D_MODEL and S % MOBA_BLOCK == 0 and S % 512 == 0
    h = x.reshape(B * S, D)
    hb = h
    n_mixers = 4
    for i in range(DEPTH):
        kind, slot = i % n_mixers, i // n_mixers
        if kind == 0:
            o = _mla_mixer(hb, B, S, mla_w_in[slot], mla_q_norm[slot], mla_kv_norm[slot],
                           mla_w_uq[slot], mla_w_ukv[slot])
            w_o = mla_w_o[slot]
        elif kind == 1:
            proj = _proj(hb, hgrn_w_in, F32, layer=slot)
            o = _hgrn_mixer_core(proj, hgrn_lb_logits, hgrn_o_norm[slot], B, S, i)
            w_o = hgrn_w_o[slot]
        elif kind == 2:
            o = _sb_mixer(hb, B, S, sb_w_in, slot)
            w_o = sb_w_o[slot]
        else:
            o = _moba_mixer(hb, B, S, moba_w_in, slot)
            w_o = moba_w_o[slot]
        h, hb = _proj_res_ln(o, w_o.astype(BF16), h, ln_g[i, 0], ln_b[i, 0])
        a = _proj(hb, mlp_w1, BF16, layer=i, act="relu2")
        h, hb = _proj_res_ln(a, mlp_w2[i].astype(BF16), h, ln_g[i, 1], ln_b[i, 1])
    return h.reshape(B, S, D)
```

```python
import functools

import jax
import jax.numpy as jnp
from jax import lax
from jax.experimental import pallas as pl
from jax.experimental.pallas import tpu as pltpu

F32 = jnp.float32
BF16 = jnp.bfloat16

D_MODEL = 2048
DEPTH = 4
N_HEADS = 16
HEAD_DIM = 128
MLA_Q_LORA = 512
MLA_KV_LORA = 512
MLA_NOPE = 128
MLA_ROPE = 64
MLA_V = 128
ROPE_THETA = 10000.0
HGRN_CHUNK = 64
HGRN_SUB = 8
HGRN_GROUP = 8
MOBA_BLOCK = 256
MOBA_TOPK = 3
ALPHA = float((2 * DEPTH) ** 0.25)
LN_EPS = 1e-5
RMS_EPS = 1e-6

V7X_VMEM_BYTES = 64 * 1024 * 1024
VMEM_LIMIT = V7X_VMEM_BYTES - 8 * 1024 * 1024
LANES = 128
MASKED = -1e30
LOG2E = 1.4426950408889634

_NT = (((1,), (1,)), ((), ()))


def _params(*sem):
    return pltpu.CompilerParams(dimension_semantics=sem, vmem_limit_bytes=VMEM_LIMIT)


def _layer_norm_rows(y, g, b):
    mu = jnp.mean(y, axis=-1, keepdims=True)
    d = y - mu
    var = jnp.mean(d * d, axis=-1, keepdims=True)
    return d * lax.rsqrt(var + LN_EPS) * g + b


def _rms_rows(x, g):
    return x * lax.rsqrt(jnp.mean(x * x, axis=-1, keepdims=True) + RMS_EPS) * g


def _proj_kernel(x_ref, w_ref, o_ref, wbf_ref, *, act, scaled_tiles, scale):
    @pl.when(pl.program_id(1) == 0)
    def _():
        wbf_ref[...] = w_ref[...].astype(BF16)

    acc = jnp.dot(x_ref[...].astype(BF16), wbf_ref[...], preferred_element_type=F32)
    if act == "relu2":
        r = jnp.maximum(acc, 0.0)
        acc = r * r
    if scaled_tiles:
        acc = acc * jnp.where(pl.program_id(0) < scaled_tiles, scale, 1.0)
    o_ref[...] = acc.astype(o_ref.dtype)


def _proj(x, w, out_dtype, *, layer=0, tm=1024, tn=1024, act=None, scaled_cols=0, scale=1.0):
    M, K = x.shape
    N = w.shape[-1]
    tm, tn = min(tm, M), min(tn, N)
    assert scaled_cols % tn == 0 and M % tm == 0 and N % tn == 0
    if w.ndim == 3:
        w_spec = pl.BlockSpec((None, K, tn), lambda j, i: (layer, 0, j))
    else:
        w_spec = pl.BlockSpec((K, tn), lambda j, i: (0, j))
    return pl.pallas_call(
        functools.partial(_proj_kernel, act=act, scaled_tiles=scaled_cols // tn, scale=scale),
        grid=(N // tn, M // tm),
        in_specs=[pl.BlockSpec((tm, K), lambda j, i: (i, 0)), w_spec],
        out_specs=pl.BlockSpec((tm, tn), lambda j, i: (i, j)),
        out_shape=jax.ShapeDtypeStruct((M, N), out_dtype),
        scratch_shapes=[pltpu.VMEM((K, tn), BF16)],
        compiler_params=_params("parallel", "arbitrary"),
        name="proj",
    )(x, w)


def _proj_res_ln_kernel(x_ref, w_ref, h_ref, g_ref, b_ref, o_ref, ob_ref, acc_ref, *, nk):
    k = pl.program_id(1)
    def part():
        return jnp.dot(x_ref[...], w_ref[...], preferred_element_type=F32)

    def finish(y):
        out = _layer_norm_rows(ALPHA * h_ref[...] + y, g_ref[...], b_ref[...])
        o_ref[...] = out
        ob_ref[...] = out.astype(BF16)

    if nk == 1:
        finish(part())
    else:
        @pl.when(k == 0)
        def _():
            acc_ref[...] = part()

        @pl.when(jnp.logical_and(k > 0, k < nk - 1))
        def _():
            acc_ref[...] += part()

        @pl.when(k == nk - 1)
        def _():
            finish(acc_ref[...] + part())


def _proj_res_ln(x, w, h, g, b, *, tm=512, tk=2048):
    M, K = x.shape
    N = w.shape[1]
    nk = K // tk
    return pl.pallas_call(
        functools.partial(_proj_res_ln_kernel, nk=nk),
        grid=(M // tm, nk),
        in_specs=[pl.BlockSpec((tm, tk), lambda i, k: (i, k)),
                  pl.BlockSpec((tk, N), lambda i, k: (k, 0)),
                  pl.BlockSpec((tm, N), lambda i, k: (i, 0)),
                  pl.BlockSpec((1, N), lambda i, k: (0, 0)),
                  pl.BlockSpec((1, N), lambda i, k: (0, 0))],
        out_specs=[pl.BlockSpec((tm, N), lambda i, k: (i, 0)),
                   pl.BlockSpec((tm, N), lambda i, k: (i, 0))],
        out_shape=[jax.ShapeDtypeStruct((M, N), F32), jax.ShapeDtypeStruct((M, N), BF16)],
        scratch_shapes=[pltpu.VMEM((tm, N), F32)],
        compiler_params=_params("parallel", "arbitrary"),
        name="proj_res_ln",
    )(x, w, h, g.reshape(1, N), b.reshape(1, N))


def _mla_in_kernel(h_ref, w_ref, qg_ref, kvg_ref, ct_ref, st_ref, cq_ref, ckv_ref, kr_ref):
    acc = jnp.dot(h_ref[...].astype(BF16), w_ref[...], preferred_element_type=F32)
    ql, kvl = MLA_Q_LORA, MLA_KV_LORA
    cq_ref[...] = _rms_rows(acc[:, :ql], qg_ref[...]).astype(cq_ref.dtype)
    ckv_ref[...] = _rms_rows(acc[:, ql:ql + kvl], kvg_ref[...]).astype(ckv_ref.dtype)
    a = acc[:, ql + kvl:ql + kvl + LANES]
    a_sw = acc[:, ql + kvl + LANES:]
    kr_ref[...] = (a * ct_ref[...] + a_sw * st_ref[...]).astype(kr_ref.dtype)


def _mla_in(h, w_ext, qg, kvg, ct, st, S, *, tm=512):
    M, K = h.shape
    N = w_ext.shape[1]
    ns = S // tm
    return pl.pallas_call(
        _mla_in_kernel,
        grid=(M // tm,),
        in_specs=[pl.BlockSpec((tm, K), lambda i: (i, 0)),
                  pl.BlockSpec((K, N), lambda i: (0, 0)),
                  pl.BlockSpec((1, MLA_Q_LORA), lambda i: (0, 0)),
                  pl.BlockSpec((1, MLA_KV_LORA), lambda i: (0, 0)),
                  pl.BlockSpec((tm, LANES), lambda i: (i % ns, 0)),
                  pl.BlockSpec((tm, LANES), lambda i: (i % ns, 0))],
        out_specs=[pl.BlockSpec((tm, MLA_Q_LORA), lambda i: (i, 0)),
                   pl.BlockSpec((tm, MLA_KV_LORA), lambda i: (i, 0)),
                   pl.BlockSpec((tm, LANES), lambda i: (i, 0))],
        out_shape=[jax.ShapeDtypeStruct((M, MLA_Q_LORA), BF16),
                   jax.ShapeDtypeStruct((M, MLA_KV_LORA), BF16),
                   jax.ShapeDtypeStruct((M, LANES), BF16)],
        compiler_params=_params("parallel"),
        name="mla_in",
    )(h, w_ext, qg.reshape(1, -1), kvg.reshape(1, -1), ct, st)


def _mla_uq_kernel(cq_ref, wm_ref, ws_ref, ct_ref, st_ref, q_ref, *, heads, scale):
    x = cq_ref[...]
    a = jnp.dot(x, wm_ref[...], preferred_element_type=F32)
    a_sw = jnp.dot(x, ws_ref[...], preferred_element_type=F32)
    ct = ct_ref[...]
    st = st_ref[...]
    for hh in range(heads):
        lo = hh * 2 * LANES
        q_ref[:, lo:lo + LANES] = (a[:, lo:lo + LANES] * scale).astype(q_ref.dtype)
        rot = a[:, lo + LANES:lo + 2 * LANES] * ct + a_sw[:, hh * LANES:(hh + 1) * LANES] * st
        q_ref[:, lo + LANES:lo + 2 * LANES] = (rot * scale).astype(q_ref.dtype)


def _mla_uq(cq, w_main, w_sw, ct, st, S, scale, *, tm=512, heads_per_step=4):
    M, K = cq.shape
    hp = heads_per_step
    ns = S // tm
    return pl.pallas_call(
        functools.partial(_mla_uq_kernel, heads=hp, scale=scale),
        grid=(M // tm, N_HEADS // hp),
        in_specs=[pl.BlockSpec((tm, K), lambda i, j: (i, 0)),
                  pl.BlockSpec((K, hp * 2 * LANES), lambda i, j: (0, j)),
                  pl.BlockSpec((K, hp * LANES), lambda i, j: (0, j)),
                  pl.BlockSpec((tm, LANES), lambda i, j: (i % ns, 0)),
                  pl.BlockSpec((tm, LANES), lambda i, j: (i % ns, 0))],
        out_specs=pl.BlockSpec((tm, hp * 2 * LANES), lambda i, j: (i, j)),
        out_shape=jax.ShapeDtypeStruct((M, N_HEADS * 2 * LANES), BF16),
        compiler_params=_params("parallel", "parallel"),
        name="mla_uq",
    )(cq, w_main, w_sw, ct, st)


def _causal_sweep(qi, sa, sb, scores, update, init):
    sa[...] = scores(0)

    def pair(p, state):
        k0 = 2 * p
        sb[...] = scores(k0 + 1)
        state = update(k0, sa[...], state, False)
        sa[...] = scores(k0 + 2)
        return update(k0 + 1, sb[...], state, False)

    state = lax.fori_loop(0, qi // 2, pair, init)

    def odd(state):
        sb[...] = scores(qi)
        state = update(qi - 1, sa[...], state, False)
        return update(qi, sb[...], state, True)

    def even(state):
        return update(qi, sa[...], state, True)

    return lax.cond(qi % 2 == 1, odd, even, state)


def _softmax_update(s, state, v):
    m, acc = state
    m_new = jnp.maximum(m, jnp.max(s, axis=-1, keepdims=True))
    alpha = jnp.exp(m - m_new)
    p = jnp.exp(s - m_new).astype(BF16)
    v_ones = jnp.concatenate([v, jnp.ones_like(v)], axis=1)
    acc = alpha * acc + jnp.dot(p, v_ones, preferred_element_type=F32)
    return m_new, acc


def _softmax_init(t):
    return (jnp.full((t, 1), MASKED, F32), jnp.zeros((t, 2 * LANES), F32))


def _softmax_finish(state):
    _, acc = state
    return acc[:, :LANES] / acc[:, LANES:]


def _mla_attn_kernel(q_ref, kn_ref, kr_ref, v_ref, o_ref, sa_ref, sb_ref, *, t):
    qi = pl.program_id(2)
    q = q_ref[...]
    row = lax.broadcasted_iota(jnp.int32, (t, t), 0)
    col = lax.broadcasted_iota(jnp.int32, (t, t), 1)

    def scores(ki):
        off = pl.multiple_of(ki * t, t)
        k = jnp.concatenate([kn_ref[pl.ds(off, t), :], kr_ref[pl.ds(off, t), :]], axis=1)
        return lax.dot_general(q, k, _NT, preferred_element_type=F32)

    def update(ki, s, state, diagonal):
        if diagonal:
            s = jnp.where(col <= row, s, -jnp.inf)
        off = pl.multiple_of(ki * t, t)
        return _softmax_update(s, state, v_ref[pl.ds(off, t), :])

    state = _causal_sweep(qi, sa_ref, sb_ref, scores, update, _softmax_init(t))
    o_ref[...] = _softmax_finish(state).astype(o_ref.dtype)


def _mla_attn(q, kv, kr, B, S, *, t=512):
    nq = S // t
    H = N_HEADS
    return pl.pallas_call(
        functools.partial(_mla_attn_kernel, t=t),
        grid=(B, H, nq),
        in_specs=[pl.BlockSpec((t, 2 * LANES), lambda b, h, i: (b * nq + i, h)),
                  pl.BlockSpec((S, LANES), lambda b, h, i: (b, h)),
                  pl.BlockSpec((S, LANES), lambda b, h, i: (b, 0)),
                  pl.BlockSpec((S, LANES), lambda b, h, i: (b, H + h))],
        out_specs=pl.BlockSpec((t, LANES), lambda b, h, i: (b * nq + i, h)),
        out_shape=jax.ShapeDtypeStruct((B * S, H * MLA_V), BF16),
        scratch_shapes=[pltpu.VMEM((t, t), F32), pltpu.VMEM((t, t), F32)],
        compiler_params=_params("parallel", "parallel", "arbitrary"),
        name="mla_attn",
    )(q, kv, kr, kv)


def _rope_tables(S):
    half = MLA_ROPE // 2
    inv = 1.0 / (ROPE_THETA ** (jnp.arange(0, MLA_ROPE, 2, dtype=F32) / MLA_ROPE))
    ang = jnp.arange(S, dtype=F32)[:, None] * inv[None, :]
    cos, sin = jnp.cos(ang), jnp.sin(ang)
    zeros = jnp.zeros((S, LANES - 2 * half), F32)
    return (jnp.concatenate([cos, cos, zeros], axis=1), jnp.concatenate([-sin, sin, zeros], axis=1))


def _mla_weights(w_in, w_uq, w_ukv):
    D = w_in.shape[0]
    half = MLA_ROPE // 2
    base = MLA_Q_LORA + MLA_KV_LORA
    x1, x2 = w_in[:, base:base + half], w_in[:, base + half:base + 2 * half]
    pad = jnp.zeros((D, LANES - 2 * half), w_in.dtype)
    w_in_ext = jnp.concatenate([w_in[:, :base], x1, x2, pad, x2, x1, pad], axis=1).astype(BF16)

    wq = w_uq.reshape(MLA_Q_LORA, N_HEADS, MLA_NOPE + MLA_ROPE)
    nope, r1, r2 = wq[..., :MLA_NOPE], wq[..., MLA_NOPE:MLA_NOPE + half], wq[..., MLA_NOPE + half:]
    padq = jnp.zeros((MLA_Q_LORA, N_HEADS, LANES - 2 * half), w_uq.dtype)
    w_main = jnp.concatenate([nope, r1, r2, padq], axis=-1).reshape(MLA_Q_LORA, N_HEADS * 2 * LANES).astype(BF16)
    w_sw = jnp.concatenate([r2, r1, padq], axis=-1).reshape(MLA_Q_LORA, N_HEADS * LANES).astype(BF16)

    wkv = w_ukv.reshape(MLA_KV_LORA, N_HEADS, MLA_NOPE + MLA_V)
    w_kv = jnp.concatenate([wkv[..., :MLA_NOPE].reshape(MLA_KV_LORA, -1),
                            wkv[..., MLA_NOPE:].reshape(MLA_KV_LORA, -1)], axis=1).astype(BF16)
    return w_in_ext, w_main, w_sw, w_kv


def _mla_mixer(h, B, S, w_in, q_norm, kv_norm, w_uq, w_ukv):
    w_in_ext, w_main, w_sw, w_kv = _mla_weights(w_in, w_uq, w_ukv)
    ct, st = _rope_tables(S)
    cq, ckv, kr = _mla_in(h, w_in_ext, q_norm, kv_norm, ct, st, S)
    scale = float((MLA_NOPE + MLA_ROPE) ** -0.5)
    q = _mla_uq(cq, w_main, w_sw, ct, st, S, scale)
    kv = _proj(ckv, w_kv, BF16)
    return _mla_attn(q, kv, kr, B, S)


def _hgrn_kernel(q_ref, f_ref, i_ref, g_ref, lbl_ref, on_ref, o_ref, state_ref, b_scr, k_scr,
                 *, layer, tile, chunk):
    C, SB = chunk, HGRN_SUB
    nb = C // SB

    @pl.when(pl.program_id(2) == 0)
    def _():
        state_ref[...] = jnp.zeros_like(state_ref)

    lg = lbl_ref[...]
    e = jnp.exp(lg - jnp.max(lg, axis=0, keepdims=True))
    p = e / jnp.sum(e, axis=0, keepdims=True)
    cs = p[0:1]
    for r in range(1, layer + 1):
        cs = cs + p[r:r + 1]
    lb = cs - p[0:1]
    log_lb = jnp.log(lb)
    log1m_lb = jnp.log1p(-lb)
    one_m_lb = 1.0 - lb
    onorm = on_ref[...]

    row = lax.broadcasted_iota(jnp.int32, (C, 3 * C), 0)
    col = lax.broadcasted_iota(jnp.int32, (C, 3 * C), 1) % C
    blk0 = (row // SB) * SB
    tri = jnp.concatenate([col <= row, col <= blk0, col <= jnp.minimum(blk0 + SB, C - 1)],
                          axis=0).astype(BF16)
    brow = lax.broadcasted_iota(jnp.int32, (C, C), 0) // SB
    bcol = lax.broadcasted_iota(jnp.int32, (C, C), 1) // SB
    bdiff = brow - bcol
    sub = lax.broadcasted_iota(jnp.int32, (SB, LANES), 0)
    causal_cap = [jnp.where(sub >= s, 0.0, -jnp.inf).astype(F32) for s in range(SB)]

    def chunk_step(c, slot):
        off = pl.multiple_of(c * C, C)
        q = q_ref[pl.ds(off, C), :]
        fp = f_ref[pl.ds(off, C), :]
        v = i_ref[pl.ds(off, C), :]
        g = g_ref[pl.ds(off, C), :]

        ls = jnp.minimum(fp, 0.0) - jnp.log(1.0 + jnp.exp(-jnp.abs(fp)))
        cc = log1m_lb + ls
        lf = jnp.maximum(log_lb, cc) + jnp.log(1.0 + jnp.exp(-jnp.abs(log_lb - cc)))
        kk = one_m_lb * jax.nn.sigmoid(-fp)

        p1 = lf.astype(BF16)
        r1 = lf - p1.astype(F32)
        p2 = r1.astype(BF16)
        p3 = (r1 - p2.astype(F32)).astype(BF16)
        cums = jnp.dot(tri, jnp.concatenate([p1, p2, p3], axis=0), preferred_element_type=F32)
        b, r, r_next = cums[:C], cums[C:2 * C], cums[2 * C:]
        bend = b[C - 1:C, :]
        b_scr[slot] = b
        k_scr[slot] = kk

        st_t = state_ref[...]
        qe = q * jnp.exp(b)
        o = lax.dot_general(qe.astype(BF16), st_t.astype(BF16), _NT, preferred_element_type=F32)
        kd = kk * jnp.exp(bend - b)
        state_ref[...] = st_t * jnp.exp(bend) + jnp.dot(v.T.astype(BF16), kd.astype(BF16),
                                                       preferred_element_type=F32)

        kt = kk * jnp.exp(r_next - b)
        gdec = jnp.exp(r_next - r)
        ql = q * jnp.exp(b - r)
        levels = [ql]
        for lvl in range(1, nb - 1):
            fac = jnp.concatenate([jnp.zeros((lvl * SB, LANES), F32), gdec[:C - lvl * SB]], axis=0)
            ql = ql * fac
            levels.append(ql)
        qs = jnp.concatenate(levels, axis=0).astype(BF16)
        rl = lax.dot_general(qs, kt.astype(BF16), _NT, preferred_element_type=F32)
        a = jnp.zeros((C, C), F32)
        for lvl in range(1, nb):
            a = a + jnp.where(bdiff == lvl, rl[(lvl - 1) * C:lvl * C, :], 0.0)
        o = o + jnp.dot(a.astype(BF16), v.astype(BF16), preferred_element_type=F32)

        diag = []
        for blk in range(nb):
            bq = b[blk * SB:(blk + 1) * SB]
            qq = q[blk * SB:(blk + 1) * SB]
            acc = jnp.zeros((SB, LANES), F32)
            for s in range(SB):
                rr = blk * SB + s
                dec = jnp.exp(jnp.minimum(bq - b_scr[slot, pl.ds(rr, 1), :], causal_cap[s]))
                a_ts = jnp.sum(qq * k_scr[slot, pl.ds(rr, 1), :] * dec, axis=-1, keepdims=True)
                acc = acc + a_ts * i_ref[pl.ds(off + rr, 1), :]
            diag.append(acc)
        o = o + jnp.concatenate(diag, axis=0)

        y = _rms_rows(o, onorm)
        o_ref[pl.ds(off, C), :] = (y * (g * jax.nn.sigmoid(g))).astype(o_ref.dtype)

    def chunk_group(grp, carry):
        for slot in range(HGRN_GROUP):
            chunk_step(HGRN_GROUP * grp + slot, slot)
        return carry

    lax.fori_loop(0, tile // (HGRN_GROUP * C), chunk_group, 0)


def _hgrn_mixer_core(proj, lb_logits, o_norm, B, S, layer, *, tile=512):
    nt = S // tile
    H = N_HEADS
    blk = lambda sec: pl.BlockSpec((tile, LANES), lambda b, h, t, sec=sec: (b * nt + t, sec * H + h))
    return pl.pallas_call(
        functools.partial(_hgrn_kernel, layer=layer, tile=tile, chunk=HGRN_CHUNK),
        grid=(B, H, nt),
        in_specs=[blk(0), blk(1), blk(2), blk(3),
                  pl.BlockSpec((DEPTH, LANES), lambda b, h, t: (0, h)),
                  pl.BlockSpec((1, LANES), lambda b, h, t: (0, h))],
        out_specs=pl.BlockSpec((tile, LANES), lambda b, h, t: (b * nt + t, h)),
        out_shape=jax.ShapeDtypeStruct((B * S, D_MODEL), BF16),
        scratch_shapes=[pltpu.VMEM((LANES, LANES), F32),
                        pltpu.VMEM((HGRN_GROUP, HGRN_CHUNK, LANES), F32),
                        pltpu.VMEM((HGRN_GROUP, HGRN_CHUNK, LANES), F32)],
        compiler_params=_params("parallel", "parallel", "arbitrary"),
        name="hgrn",
    )(proj, proj, proj, proj, lb_logits, o_norm.reshape(1, -1))


SB_CUMSUM_BLOCK = 256


def _sb_kernel(q_ref, k_ref, v_ref, o_ref, sa_ref, sb_ref, *, t):
    cb = SB_CUMSUM_BLOCK
    qi = pl.program_id(2)
    q = q_ref[...]
    row = lax.broadcasted_iota(jnp.int32, (t, t), 0)
    col = lax.broadcasted_iota(jnp.int32, (t, t), 1)
    strict = col < row
    jj = lax.broadcasted_iota(jnp.int32, (2 * cb, cb), 0) % cb
    ss = lax.broadcasted_iota(jnp.int32, (2 * cb, cb), 1)
    from_s2 = (jj >= ss).astype(BF16)

    def scores(ki):
        off = pl.multiple_of(ki * t, t)
        return lax.dot_general(q, k_ref[pl.ds(off, t), :], _NT, preferred_element_type=F32)

    def update(ki, zn, acc, diagonal):
        l1m = jnp.minimum(zn, 0.0) - jnp.log(1.0 + jnp.exp2(jnp.abs(zn) * (-LOG2E)))
        if diagonal:
            l1m = jnp.where(strict, l1m, 0.0)
        parts = []
        tail = None
        for j in reversed(range(t // cb)):
            lj = l1m[:, j * cb:(j + 1) * cb]
            hi = lj.astype(BF16)
            lo = (lj - hi.astype(F32)).astype(BF16)
            x = jnp.dot(jnp.concatenate([hi, lo], axis=1), from_s2, preferred_element_type=F32)
            if tail is not None:
                x = x + tail
            parts.insert(0, x)
            tail = x[:, 0:1]
        incl = jnp.concatenate(parts, axis=1)
        w = jnp.exp(incl - zn)
        if diagonal:
            w = jnp.where(strict, w, 0.0)
        off = pl.multiple_of(ki * t, t)
        return acc * jnp.exp(tail) + jnp.dot(w.astype(BF16), v_ref[pl.ds(off, t), :],
                                             preferred_element_type=F32)

    acc = _causal_sweep(qi, sa_ref, sb_ref, scores, update, jnp.zeros((t, HEAD_DIM), F32))
    o_ref[...] = acc.astype(o_ref.dtype)


def _sb_attn(qkv, B, S, *, t=512):
    nq = S // t
    H = N_HEADS
    return pl.pallas_call(
        functools.partial(_sb_kernel, t=t),
        grid=(B, H, nq),
        in_specs=[pl.BlockSpec((t, LANES), lambda b, h, i: (b * nq + i, h)),
                  pl.BlockSpec((S, LANES), lambda b, h, i: (b, H + h)),
                  pl.BlockSpec((S, LANES), lambda b, h, i: (b, 2 * H + h))],
        out_specs=pl.BlockSpec((t, LANES), lambda b, h, i: (b * nq + i, h)),
        out_shape=jax.ShapeDtypeStruct((B * S, D_MODEL), BF16),
        scratch_shapes=[pltpu.VMEM((t, t), F32), pltpu.VMEM((t, t), F32)],
        compiler_params=_params("parallel", "parallel", "arbitrary"),
        name="sb_attn",
    )(qkv, qkv, qkv)


MOBA_VETO = 2.0 ** 100


def _moba_kernel(q_ref, k_ref, v_ref, o_ref, kmean_ref, sa_ref, sb_ref, *, nblk, t):
    bpt = t // MOBA_BLOCK
    qi = pl.program_id(2)

    @pl.when(qi == 0)
    def _():
        kmean_ref[...] = jnp.mean(k_ref[...].astype(F32).reshape(nblk, MOBA_BLOCK, LANES), axis=1)

    q = q_ref[...]

    gate = lax.dot_general(kmean_ref[...], q.astype(F32), _NT, precision=lax.Precision.HIGHEST,
                           preferred_element_type=F32)
    blk = lax.broadcasted_iota(jnp.int32, (nblk, t), 0)
    own = qi * bpt + lax.broadcasted_iota(jnp.int32, (nblk, t), 1) // MOBA_BLOCK
    neg_inf = jnp.float32(-jnp.inf)
    gate = jnp.where(blk < own, gate, neg_inf)
    sel = (blk == own).astype(F32)
    for _ in range(MOBA_TOPK):
        mx = jnp.max(gate, axis=0, keepdims=True)
        first = jnp.min(jnp.where(gate == mx, blk, nblk), axis=0, keepdims=True)
        pick = jnp.logical_and(blk == first, mx > neg_inf)
        sel = jnp.where(pick, 1.0, sel)
        gate = jnp.where(pick, neg_inf, gate)

    veto = jnp.concatenate([sel - 1.0, jnp.zeros((LANES - nblk, t), F32)], axis=0)
    q_ext = jnp.concatenate([q, veto.T.astype(BF16)], axis=1)
    lane_blk = lax.broadcasted_iota(jnp.int32, (t, LANES), 1)
    key_blk = lax.broadcasted_iota(jnp.int32, (t, LANES), 0) // MOBA_BLOCK

    row = lax.broadcasted_iota(jnp.int32, (t, t), 0)
    col = lax.broadcasted_iota(jnp.int32, (t, t), 1)
    own_future = jnp.logical_and(row // MOBA_BLOCK == col // MOBA_BLOCK, col > row)

    def scores(kc):
        off = pl.multiple_of(kc * t, t)
        hot = jnp.where(lane_blk == kc * bpt + key_blk, MOBA_VETO, 0.0).astype(BF16)
        k_ext = jnp.concatenate([k_ref[pl.ds(off, t), :], hot], axis=1)
        return lax.dot_general(q_ext, k_ext, _NT, preferred_element_type=F32)

    def update(kc, s, state, diagonal):
        if diagonal:
            s = jnp.where(own_future, -jnp.inf, s)
        off = pl.multiple_of(kc * t, t)
        return _softmax_update(s, state, v_ref[pl.ds(off, t), :])

    state = _causal_sweep(qi, sa_ref, sb_ref, scores, update, _softmax_init(t))
    o_ref[...] = _softmax_finish(state).astype(o_ref.dtype)


def _moba_attn(qkv, B, S, *, t=512):
    nq = S // t
    H = N_HEADS
    return pl.pallas_call(
        functools.partial(_moba_kernel, nblk=S // MOBA_BLOCK, t=t),
        grid=(B, H, nq),
        in_specs=[pl.BlockSpec((t, LANES), lambda b, h, i: (b * nq + i, h)),
                  pl.BlockSpec((S, LANES), lambda b, h, i: (b, H + h)),
                  pl.BlockSpec((S, LANES), lambda b, h, i: (b, 2 * H + h))],
        out_specs=pl.BlockSpec((t, LANES), lambda b, h, i: (b * nq + i, h)),
        out_shape=jax.ShapeDtypeStruct((B * S, D_MODEL), BF16),
        scratch_shapes=[pltpu.VMEM((S // MOBA_BLOCK, LANES), F32), pltpu.VMEM((t, t), F32),
                        pltpu.VMEM((t, t), F32)],
        compiler_params=_params("parallel", "parallel", "arbitrary"),
        name="moba_attn",
    )(qkv, qkv, qkv)


def _qkv_proj(h, w_in, layer, q_scale):
    return _proj(h, w_in, BF16, layer=layer, scaled_cols=D_MODEL, scale=q_scale)


def _sb_mixer(h, B, S, w_in, layer=0):
    return _sb_attn(_qkv_proj(h, w_in, layer, -float(HEAD_DIM ** -0.5)), B, S)


def _moba_mixer(h, B, S, w_in, layer=0):
    return _moba_attn(_qkv_proj(h, w_in, layer, float(HEAD_DIM ** -0.5)), B, S)


def kernel(x, mla_w_in, mla_q_norm, mla_kv_norm, mla_w_uq, mla_w_ukv, mla_w_o, hgrn_w_in, hgrn_lb_logits, hgrn_o_norm, hgrn_w_o, sb_w_in, sb_w_o, moba_w_in, moba_w_o, ln_g, ln_b, mlp_w1, mlp_w2):
    B, S, D = x.shape
    assert D == D_MODEL and S % MOBA_BLOCK == 0 and S % 512 == 0
    h = x.reshape(B * S, D)
    hb = h
    n_mixers = 4
    for i in range(DEPTH):
        kind, slot = i % n_mixers, i // n_mixers
        if kind == 0:
            o = _mla_mixer(hb, B, S, mla_w_in[slot], mla_q_norm[slot], mla_kv_norm[slot],
                           mla_w_uq[slot], mla_w_ukv[slot])
            w_o = mla_w_o[slot]
        elif kind == 1:
            proj = _proj(hb, hgrn_w_in, F32, layer=slot)
            o = _hgrn_mixer_core(proj, hgrn_lb_logits, hgrn_o_norm[slot], B, S, i)
            w_o = hgrn_w_o[slot]
        elif kind == 2:
            o = _sb_mixer(hb, B, S, sb_w_in, slot)
            w_o = sb_w_o[slot]
        else:
            o = _moba_mixer(hb, B, S, moba_w_in, slot)
            w_o = moba_w_o[slot]
        h, hb = _proj_res_ln(o, w_o.astype(BF16), h, ln_g[i, 0], ln_b[i, 0])
        a = _proj(hb, mlp_w1, BF16, layer=i, act="relu2")
        h, hb = _proj_res_ln(a, mlp_w2[i].astype(BF16), h, ln_g[i, 1], ln_b[i, 1])
    return h.reshape(B, S, D)
```

```python
import functools

import jax
import jax.numpy as jnp
from jax import lax
from jax.experimental import pallas as pl
from jax.experimental.pallas import tpu as pltpu

F32 = jnp.float32
BF16 = jnp.bfloat16

D_MODEL = 2048
DEPTH = 4
N_HEADS = 16
HEAD_DIM = 128
MLA_Q_LORA = 512
MLA_KV_LORA = 512
MLA_NOPE = 128
MLA_ROPE = 64
MLA_V = 128
ROPE_THETA = 10000.0
HGRN_CHUNK = 64
HGRN_SUB = 8
HGRN_GROUP = 8
MOBA_BLOCK = 256
MOBA_TOPK = 3
ALPHA = float((2 * DEPTH) ** 0.25)
LN_EPS = 1e-5
RMS_EPS = 1e-6

V7X_VMEM_BYTES = 64 * 1024 * 1024
VMEM_LIMIT = V7X_VMEM_BYTES - 8 * 1024 * 1024
LANES = 128
MASKED = -1e30
LOG2E = 1.4426950408889634

_NT = (((1,), (1,)), ((), ()))


def _params(*sem):
    return pltpu.CompilerParams(dimension_semantics=sem, vmem_limit_bytes=VMEM_LIMIT)


def _layer_norm_rows(y, g, b):
    mu = jnp.mean(y, axis=-1, keepdims=True)
    d = y - mu
    var = jnp.mean(d * d, axis=-1, keepdims=True)
    return d * lax.rsqrt(var + LN_EPS) * g + b


def _rms_rows(x, g):
    return x * lax.rsqrt(jnp.mean(x * x, axis=-1, keepdims=True) + RMS_EPS) * g


def _proj_kernel(x_ref, w_ref, o_ref, wbf_ref, *, act, scaled_tiles, scale):
    @pl.when(pl.program_id(1) == 0)
    def _():
        wbf_ref[...] = w_ref[...].astype(BF16)

    acc = jnp.dot(x_ref[...].astype(BF16), wbf_ref[...], preferred_element_type=F32)
    if act == "relu2":
        r = jnp.maximum(acc, 0.0)
        acc = r * r
    if scaled_tiles:
        acc = acc * jnp.where(pl.program_id(0) < scaled_tiles, scale, 1.0)
    o_ref[...] = acc.astype(o_ref.dtype)


def _proj(x, w, out_dtype, *, layer=0, tm=1024, tn=1024, act=None, scaled_cols=0, scale=1.0):
    M, K = x.shape
    N = w.shape[-1]
    tm, tn = min(tm, M), min(tn, N)
    assert scaled_cols % tn == 0 and M % tm == 0 and N % tn == 0
    if w.ndim == 3:
        w_spec = pl.BlockSpec((None, K, tn), lambda j, i: (layer, 0, j))
    else:
        w_spec = pl.BlockSpec((K, tn), lambda j, i: (0, j))
    return pl.pallas_call(
        functools.partial(_proj_kernel, act=act, scaled_tiles=scaled_cols // tn, scale=scale),
        grid=(N // tn, M // tm),
        in_specs=[pl.BlockSpec((tm, K), lambda j, i: (i, 0)), w_spec],
        out_specs=pl.BlockSpec((tm, tn), lambda j, i: (i, j)),
        out_shape=jax.ShapeDtypeStruct((M, N), out_dtype),
        scratch_shapes=[pltpu.VMEM((K, tn), BF16)],
        compiler_params=_params("parallel", "arbitrary"),
        name="proj",
    )(x, w)


def _proj_res_ln_kernel(x_ref, w_ref, h_ref, g_ref, b_ref, o_ref, ob_ref, acc_ref, *, nk):
    k = pl.program_id(1)
    def part():
        return jnp.dot(x_ref[...], w_ref[...], preferred_element_type=F32)

    def finish(y):
        out = _layer_norm_rows(ALPHA * h_ref[...] + y, g_ref[...], b_ref[...])
        o_ref[...] = out
        ob_ref[...] = out.astype(BF16)

    if nk == 1:
        finish(part())
    else:
        @pl.when(k == 0)
        def _():
            acc_ref[...] = part()

        @pl.when(jnp.logical_and(k > 0, k < nk - 1))
        def _():
            acc_ref[...] += part()

        @pl.when(k == nk - 1)
        def _():
            finish(acc_ref[...] + part())


def _proj_res_ln(x, w, h, g, b, *, layer=0, tm=512, tk=2048):
    M, K = x.shape
    N = w.shape[-1]
    nk = K // tk
    if w.ndim == 3:
        w_spec = pl.BlockSpec((None, tk, N), lambda i, k: (layer, k, 0))
    else:
        w_spec = pl.BlockSpec((tk, N), lambda i, k: (k, 0))
    return pl.pallas_call(
        functools.partial(_proj_res_ln_kernel, nk=nk),
        grid=(M // tm, nk),
        in_specs=[pl.BlockSpec((tm, tk), lambda i, k: (i, k)),
                  w_spec,
                  pl.BlockSpec((tm, N), lambda i, k: (i, 0)),
                  pl.BlockSpec((1, N), lambda i, k: (0, 0)),
                  pl.BlockSpec((1, N), lambda i, k: (0, 0))],
        out_specs=[pl.BlockSpec((tm, N), lambda i, k: (i, 0)),
                   pl.BlockSpec((tm, N), lambda i, k: (i, 0))],
        out_shape=[jax.ShapeDtypeStruct((M, N), F32), jax.ShapeDtypeStruct((M, N), BF16)],
        scratch_shapes=[pltpu.VMEM((tm, N), F32)],
        compiler_params=_params("parallel", "arbitrary"),
        name="proj_res_ln",
    )(x, w, h, g.reshape(1, N), b.reshape(1, N))


def _mla_in_kernel(h_ref, w_ref, qg_ref, kvg_ref, ct_ref, st_ref, cq_ref, ckv_ref, kr_ref):
    acc = jnp.dot(h_ref[...].astype(BF16), w_ref[...], preferred_element_type=F32)
    ql, kvl = MLA_Q_LORA, MLA_KV_LORA
    cq_ref[...] = _rms_rows(acc[:, :ql], qg_ref[...]).astype(cq_ref.dtype)
    ckv_ref[...] = _rms_rows(acc[:, ql:ql + kvl], kvg_ref[...]).astype(ckv_ref.dtype)
    a = acc[:, ql + kvl:ql + kvl + LANES]
    a_sw = acc[:, ql + kvl + LANES:]
    kr_ref[...] = (a * ct_ref[...] + a_sw * st_ref[...]).astype(kr_ref.dtype)


def _mla_in(h, w_ext, qg, kvg, ct, st, S, *, tm=512):
    M, K = h.shape
    N = w_ext.shape[1]
    ns = S // tm
    return pl.pallas_call(
        _mla_in_kernel,
        grid=(M // tm,),
        in_specs=[pl.BlockSpec((tm, K), lambda i: (i, 0)),
                  pl.BlockSpec((K, N), lambda i: (0, 0)),
                  pl.BlockSpec((1, MLA_Q_LORA), lambda i: (0, 0)),
                  pl.BlockSpec((1, MLA_KV_LORA), lambda i: (0, 0)),
                  pl.BlockSpec((tm, LANES), lambda i: (i % ns, 0)),
                  pl.BlockSpec((tm, LANES), lambda i: (i % ns, 0))],
        out_specs=[pl.BlockSpec((tm, MLA_Q_LORA), lambda i: (i, 0)),
                   pl.BlockSpec((tm, MLA_KV_LORA), lambda i: (i, 0)),
                   pl.BlockSpec((tm, LANES), lambda i: (i, 0))],
        out_shape=[jax.ShapeDtypeStruct((M, MLA_Q_LORA), BF16),
                   jax.ShapeDtypeStruct((M, MLA_KV_LORA), BF16),
                   jax.ShapeDtypeStruct((M, LANES), BF16)],
        compiler_params=_params("parallel"),
        name="mla_in",
    )(h, w_ext, qg.reshape(1, -1), kvg.reshape(1, -1), ct, st)


def _mla_uq_kernel(cq_ref, wm_ref, ws_ref, ct_ref, st_ref, q_ref, *, heads, scale):
    x = cq_ref[...]
    a = jnp.dot(x, wm_ref[...], preferred_element_type=F32)
    a_sw = jnp.dot(x, ws_ref[...], preferred_element_type=F32)
    ct = ct_ref[...]
    st = st_ref[...]
    for hh in range(heads):
        lo = hh * 2 * LANES
        q_ref[:, lo:lo + LANES] = (a[:, lo:lo + LANES] * scale).astype(q_ref.dtype)
        rot = a[:, lo + LANES:lo + 2 * LANES] * ct + a_sw[:, hh * LANES:(hh + 1) * LANES] * st
        q_ref[:, lo + LANES:lo + 2 * LANES] = (rot * scale).astype(q_ref.dtype)


def _mla_uq(cq, w_main, w_sw, ct, st, S, scale, *, tm=512, heads_per_step=4):
    M, K = cq.shape
    hp = heads_per_step
    ns = S // tm
    return pl.pallas_call(
        functools.partial(_mla_uq_kernel, heads=hp, scale=scale),
        grid=(M // tm, N_HEADS // hp),
        in_specs=[pl.BlockSpec((tm, K), lambda i, j: (i, 0)),
                  pl.BlockSpec((K, hp * 2 * LANES), lambda i, j: (0, j)),
                  pl.BlockSpec((K, hp * LANES), lambda i, j: (0, j)),
                  pl.BlockSpec((tm, LANES), lambda i, j: (i % ns, 0)),
                  pl.BlockSpec((tm, LANES), lambda i, j: (i % ns, 0))],
        out_specs=pl.BlockSpec((tm, hp * 2 * LANES), lambda i, j: (i, j)),
        out_shape=jax.ShapeDtypeStruct((M, N_HEADS * 2 * LANES), BF16),
        compiler_params=_params("parallel", "parallel"),
        name="mla_uq",
    )(cq, w_main, w_sw, ct, st)


def _causal_sweep(qi, sa, sb, scores, update, init):
    sa[...] = scores(0)

    def pair(p, state):
        k0 = 2 * p
        sb[...] = scores(k0 + 1)
        state = update(k0, sa, state, False)
        sa[...] = scores(k0 + 2)
        return update(k0 + 1, sb, state, False)

    state = lax.fori_loop(0, qi // 2, pair, init)

    def odd(state):
        sb[...] = scores(qi)
        state = update(qi - 1, sa, state, False)
        return update(qi, sb, state, True)

    def even(state):
        return update(qi, sa, state, True)

    return lax.cond(qi % 2 == 1, odd, even, state)


def _softmax_update(s, state, v):
    m, acc = state
    m_new = jnp.maximum(m, jnp.max(s, axis=-1, keepdims=True))
    alpha = jnp.exp2(m - m_new)
    p = jnp.exp2(s - m_new).astype(BF16)
    v_ones = jnp.concatenate([v, jnp.ones_like(v)], axis=1)
    acc = alpha * acc + jnp.dot(p, v_ones, preferred_element_type=F32)
    return m_new, acc


def _softmax_update_diag(s_ref, state, v_ref, off, t):
    h = t // 2
    m, acc = state
    keep = (lax.broadcasted_iota(jnp.int32, (h, h), 1) <= lax.broadcasted_iota(jnp.int32, (h, h), 0))
    s_top = jnp.where(keep, s_ref[0:h, 0:h], -jnp.inf)
    top = _softmax_update(s_top, (m[:h], acc[:h]), v_ref[pl.ds(off, h), :])
    s_bot = jnp.concatenate([s_ref[h:t, 0:h], jnp.where(keep, s_ref[h:t, h:t], -jnp.inf)], axis=1)
    bot = _softmax_update(s_bot, (m[h:], acc[h:]), v_ref[pl.ds(off, t), :])
    return jnp.concatenate([top[0], bot[0]], axis=0), jnp.concatenate([top[1], bot[1]], axis=0)


def _softmax_init(t):
    return (jnp.full((t, 1), MASKED, F32), jnp.zeros((t, 2 * LANES), F32))


def _softmax_finish(state):
    _, acc = state
    return acc[:, :LANES] / acc[:, LANES:]


def _mla_attn_kernel(q_ref, kn_ref, kr_ref, v_ref, o_ref, sa_ref, sb_ref, *, t):
    qi = pl.program_id(2)
    q = q_ref[...]

    def scores(ki):
        off = pl.multiple_of(ki * t, t)
        k = jnp.concatenate([kn_ref[pl.ds(off, t), :], kr_ref[pl.ds(off, t), :]], axis=1)
        return lax.dot_general(q, k, _NT, preferred_element_type=F32)

    def update(ki, s_ref, state, diagonal):
        off = pl.multiple_of(ki * t, t)
        if diagonal:
            return _softmax_update_diag(s_ref, state, v_ref, off, t)
        return _softmax_update(s_ref[...], state, v_ref[pl.ds(off, t), :])

    state = _causal_sweep(qi, sa_ref, sb_ref, scores, update, _softmax_init(t))
    o_ref[...] = _softmax_finish(state).astype(o_ref.dtype)


def _mla_attn(q, kv, kr, B, S, *, t=512):
    nq = S // t
    H = N_HEADS
    return pl.pallas_call(
        functools.partial(_mla_attn_kernel, t=t),
        grid=(B, H, nq),
        in_specs=[pl.BlockSpec((t, 2 * LANES), lambda b, h, i: (b * nq + i, h)),
                  pl.BlockSpec((S, LANES), lambda b, h, i: (b, h)),
                  pl.BlockSpec((S, LANES), lambda b, h, i: (b, 0)),
                  pl.BlockSpec((S, LANES), lambda b, h, i: (b, H + h))],
        out_specs=pl.BlockSpec((t, LANES), lambda b, h, i: (b * nq + i, h)),
        out_shape=jax.ShapeDtypeStruct((B * S, H * MLA_V), BF16),
        scratch_shapes=[pltpu.VMEM((t, t), F32), pltpu.VMEM((t, t), F32)],
        compiler_params=_params("parallel", "parallel", "arbitrary"),
        name="mla_attn",
    )(q, kv, kr, kv)


def _rope_tables(S):
    half = MLA_ROPE // 2
    inv = 1.0 / (ROPE_THETA ** (jnp.arange(0, MLA_ROPE, 2, dtype=F32) / MLA_ROPE))
    ang = jnp.arange(S, dtype=F32)[:, None] * inv[None, :]
    cos, sin = jnp.cos(ang), jnp.sin(ang)
    zeros = jnp.zeros((S, LANES - 2 * half), F32)
    return (jnp.concatenate([cos, cos, zeros], axis=1), jnp.concatenate([-sin, sin, zeros], axis=1))


def _mla_weights(w_in, w_uq, w_ukv):
    D = w_in.shape[0]
    half = MLA_ROPE // 2
    base = MLA_Q_LORA + MLA_KV_LORA
    x1, x2 = w_in[:, base:base + half], w_in[:, base + half:base + 2 * half]
    pad = jnp.zeros((D, LANES - 2 * half), w_in.dtype)
    w_in_ext = jnp.concatenate([w_in[:, :base], x1, x2, pad, x2, x1, pad], axis=1).astype(BF16)

    wq = w_uq.reshape(MLA_Q_LORA, N_HEADS, MLA_NOPE + MLA_ROPE)
    nope, r1, r2 = wq[..., :MLA_NOPE], wq[..., MLA_NOPE:MLA_NOPE + half], wq[..., MLA_NOPE + half:]
    padq = jnp.zeros((MLA_Q_LORA, N_HEADS, LANES - 2 * half), w_uq.dtype)
    w_main = jnp.concatenate([nope, r1, r2, padq], axis=-1).reshape(MLA_Q_LORA, N_HEADS * 2 * LANES).astype(BF16)
    w_sw = jnp.concatenate([r2, r1, padq], axis=-1).reshape(MLA_Q_LORA, N_HEADS * LANES).astype(BF16)

    wkv = w_ukv.reshape(MLA_KV_LORA, N_HEADS, MLA_NOPE + MLA_V)
    w_kv = jnp.concatenate([wkv[..., :MLA_NOPE].reshape(MLA_KV_LORA, -1),
                            wkv[..., MLA_NOPE:].reshape(MLA_KV_LORA, -1)], axis=1).astype(BF16)
    return w_in_ext, w_main, w_sw, w_kv


def _mla_mixer(h, B, S, w_in, q_norm, kv_norm, w_uq, w_ukv):
    w_in_ext, w_main, w_sw, w_kv = _mla_weights(w_in, w_uq, w_ukv)
    ct, st = _rope_tables(S)
    cq, ckv, kr = _mla_in(h, w_in_ext, q_norm, kv_norm, ct, st, S)
    scale = float((MLA_NOPE + MLA_ROPE) ** -0.5) * LOG2E
    q = _mla_uq(cq, w_main, w_sw, ct, st, S, scale)
    kv = _proj(ckv, w_kv, BF16)
    return _mla_attn(q, kv, kr, B, S)


def _hgrn_kernel(q_ref, f_ref, i_ref, g_ref, lbl_ref, on_ref, o_ref, state_ref, b_scr, k_scr,
                 *, layer, tile, chunk):
    C, SB = chunk, HGRN_SUB
    nb = C // SB

    @pl.when(pl.program_id(2) == 0)
    def _():
        state_ref[...] = jnp.zeros_like(state_ref)

    lg = lbl_ref[...]
    e = jnp.exp(lg - jnp.max(lg, axis=0, keepdims=True))
    p = e / jnp.sum(e, axis=0, keepdims=True)
    cs = p[0:1]
    for r in range(1, layer + 1):
        cs = cs + p[r:r + 1]
    lb = cs - p[0:1]
    log_lb = jnp.log(lb)
    log1m_lb = jnp.log1p(-lb)
    one_m_lb = 1.0 - lb
    onorm = on_ref[...]

    row = lax.broadcasted_iota(jnp.int32, (C, 3 * C), 0)
    col = lax.broadcasted_iota(jnp.int32, (C, 3 * C), 1) % C
    blk0 = (row // SB) * SB
    tri = jnp.concatenate([col <= row, col <= blk0, col <= jnp.minimum(blk0 + SB, C - 1)],
                          axis=0).astype(BF16)
    brow = lax.broadcasted_iota(jnp.int32, (C, C), 0) // SB
    bcol = lax.broadcasted_iota(jnp.int32, (C, C), 1) // SB
    bdiff = brow - bcol
    sub = lax.broadcasted_iota(jnp.int32, (SB, LANES), 0)
    causal_cap = [jnp.where(sub >= s, 0.0, -jnp.inf).astype(F32) for s in range(SB)]

    def chunk_step(c, slot):
        off = pl.multiple_of(c * C, C)
        q = q_ref[pl.ds(off, C), :]
        fp = f_ref[pl.ds(off, C), :]
        v = i_ref[pl.ds(off, C), :]
        g = g_ref[pl.ds(off, C), :]

        ls = jnp.minimum(fp, 0.0) - jnp.log(1.0 + jnp.exp(-jnp.abs(fp)))
        cc = log1m_lb + ls
        lf = jnp.maximum(log_lb, cc) + jnp.log(1.0 + jnp.exp(-jnp.abs(log_lb - cc)))
        kk = one_m_lb * jax.nn.sigmoid(-fp)

        p1 = lf.astype(BF16)
        r1 = lf - p1.astype(F32)
        p2 = r1.astype(BF16)
        p3 = (r1 - p2.astype(F32)).astype(BF16)
        cums = jnp.dot(tri, jnp.concatenate([p1, p2, p3], axis=0), preferred_element_type=F32)
        b, r, r_next = cums[:C], cums[C:2 * C], cums[2 * C:]
        bend = b[C - 1:C, :]
        b_scr[slot] = b
        k_scr[slot] = kk

        st_t = state_ref[...]
        qe = q * jnp.exp(b)
        o = lax.dot_general(qe.astype(BF16), st_t.astype(BF16), _NT, preferred_element_type=F32)
        kd = kk * jnp.exp(bend - b)
        state_ref[...] = st_t * jnp.exp(bend) + jnp.dot(v.T.astype(BF16), kd.astype(BF16),
                                                       preferred_element_type=F32)

        kt = kk * jnp.exp(r_next - b)
        gdec = jnp.exp(r_next - r)
        ql = q * jnp.exp(b - r)
        levels = [ql]
        for lvl in range(1, nb - 1):
            fac = jnp.concatenate([jnp.zeros((lvl * SB, LANES), F32), gdec[:C - lvl * SB]], axis=0)
            ql = ql * fac
            levels.append(ql)
        qs = jnp.concatenate(levels, axis=0).astype(BF16)
        rl = lax.dot_general(qs, kt.astype(BF16), _NT, preferred_element_type=F32)
        a = jnp.zeros((C, C), F32)
        for lvl in range(1, nb):
            a = a + jnp.where(bdiff == lvl, rl[(lvl - 1) * C:lvl * C, :], 0.0)
        o = o + jnp.dot(a.astype(BF16), v.astype(BF16), preferred_element_type=F32)

        diag = []
        for blk in range(nb):
            bq = b[blk * SB:(blk + 1) * SB]
            qq = q[blk * SB:(blk + 1) * SB]
            acc = jnp.zeros((SB, LANES), F32)
            for s in range(SB):
                rr = blk * SB + s
                dec = jnp.exp(jnp.minimum(bq - b_scr[slot, pl.ds(rr, 1), :], causal_cap[s]))
                a_ts = jnp.sum(qq * k_scr[slot, pl.ds(rr, 1), :] * dec, axis=-1, keepdims=True)
                acc = acc + a_ts * i_ref[pl.ds(off + rr, 1), :]
            diag.append(acc)
        o = o + jnp.concatenate(diag, axis=0)

        y = _rms_rows(o, onorm)
        o_ref[pl.ds(off, C), :] = (y * (g * jax.nn.sigmoid(g))).astype(o_ref.dtype)

    def chunk_group(grp, carry):
        for slot in range(HGRN_GROUP):
            chunk_step(HGRN_GROUP * grp + slot, slot)
        return carry

    lax.fori_loop(0, tile // (HGRN_GROUP * C), chunk_group, 0)


def _hgrn_mixer_core(proj, lb_logits, o_norm, B, S, layer, *, tile=512):
    nt = S // tile
    H = N_HEADS
    blk = lambda sec: pl.BlockSpec((tile, LANES), lambda b, h, t, sec=sec: (b * nt + t, sec * H + h))
    return pl.pallas_call(
        functools.partial(_hgrn_kernel, layer=layer, tile=tile, chunk=HGRN_CHUNK),
        grid=(B, H, nt),
        in_specs=[blk(0), blk(1), blk(2), blk(3),
                  pl.BlockSpec((DEPTH, LANES), lambda b, h, t: (0, h)),
                  pl.BlockSpec((1, LANES), lambda b, h, t: (0, h))],
        out_specs=pl.BlockSpec((tile, LANES), lambda b, h, t: (b * nt + t, h)),
        out_shape=jax.ShapeDtypeStruct((B * S, D_MODEL), BF16),
        scratch_shapes=[pltpu.VMEM((LANES, LANES), F32),
                        pltpu.VMEM((HGRN_GROUP, HGRN_CHUNK, LANES), F32),
                        pltpu.VMEM((HGRN_GROUP, HGRN_CHUNK, LANES), F32)],
        compiler_params=_params("parallel", "parallel", "arbitrary"),
        name="hgrn",
    )(proj, proj, proj, proj, lb_logits, o_norm.reshape(1, -1))


SB_CUMSUM_BLOCK = 256


def _sb_kernel(q_ref, k_ref, v_ref, o_ref, sa_ref, sb_ref, *, t):
    cb = SB_CUMSUM_BLOCK
    qi = pl.program_id(2)
    q = q_ref[...]
    row = lax.broadcasted_iota(jnp.int32, (t, t), 0)
    col = lax.broadcasted_iota(jnp.int32, (t, t), 1)
    strict = col < row
    jj = lax.broadcasted_iota(jnp.int32, (2 * cb, cb), 0) % cb
    ss = lax.broadcasted_iota(jnp.int32, (2 * cb, cb), 1)
    from_s2 = (jj >= ss).astype(BF16)

    def scores(ki):
        off = pl.multiple_of(ki * t, t)
        return lax.dot_general(q, k_ref[pl.ds(off, t), :], _NT, preferred_element_type=F32)

    def update(ki, zn_ref, acc, diagonal):
        off = pl.multiple_of(ki * t, t)
        tail = None
        new = None
        for j in reversed(range(t // cb)):
            cols = slice(j * cb, (j + 1) * cb)
            zn = zn_ref[:, cols]
            l1m = jnp.minimum(zn, 0.0) - jnp.log(1.0 + jnp.exp2(jnp.abs(zn) * (-LOG2E)))
            if diagonal:
                l1m = jnp.where(strict[:, cols], l1m, 0.0)
            hi = l1m.astype(BF16)
            lo = (l1m - hi.astype(F32)).astype(BF16)
            x = jnp.dot(jnp.concatenate([hi, lo], axis=1), from_s2, preferred_element_type=F32)
            if tail is not None:
                x = x + tail
            tail = x[:, 0:1]
            w = jnp.exp(x - zn)
            if diagonal:
                w = jnp.where(strict[:, cols], w, 0.0)
            part = jnp.dot(w.astype(BF16), v_ref[pl.ds(off + j * cb, cb), :], preferred_element_type=F32)
            new = part if new is None else new + part
        return acc * jnp.exp(tail) + new

    acc = _causal_sweep(qi, sa_ref, sb_ref, scores, update, jnp.zeros((t, HEAD_DIM), F32))
    o_ref[...] = acc.astype(o_ref.dtype)


def _sb_attn(qkv, B, S, *, t=512):
    nq = S // t
    H = N_HEADS
    return pl.pallas_call(
        functools.partial(_sb_kernel, t=t),
        grid=(B, H, nq),
        in_specs=[pl.BlockSpec((t, LANES), lambda b, h, i: (b * nq + i, h)),
                  pl.BlockSpec((S, LANES), lambda b, h, i: (b, H + h)),
                  pl.BlockSpec((S, LANES), lambda b, h, i: (b, 2 * H + h))],
        out_specs=pl.BlockSpec((t, LANES), lambda b, h, i: (b * nq + i, h)),
        out_shape=jax.ShapeDtypeStruct((B * S, D_MODEL), BF16),
        scratch_shapes=[pltpu.VMEM((t, t), F32), pltpu.VMEM((t, t), F32)],
        compiler_params=_params("parallel", "parallel", "arbitrary"),
        name="sb_attn",
    )(qkv, qkv, qkv)


MOBA_VETO = 2.0 ** 100


def _moba_kernel(q_ref, k_ref, v_ref, o_ref, kmean_ref, sa_ref, sb_ref, *, nblk, t):
    bpt = t // MOBA_BLOCK
    qi = pl.program_id(2)

    @pl.when(qi == 0)
    def _():
        kmean_ref[...] = jnp.mean(k_ref[...].astype(F32).reshape(nblk, MOBA_BLOCK, LANES), axis=1)

    q = q_ref[...]

    gate = lax.dot_general(kmean_ref[...], q.astype(F32), _NT, precision=lax.Precision.HIGHEST,
                           preferred_element_type=F32)
    blk = lax.broadcasted_iota(jnp.int32, (nblk, t), 0)
    own = qi * bpt + lax.broadcasted_iota(jnp.int32, (nblk, t), 1) // MOBA_BLOCK
    neg_inf = jnp.float32(-jnp.inf)
    gate = jnp.where(blk < own, gate, neg_inf)
    sel = (blk == own).astype(F32)
    for _ in range(MOBA_TOPK):
        mx = jnp.max(gate, axis=0, keepdims=True)
        first = jnp.min(jnp.where(gate == mx, blk, nblk), axis=0, keepdims=True)
        pick = jnp.logical_and(blk == first, mx > neg_inf)
        sel = jnp.where(pick, 1.0, sel)
        gate = jnp.where(pick, neg_inf, gate)

    veto = jnp.concatenate([sel - 1.0, jnp.zeros((LANES - nblk, t), F32)], axis=0)
    q_ext = jnp.concatenate([q, veto.T.astype(BF16)], axis=1)
    lane_blk = lax.broadcasted_iota(jnp.int32, (t, LANES), 1)
    key_blk = lax.broadcasted_iota(jnp.int32, (t, LANES), 0) // MOBA_BLOCK

    def scores(kc):
        off = pl.multiple_of(kc * t, t)
        hot = jnp.where(lane_blk == kc * bpt + key_blk, MOBA_VETO, 0.0).astype(BF16)
        k_ext = jnp.concatenate([k_ref[pl.ds(off, t), :], hot], axis=1)
        return lax.dot_general(q_ext, k_ext, _NT, preferred_element_type=F32)

    def update(kc, s_ref, state, diagonal):
        off = pl.multiple_of(kc * t, t)
        if diagonal:
            return _softmax_update_diag(s_ref, state, v_ref, off, t)
        return _softmax_update(s_ref[...], state, v_ref[pl.ds(off, t), :])

    state = _causal_sweep(qi, sa_ref, sb_ref, scores, update, _softmax_init(t))
    o_ref[...] = _softmax_finish(state).astype(o_ref.dtype)


def _moba_attn(qkv, B, S, *, t=2 * MOBA_BLOCK):
    assert t == 2 * MOBA_BLOCK
    nq = S // t
    H = N_HEADS
    return pl.pallas_call(
        functools.partial(_moba_kernel, nblk=S // MOBA_BLOCK, t=t),
        grid=(B, H, nq),
        in_specs=[pl.BlockSpec((t, LANES), lambda b, h, i: (b * nq + i, h)),
                  pl.BlockSpec((S, LANES), lambda b, h, i: (b, H + h)),
                  pl.BlockSpec((S, LANES), lambda b, h, i: (b, 2 * H + h))],
        out_specs=pl.BlockSpec((t, LANES), lambda b, h, i: (b * nq + i, h)),
        out_shape=jax.ShapeDtypeStruct((B * S, D_MODEL), BF16),
        scratch_shapes=[pltpu.VMEM((S // MOBA_BLOCK, LANES), F32), pltpu.VMEM((t, t), F32),
                        pltpu.VMEM((t, t), F32)],
        compiler_params=_params("parallel", "parallel", "arbitrary"),
        name="moba_attn",
    )(qkv, qkv, qkv)


def _qkv_proj(h, w_in, layer, q_scale):
    return _proj(h, w_in, BF16, layer=layer, scaled_cols=D_MODEL, scale=q_scale)


def _sb_mixer(h, B, S, w_in, layer=0):
    return _sb_attn(_qkv_proj(h, w_in, layer, -float(HEAD_DIM ** -0.5)), B, S)


def _moba_mixer(h, B, S, w_in, layer=0):
    return _moba_attn(_qkv_proj(h, w_in, layer, float(HEAD_DIM ** -0.5) * LOG2E), B, S)


def kernel(x, mla_w_in, mla_q_norm, mla_kv_norm, mla_w_uq, mla_w_ukv, mla_w_o, hgrn_w_in, hgrn_lb_logits, hgrn_o_norm, hgrn_w_o, sb_w_in, sb_w_o, moba_w_in, moba_w_o, ln_g, ln_b, mlp_w1, mlp_w2):
    B, S, D = x.shape
    assert D == D_MODEL and S % MOBA_BLOCK == 0 and S % 512 == 0
    h = x.reshape(B * S, D)
    hb = h
    mlp_w2_bf16 = mlp_w2.astype(BF16)
    n_mixers = 4
    for i in range(DEPTH):
        kind, slot = i % n_mixers, i // n_mixers
        if kind == 0:
            o = _mla_mixer(hb, B, S, mla_w_in[slot], mla_q_norm[slot], mla_kv_norm[slot],
                           mla_w_uq[slot], mla_w_ukv[slot])
            w_o = mla_w_o[slot]
        elif kind == 1:
            proj = _proj(hb, hgrn_w_in, F32, layer=slot)
            o = _hgrn_mixer_core(proj, hgrn_lb_logits, hgrn_o_norm[slot], B, S, i)
            w_o = hgrn_w_o[slot]
        elif kind == 2:
            o = _sb_mixer(hb, B, S, sb_w_in, slot)
            w_o = sb_w_o[slot]
        else:
            o = _moba_mixer(hb, B, S, moba_w_in, slot)
            w_o = moba_w_o[slot]
        h, hb = _proj_res_ln(o, w_o.astype(BF16), h, ln_g[i, 0], ln_b[i, 0])
        a = _proj(hb, mlp_w1, BF16, layer=i, act="relu2")
        h, hb = _proj_res_ln(a, mlp_w2_bf16, h, ln_g[i, 1], ln_b[i, 1], layer=i)
    return h.reshape(B, S, D)
```

```python
import functools

import jax
import jax.numpy as jnp
from jax import lax
from jax.experimental import pallas as pl
from jax.experimental.pallas import tpu as pltpu

F32 = jnp.float32
BF16 = jnp.bfloat16

D_MODEL = 2048
DEPTH = 4
N_HEADS = 16
HEAD_DIM = 128
MLA_Q_LORA = 512
MLA_KV_LORA = 512
MLA_NOPE = 128
MLA_ROPE = 64
MLA_V = 128
ROPE_THETA = 10000.0
HGRN_CHUNK = 64
HGRN_SUB = 8
HGRN_GROUP = 8
ATTN_HEADS_PER_STEP = 1
SOFTMAX_HEADS_PER_STEP = 2
MOBA_BLOCK = 256
MOBA_TOPK = 3
ALPHA = float((2 * DEPTH) ** 0.25)
LN_EPS = 1e-5
RMS_EPS = 1e-6

V7X_VMEM_BYTES = 64 * 1024 * 1024
VMEM_LIMIT = V7X_VMEM_BYTES - 8 * 1024 * 1024
LANES = 128
MASKED = -1e30
LOG2E = 1.4426950408889634

_NT = (((1,), (1,)), ((), ()))


def _params(*sem):
    return pltpu.CompilerParams(dimension_semantics=sem, vmem_limit_bytes=VMEM_LIMIT)


def _layer_norm_rows(y, g, b):
    mu = jnp.mean(y, axis=-1, keepdims=True)
    d = y - mu
    var = jnp.mean(d * d, axis=-1, keepdims=True)
    return d * lax.rsqrt(var + LN_EPS) * g + b


def _rms_rows(x, g):
    return x * lax.rsqrt(jnp.mean(x * x, axis=-1, keepdims=True) + RMS_EPS) * g


def _proj_kernel(x_ref, w_ref, o_ref, wbf_ref, *, act, scaled_tiles, scale):
    @pl.when(pl.program_id(1) == 0)
    def _():
        wbf_ref[...] = w_ref[...].astype(BF16)

    acc = jnp.dot(x_ref[...].astype(BF16), wbf_ref[...], preferred_element_type=F32)
    if act == "relu2":
        r = jnp.maximum(acc, 0.0)
        acc = r * r
    if scaled_tiles:
        acc = acc * jnp.where(pl.program_id(0) < scaled_tiles, scale, 1.0)
    o_ref[...] = acc.astype(o_ref.dtype)


def _proj(x, w, out_dtype, *, layer=0, tm=1024, tn=1024, act=None, scaled_cols=0, scale=1.0):
    M, K = x.shape
    N = w.shape[-1]
    tm, tn = min(tm, M), min(tn, N)
    assert scaled_cols % tn == 0 and M % tm == 0 and N % tn == 0
    if w.ndim == 3:
        w_spec = pl.BlockSpec((None, K, tn), lambda j, i: (layer, 0, j))
    else:
        w_spec = pl.BlockSpec((K, tn), lambda j, i: (0, j))
    return pl.pallas_call(
        functools.partial(_proj_kernel, act=act, scaled_tiles=scaled_cols // tn, scale=scale),
        grid=(N // tn, M // tm),
        in_specs=[pl.BlockSpec((tm, K), lambda j, i: (i, 0)), w_spec],
        out_specs=pl.BlockSpec((tm, tn), lambda j, i: (i, j)),
        out_shape=jax.ShapeDtypeStruct((M, N), out_dtype),
        scratch_shapes=[pltpu.VMEM((K, tn), BF16)],
        compiler_params=_params("parallel", "arbitrary"),
        name="proj",
    )(x, w)


def _proj_res_ln_kernel(x_ref, w_ref, h_ref, g_ref, b_ref, o_ref, ob_ref, acc_ref, *, nk):
    k = pl.program_id(1)
    def part():
        return jnp.dot(x_ref[...], w_ref[...], preferred_element_type=F32)

    def finish(y):
        out = _layer_norm_rows(ALPHA * h_ref[...] + y, g_ref[...], b_ref[...])
        o_ref[...] = out
        ob_ref[...] = out.astype(BF16)

    if nk == 1:
        finish(part())
    else:
        @pl.when(k == 0)
        def _():
            acc_ref[...] = part()

        @pl.when(jnp.logical_and(k > 0, k < nk - 1))
        def _():
            acc_ref[...] += part()

        @pl.when(k == nk - 1)
        def _():
            finish(acc_ref[...] + part())


def _proj_res_ln(x, w, h, g, b, *, layer=0, tm=512, tk=2048):
    M, K = x.shape
    N = w.shape[-1]
    nk = K // tk
    if w.ndim == 3:
        w_spec = pl.BlockSpec((None, tk, N), lambda i, k: (layer, k, 0))
    else:
        w_spec = pl.BlockSpec((tk, N), lambda i, k: (k, 0))
    return pl.pallas_call(
        functools.partial(_proj_res_ln_kernel, nk=nk),
        grid=(M // tm, nk),
        in_specs=[pl.BlockSpec((tm, tk), lambda i, k: (i, k)),
                  w_spec,
                  pl.BlockSpec((tm, N), lambda i, k: (i, 0)),
                  pl.BlockSpec((1, N), lambda i, k: (0, 0)),
                  pl.BlockSpec((1, N), lambda i, k: (0, 0))],
        out_specs=[pl.BlockSpec((tm, N), lambda i, k: (i, 0)),
                   pl.BlockSpec((tm, N), lambda i, k: (i, 0))],
        out_shape=[jax.ShapeDtypeStruct((M, N), F32), jax.ShapeDtypeStruct((M, N), BF16)],
        scratch_shapes=[pltpu.VMEM((tm, N), F32)],
        compiler_params=_params("parallel", "arbitrary"),
        name="proj_res_ln",
    )(x, w, h, g.reshape(1, N), b.reshape(1, N))


def _mla_in_kernel(h_ref, w_ref, qg_ref, kvg_ref, ct_ref, st_ref, cq_ref, ckv_ref, kr_ref):
    acc = jnp.dot(h_ref[...].astype(BF16), w_ref[...], preferred_element_type=F32)
    ql, kvl = MLA_Q_LORA, MLA_KV_LORA
    cq_ref[...] = _rms_rows(acc[:, :ql], qg_ref[...]).astype(cq_ref.dtype)
    ckv_ref[...] = _rms_rows(acc[:, ql:ql + kvl], kvg_ref[...]).astype(ckv_ref.dtype)
    a = acc[:, ql + kvl:ql + kvl + LANES]
    a_sw = acc[:, ql + kvl + LANES:]
    kr_ref[...] = (a * ct_ref[...] + a_sw * st_ref[...]).astype(kr_ref.dtype)


def _mla_in(h, w_ext, qg, kvg, ct, st, S, *, tm=512):
    M, K = h.shape
    N = w_ext.shape[1]
    ns = S // tm
    return pl.pallas_call(
        _mla_in_kernel,
        grid=(M // tm,),
        in_specs=[pl.BlockSpec((tm, K), lambda i: (i, 0)),
                  pl.BlockSpec((K, N), lambda i: (0, 0)),
                  pl.BlockSpec((1, MLA_Q_LORA), lambda i: (0, 0)),
                  pl.BlockSpec((1, MLA_KV_LORA), lambda i: (0, 0)),
                  pl.BlockSpec((tm, LANES), lambda i: (i % ns, 0)),
                  pl.BlockSpec((tm, LANES), lambda i: (i % ns, 0))],
        out_specs=[pl.BlockSpec((tm, MLA_Q_LORA), lambda i: (i, 0)),
                   pl.BlockSpec((tm, MLA_KV_LORA), lambda i: (i, 0)),
                   pl.BlockSpec((tm, LANES), lambda i: (i, 0))],
        out_shape=[jax.ShapeDtypeStruct((M, MLA_Q_LORA), BF16),
                   jax.ShapeDtypeStruct((M, MLA_KV_LORA), BF16),
                   jax.ShapeDtypeStruct((M, LANES), BF16)],
        compiler_params=_params("parallel"),
        name="mla_in",
    )(h, w_ext, qg.reshape(1, -1), kvg.reshape(1, -1), ct, st)


def _mla_uq_kernel(cq_ref, wm_ref, ws_ref, ct_ref, st_ref, q_ref, *, heads, scale):
    x = cq_ref[...]
    a = jnp.dot(x, wm_ref[...], preferred_element_type=F32)
    a_sw = jnp.dot(x, ws_ref[...], preferred_element_type=F32)
    ct = ct_ref[...]
    st = st_ref[...]
    for hh in range(heads):
        lo = hh * 2 * LANES
        q_ref[:, lo:lo + LANES] = (a[:, lo:lo + LANES] * scale).astype(q_ref.dtype)
        rot = a[:, lo + LANES:lo + 2 * LANES] * ct + a_sw[:, hh * LANES:(hh + 1) * LANES] * st
        q_ref[:, lo + LANES:lo + 2 * LANES] = (rot * scale).astype(q_ref.dtype)


def _mla_uq(cq, w_main, w_sw, ct, st, S, scale, *, tm=512, heads_per_step=4):
    M, K = cq.shape
    hp = heads_per_step
    ns = S // tm
    return pl.pallas_call(
        functools.partial(_mla_uq_kernel, heads=hp, scale=scale),
        grid=(M // tm, N_HEADS // hp),
        in_specs=[pl.BlockSpec((tm, K), lambda i, j: (i, 0)),
                  pl.BlockSpec((K, hp * 2 * LANES), lambda i, j: (0, j)),
                  pl.BlockSpec((K, hp * LANES), lambda i, j: (0, j)),
                  pl.BlockSpec((tm, LANES), lambda i, j: (i % ns, 0)),
                  pl.BlockSpec((tm, LANES), lambda i, j: (i % ns, 0))],
        out_specs=pl.BlockSpec((tm, hp * 2 * LANES), lambda i, j: (i, j)),
        out_shape=jax.ShapeDtypeStruct((M, N_HEADS * 2 * LANES), BF16),
        compiler_params=_params("parallel", "parallel"),
        name="mla_uq",
    )(cq, w_main, w_sw, ct, st)


def _causal_sweep(qi, streams):
    def fill(which, ki):
        for stream in streams:
            stream[which][...] = stream[2](ki)

    def step(which, ki, states, diagonal):
        return tuple(stream[3](ki, stream[which], st, diagonal) for stream, st in zip(streams, states))

    fill(0, 0)

    def pair(p, states):
        k0 = 2 * p
        fill(1, k0 + 1)
        states = step(0, k0, states, False)
        fill(0, k0 + 2)
        return step(1, k0 + 1, states, False)

    states = lax.fori_loop(0, qi // 2, pair, tuple(stream[4] for stream in streams))

    def odd(states):
        fill(1, qi)
        states = step(0, qi - 1, states, False)
        return step(1, qi, states, True)

    def even(states):
        return step(0, qi, states, True)

    return lax.cond(qi % 2 == 1, odd, even, states)


def _softmax_update(s, state, v):
    m, acc = state
    m_new = jnp.maximum(m, jnp.max(s, axis=-1, keepdims=True))
    alpha = jnp.exp2(m - m_new)
    p = jnp.exp2(s - m_new).astype(BF16)
    v_ones = jnp.concatenate([v, jnp.ones_like(v)], axis=1)
    acc = alpha * acc + jnp.dot(p, v_ones, preferred_element_type=F32)
    return m_new, acc


def _softmax_update_diag(s_ref, state, v_rows, t):
    h = t // 2
    m, acc = state
    keep = (lax.broadcasted_iota(jnp.int32, (h, h), 1) <= lax.broadcasted_iota(jnp.int32, (h, h), 0))
    s_top = jnp.where(keep, s_ref[0:h, 0:h], -jnp.inf)
    top = _softmax_update(s_top, (m[:h], acc[:h]), v_rows(h))
    s_bot = jnp.concatenate([s_ref[h:t, 0:h], jnp.where(keep, s_ref[h:t, h:t], -jnp.inf)], axis=1)
    bot = _softmax_update(s_bot, (m[h:], acc[h:]), v_rows(t))
    return jnp.concatenate([top[0], bot[0]], axis=0), jnp.concatenate([top[1], bot[1]], axis=0)


def _softmax_init(t):
    return (jnp.full((t, 1), MASKED, F32), jnp.zeros((t, 2 * LANES), F32))


def _softmax_finish(state):
    _, acc = state
    return acc[:, :LANES] / acc[:, LANES:]


def _mla_attn_kernel(q_ref, kn_ref, kr_ref, v_ref, o_ref, *score_refs, t, heads):
    qi = pl.program_id(2)

    def stream(hd):
        lanes = slice(hd * LANES, (hd + 1) * LANES)
        q = q_ref[:, 2 * hd * LANES:2 * (hd + 1) * LANES]

        def scores(ki):
            off = pl.multiple_of(ki * t, t)
            k = jnp.concatenate([kn_ref[pl.ds(off, t), lanes], kr_ref[pl.ds(off, t), :]], axis=1)
            return lax.dot_general(q, k, _NT, preferred_element_type=F32)

        def update(ki, s_ref, state, diagonal):
            off = pl.multiple_of(ki * t, t)
            v_rows = lambda n: v_ref[pl.ds(off, n), lanes]
            if diagonal:
                return _softmax_update_diag(s_ref, state, v_rows, t)
            return _softmax_update(s_ref[...], state, v_rows(t))

        return (score_refs[2 * hd], score_refs[2 * hd + 1], scores, update, _softmax_init(t))

    states = _causal_sweep(qi, [stream(hd) for hd in range(heads)])
    for hd, state in enumerate(states):
        o_ref[:, hd * LANES:(hd + 1) * LANES] = _softmax_finish(state).astype(o_ref.dtype)


def _mla_attn(q, kv, kr, B, S, *, t=512, heads=SOFTMAX_HEADS_PER_STEP):
    nq = S // t
    G = N_HEADS // heads
    w = heads * LANES
    return pl.pallas_call(
        functools.partial(_mla_attn_kernel, t=t, heads=heads),
        grid=(B, G, nq),
        in_specs=[pl.BlockSpec((t, 2 * w), lambda b, g, i: (b * nq + i, g)),
                  pl.BlockSpec((S, w), lambda b, g, i: (b, g)),
                  pl.BlockSpec((S, LANES), lambda b, g, i: (b, 0)),
                  pl.BlockSpec((S, w), lambda b, g, i: (b, G + g))],
        out_specs=pl.BlockSpec((t, w), lambda b, g, i: (b * nq + i, g)),
        out_shape=jax.ShapeDtypeStruct((B * S, N_HEADS * MLA_V), BF16),
        scratch_shapes=[pltpu.VMEM((t, t), F32) for _ in range(2 * heads)],
        compiler_params=_params("parallel", "parallel", "arbitrary"),
        name="mla_attn",
    )(q, kv, kr, kv)


def _rope_tables(S):
    half = MLA_ROPE // 2
    inv = 1.0 / (ROPE_THETA ** (jnp.arange(0, MLA_ROPE, 2, dtype=F32) / MLA_ROPE))
    ang = jnp.arange(S, dtype=F32)[:, None] * inv[None, :]
    cos, sin = jnp.cos(ang), jnp.sin(ang)
    zeros = jnp.zeros((S, LANES - 2 * half), F32)
    return (jnp.concatenate([cos, cos, zeros], axis=1), jnp.concatenate([-sin, sin, zeros], axis=1))


def _mla_weights(w_in, w_uq, w_ukv):
    D = w_in.shape[0]
    half = MLA_ROPE // 2
    base = MLA_Q_LORA + MLA_KV_LORA
    x1, x2 = w_in[:, base:base + half], w_in[:, base + half:base + 2 * half]
    pad = jnp.zeros((D, LANES - 2 * half), w_in.dtype)
    w_in_ext = jnp.concatenate([w_in[:, :base], x1, x2, pad, x2, x1, pad], axis=1).astype(BF16)

    wq = w_uq.reshape(MLA_Q_LORA, N_HEADS, MLA_NOPE + MLA_ROPE)
    nope, r1, r2 = wq[..., :MLA_NOPE], wq[..., MLA_NOPE:MLA_NOPE + half], wq[..., MLA_NOPE + half:]
    padq = jnp.zeros((MLA_Q_LORA, N_HEADS, LANES - 2 * half), w_uq.dtype)
    w_main = jnp.concatenate([nope, r1, r2, padq], axis=-1).reshape(MLA_Q_LORA, N_HEADS * 2 * LANES).astype(BF16)
    w_sw = jnp.concatenate([r2, r1, padq], axis=-1).reshape(MLA_Q_LORA, N_HEADS * LANES).astype(BF16)

    wkv = w_ukv.reshape(MLA_KV_LORA, N_HEADS, MLA_NOPE + MLA_V)
    w_kv = jnp.concatenate([wkv[..., :MLA_NOPE].reshape(MLA_KV_LORA, -1),
                            wkv[..., MLA_NOPE:].reshape(MLA_KV_LORA, -1)], axis=1).astype(BF16)
    return w_in_ext, w_main, w_sw, w_kv


def _mla_mixer(h, B, S, w_in, q_norm, kv_norm, w_uq, w_ukv):
    w_in_ext, w_main, w_sw, w_kv = _mla_weights(w_in, w_uq, w_ukv)
    ct, st = _rope_tables(S)
    cq, ckv, kr = _mla_in(h, w_in_ext, q_norm, kv_norm, ct, st, S)
    scale = float((MLA_NOPE + MLA_ROPE) ** -0.5) * LOG2E
    q = _mla_uq(cq, w_main, w_sw, ct, st, S, scale)
    kv = _proj(ckv, w_kv, BF16)
    return _mla_attn(q, kv, kr, B, S)


def _hgrn_kernel(q_ref, f_ref, i_ref, g_ref, lbl_ref, on_ref, o_ref, state_ref, b_scr, k_scr,
                 *, layer, tile, chunk):
    C, SB = chunk, HGRN_SUB
    nb = C // SB

    @pl.when(pl.program_id(2) == 0)
    def _():
        state_ref[...] = jnp.zeros_like(state_ref)

    lg = lbl_ref[...]
    e = jnp.exp(lg - jnp.max(lg, axis=0, keepdims=True))
    p = e / jnp.sum(e, axis=0, keepdims=True)
    cs = p[0:1]
    for r in range(1, layer + 1):
        cs = cs + p[r:r + 1]
    lb = cs - p[0:1]
    log_lb = jnp.log(lb)
    log1m_lb = jnp.log1p(-lb)
    one_m_lb = 1.0 - lb
    onorm = on_ref[...]

    row = lax.broadcasted_iota(jnp.int32, (C, 3 * C), 0)
    col = lax.broadcasted_iota(jnp.int32, (C, 3 * C), 1) % C
    blk0 = (row // SB) * SB
    tri = jnp.concatenate([col <= row, col <= blk0, col <= jnp.minimum(blk0 + SB, C - 1)],
                          axis=0).astype(BF16)
    brow = lax.broadcasted_iota(jnp.int32, (C, C), 0) // SB
    bcol = lax.broadcasted_iota(jnp.int32, (C, C), 1) // SB
    bdiff = brow - bcol
    sub = lax.broadcasted_iota(jnp.int32, (SB, LANES), 0)
    causal_cap = [jnp.where(sub >= s, 0.0, -jnp.inf).astype(F32) for s in range(SB)]

    def chunk_step(c, slot):
        off = pl.multiple_of(c * C, C)
        q = q_ref[pl.ds(off, C), :]
        fp = f_ref[pl.ds(off, C), :]
        v = i_ref[pl.ds(off, C), :]
        g = g_ref[pl.ds(off, C), :]

        ls = jnp.minimum(fp, 0.0) - jnp.log(1.0 + jnp.exp(-jnp.abs(fp)))
        cc = log1m_lb + ls
        lf = jnp.maximum(log_lb, cc) + jnp.log(1.0 + jnp.exp(-jnp.abs(log_lb - cc)))
        kk = one_m_lb * jax.nn.sigmoid(-fp)

        p1 = lf.astype(BF16)
        r1 = lf - p1.astype(F32)
        p2 = r1.astype(BF16)
        p3 = (r1 - p2.astype(F32)).astype(BF16)
        cums = jnp.dot(tri, jnp.concatenate([p1, p2, p3], axis=0), preferred_element_type=F32)
        b, r, r_next = cums[:C], cums[C:2 * C], cums[2 * C:]
        bend = b[C - 1:C, :]
        b_scr[slot] = b
        k_scr[slot] = kk

        st_t = state_ref[...]
        qe = q * jnp.exp(b)
        o = lax.dot_general(qe.astype(BF16), st_t.astype(BF16), _NT, preferred_element_type=F32)
        kd = kk * jnp.exp(bend - b)
        state_ref[...] = st_t * jnp.exp(bend) + jnp.dot(v.T.astype(BF16), kd.astype(BF16),
                                                       preferred_element_type=F32)

        kt = kk * jnp.exp(r_next - b)
        gdec = jnp.exp(r_next - r)
        ql = q * jnp.exp(b - r)
        levels = [ql]
        for lvl in range(1, nb - 1):
            fac = jnp.concatenate([jnp.zeros((lvl * SB, LANES), F32), gdec[:C - lvl * SB]], axis=0)
            ql = ql * fac
            levels.append(ql)
        qs = jnp.concatenate(levels, axis=0).astype(BF16)
        rl = lax.dot_general(qs, kt.astype(BF16), _NT, preferred_element_type=F32)
        a = jnp.zeros((C, C), F32)
        for lvl in range(1, nb):
            a = a + jnp.where(bdiff == lvl, rl[(lvl - 1) * C:lvl * C, :], 0.0)
        o = o + jnp.dot(a.astype(BF16), v.astype(BF16), preferred_element_type=F32)

        diag = []
        for blk in range(nb):
            bq = b[blk * SB:(blk + 1) * SB]
            qq = q[blk * SB:(blk + 1) * SB]
            acc = jnp.zeros((SB, LANES), F32)
            for s in range(SB):
                rr = blk * SB + s
                dec = jnp.exp(jnp.minimum(bq - b_scr[slot, pl.ds(rr, 1), :], causal_cap[s]))
                a_ts = jnp.sum(qq * k_scr[slot, pl.ds(rr, 1), :] * dec, axis=-1, keepdims=True)
                acc = acc + a_ts * i_ref[pl.ds(off + rr, 1), :]
            diag.append(acc)
        o = o + jnp.concatenate(diag, axis=0)

        y = _rms_rows(o, onorm)
        o_ref[pl.ds(off, C), :] = (y * (g * jax.nn.sigmoid(g))).astype(o_ref.dtype)

    def chunk_group(grp, carry):
        for slot in range(HGRN_GROUP):
            chunk_step(HGRN_GROUP * grp + slot, slot)
        return carry

    lax.fori_loop(0, tile // (HGRN_GROUP * C), chunk_group, 0)


def _hgrn_mixer_core(proj, lb_logits, o_norm, B, S, layer, *, tile=512):
    nt = S // tile
    H = N_HEADS
    blk = lambda sec: pl.BlockSpec((tile, LANES), lambda b, h, t, sec=sec: (b * nt + t, sec * H + h))
    return pl.pallas_call(
        functools.partial(_hgrn_kernel, layer=layer, tile=tile, chunk=HGRN_CHUNK),
        grid=(B, H, nt),
        in_specs=[blk(0), blk(1), blk(2), blk(3),
                  pl.BlockSpec((DEPTH, LANES), lambda b, h, t: (0, h)),
                  pl.BlockSpec((1, LANES), lambda b, h, t: (0, h))],
        out_specs=pl.BlockSpec((tile, LANES), lambda b, h, t: (b * nt + t, h)),
        out_shape=jax.ShapeDtypeStruct((B * S, D_MODEL), BF16),
        scratch_shapes=[pltpu.VMEM((LANES, LANES), F32),
                        pltpu.VMEM((HGRN_GROUP, HGRN_CHUNK, LANES), F32),
                        pltpu.VMEM((HGRN_GROUP, HGRN_CHUNK, LANES), F32)],
        compiler_params=_params("parallel", "parallel", "arbitrary"),
        name="hgrn",
    )(proj, proj, proj, proj, lb_logits, o_norm.reshape(1, -1))


SB_CUMSUM_BLOCK = 256


def _sb_kernel(q_ref, k_ref, v_ref, o_ref, *score_refs, t, heads):
    cb = SB_CUMSUM_BLOCK
    qi = pl.program_id(2)
    row = lax.broadcasted_iota(jnp.int32, (t, t), 0)
    col = lax.broadcasted_iota(jnp.int32, (t, t), 1)
    strict = col < row
    jj = lax.broadcasted_iota(jnp.int32, (2 * cb, cb), 0) % cb
    ss = lax.broadcasted_iota(jnp.int32, (2 * cb, cb), 1)
    from_s2 = (jj >= ss).astype(BF16)

    def stream(hd):
        lanes = slice(hd * LANES, (hd + 1) * LANES)
        q = q_ref[:, lanes]

        def scores(ki):
            off = pl.multiple_of(ki * t, t)
            return lax.dot_general(q, k_ref[pl.ds(off, t), lanes], _NT, preferred_element_type=F32)

        def update(ki, zn_ref, acc, diagonal):
            off = pl.multiple_of(ki * t, t)
            tail = None
            new = None
            for j in reversed(range(t // cb)):
                cols = slice(j * cb, (j + 1) * cb)
                zn = zn_ref[:, cols]
                l1m = jnp.minimum(zn, 0.0) - jnp.log(1.0 + jnp.exp2(jnp.abs(zn) * (-LOG2E)))
                if diagonal:
                    l1m = jnp.where(strict[:, cols], l1m, 0.0)
                hi = l1m.astype(BF16)
                lo = (l1m - hi.astype(F32)).astype(BF16)
                x = jnp.dot(jnp.concatenate([hi, lo], axis=1), from_s2, preferred_element_type=F32)
                if tail is not None:
                    x = x + tail
                tail = x[:, 0:1]
                w = jnp.exp(x - zn)
                if diagonal:
                    w = jnp.where(strict[:, cols], w, 0.0)
                part = jnp.dot(w.astype(BF16), v_ref[pl.ds(off + j * cb, cb), lanes],
                               preferred_element_type=F32)
                new = part if new is None else new + part
            return acc * jnp.exp(tail) + new

        return (score_refs[2 * hd], score_refs[2 * hd + 1], scores, update, jnp.zeros((t, HEAD_DIM), F32))

    accs = _causal_sweep(qi, [stream(hd) for hd in range(heads)])
    for hd, acc in enumerate(accs):
        o_ref[:, hd * LANES:(hd + 1) * LANES] = acc.astype(o_ref.dtype)


def _sb_attn(qkv, B, S, *, t=512, heads=ATTN_HEADS_PER_STEP):
    nq = S // t
    G = N_HEADS // heads
    w = heads * LANES
    return pl.pallas_call(
        functools.partial(_sb_kernel, t=t, heads=heads),
        grid=(B, G, nq),
        in_specs=[pl.BlockSpec((t, w), lambda b, g, i: (b * nq + i, g)),
                  pl.BlockSpec((S, w), lambda b, g, i: (b, G + g)),
                  pl.BlockSpec((S, w), lambda b, g, i: (b, 2 * G + g))],
        out_specs=pl.BlockSpec((t, w), lambda b, g, i: (b * nq + i, g)),
        out_shape=jax.ShapeDtypeStruct((B * S, D_MODEL), BF16),
        scratch_shapes=[pltpu.VMEM((t, t), F32) for _ in range(2 * heads)],
        compiler_params=_params("parallel", "parallel", "arbitrary"),
        name="sb_attn",
    )(qkv, qkv, qkv)


MOBA_VETO = 2.0 ** 100


def _moba_kernel(q_ref, k_ref, v_ref, o_ref, kmean_ref, *score_refs, nblk, t, heads):
    bpt = t // MOBA_BLOCK
    qi = pl.program_id(2)

    @pl.when(qi == 0)
    def _():
        for hd in range(heads):
            k_all = k_ref[:, hd * LANES:(hd + 1) * LANES].astype(F32)
            kmean_ref[hd] = jnp.mean(k_all.reshape(nblk, MOBA_BLOCK, LANES), axis=1)

    blk = lax.broadcasted_iota(jnp.int32, (nblk, t), 0)
    own = qi * bpt + lax.broadcasted_iota(jnp.int32, (nblk, t), 1) // MOBA_BLOCK
    neg_inf = jnp.float32(-jnp.inf)
    lane_blk = lax.broadcasted_iota(jnp.int32, (t, LANES), 1)
    key_blk = lax.broadcasted_iota(jnp.int32, (t, LANES), 0) // MOBA_BLOCK

    def stream(hd):
        lanes = slice(hd * LANES, (hd + 1) * LANES)
        q = q_ref[:, lanes]

        gate = lax.dot_general(kmean_ref[hd], q.astype(F32), _NT, precision=lax.Precision.HIGHEST,
                               preferred_element_type=F32)
        gate = jnp.where(blk < own, gate, neg_inf)
        sel = (blk == own).astype(F32)
        for _ in range(MOBA_TOPK):
            mx = jnp.max(gate, axis=0, keepdims=True)
            first = jnp.min(jnp.where(gate == mx, blk, nblk), axis=0, keepdims=True)
            pick = jnp.logical_and(blk == first, mx > neg_inf)
            sel = jnp.where(pick, 1.0, sel)
            gate = jnp.where(pick, neg_inf, gate)

        veto = jnp.concatenate([sel - 1.0, jnp.zeros((LANES - nblk, t), F32)], axis=0)
        q_ext = jnp.concatenate([q, veto.T.astype(BF16)], axis=1)

        def scores(kc):
            off = pl.multiple_of(kc * t, t)
            hot = jnp.where(lane_blk == kc * bpt + key_blk, MOBA_VETO, 0.0).astype(BF16)
            k_ext = jnp.concatenate([k_ref[pl.ds(off, t), lanes], hot], axis=1)
            return lax.dot_general(q_ext, k_ext, _NT, preferred_element_type=F32)

        def update(kc, s_ref, state, diagonal):
            off = pl.multiple_of(kc * t, t)
            v_rows = lambda n: v_ref[pl.ds(off, n), lanes]
            if diagonal:
                return _softmax_update_diag(s_ref, state, v_rows, t)
            return _softmax_update(s_ref[...], state, v_rows(t))

        return (score_refs[2 * hd], score_refs[2 * hd + 1], scores, update, _softmax_init(t))

    states = _causal_sweep(qi, [stream(hd) for hd in range(heads)])
    for hd, state in enumerate(states):
        o_ref[:, hd * LANES:(hd + 1) * LANES] = _softmax_finish(state).astype(o_ref.dtype)


def _moba_attn(qkv, B, S, *, t=2 * MOBA_BLOCK, heads=SOFTMAX_HEADS_PER_STEP):
    assert t == 2 * MOBA_BLOCK
    nq = S // t
    nblk = S // MOBA_BLOCK
    G = N_HEADS // heads
    w = heads * LANES
    return pl.pallas_call(
        functools.partial(_moba_kernel, nblk=nblk, t=t, heads=heads),
        grid=(B, G, nq),
        in_specs=[pl.BlockSpec((t, w), lambda b, g, i: (b * nq + i, g)),
                  pl.BlockSpec((S, w), lambda b, g, i: (b, G + g)),
                  pl.BlockSpec((S, w), lambda b, g, i: (b, 2 * G + g))],
        out_specs=pl.BlockSpec((t, w), lambda b, g, i: (b * nq + i, g)),
        out_shape=jax.ShapeDtypeStruct((B * S, D_MODEL), BF16),
        scratch_shapes=[pltpu.VMEM((heads, nblk, LANES), F32)]
                       + [pltpu.VMEM((t, t), F32) for _ in range(2 * heads)],
        compiler_params=_params("parallel", "parallel", "arbitrary"),
        name="moba_attn",
    )(qkv, qkv, qkv)


def _qkv_proj(h, w_in, layer, q_scale):
    return _proj(h, w_in, BF16, layer=layer, scaled_cols=D_MODEL, scale=q_scale)


def _sb_mixer(h, B, S, w_in, layer=0):
    return _sb_attn(_qkv_proj(h, w_in, layer, -float(HEAD_DIM ** -0.5)), B, S)


def _moba_mixer(h, B, S, w_in, layer=0):
    return _moba_attn(_qkv_proj(h, w_in, layer, float(HEAD_DIM ** -0.5) * LOG2E), B, S)


def kernel(x, mla_w_in, mla_q_norm, mla_kv_norm, mla_w_uq, mla_w_ukv, mla_w_o, hgrn_w_in, hgrn_lb_logits, hgrn_o_norm, hgrn_w_o, sb_w_in, sb_w_o, moba_w_in, moba_w_o, ln_g, ln_b, mlp_w1, mlp_w2):
    B, S, D = x.shape
    assert D == D_MODEL and S % MOBA_BLOCK == 0 and S % 512 == 0
    h = x.reshape(B * S, D)
    hb = h
    mlp_w2_bf16 = mlp_w2.astype(BF16)
    n_mixers = 4
    for i in range(DEPTH):
        kind, slot = i % n_mixers, i // n_mixers
        if kind == 0:
            o = _mla_mixer(hb, B, S, mla_w_in[slot], mla_q_norm[slot], mla_kv_norm[slot],
                           mla_w_uq[slot], mla_w_ukv[slot])
            w_o = mla_w_o[slot]
        elif kind == 1:
            proj = _proj(hb, hgrn_w_in, F32, layer=slot)
            o = _hgrn_mixer_core(proj, hgrn_lb_logits, hgrn_o_norm[slot], B, S, i)
            w_o = hgrn_w_o[slot]
        elif kind == 2:
            o = _sb_mixer(hb, B, S, sb_w_in, slot)
            w_o = sb_w_o[slot]
        else:
            o = _moba_mixer(hb, B, S, moba_w_in, slot)
            w_o = moba_w_o[slot]
        h, hb = _proj_res_ln(o, w_o.astype(BF16), h, ln_g[i, 0], ln_b[i, 0])
        a = _proj(hb, mlp_w1, BF16, layer=i, act="relu2")
        h, hb = _proj_res_ln(a, mlp_w2_bf16, h, ln_g[i, 1], ln_b[i, 1], layer=i)
    return h.reshape(B, S, D)
```

```python
import functools

import jax
import jax.numpy as jnp
from jax import lax
from jax.experimental import pallas as pl
from jax.experimental.pallas import tpu as pltpu

F32 = jnp.float32
BF16 = jnp.bfloat16

D_MODEL = 2048
DEPTH = 4
N_HEADS = 16
HEAD_DIM = 128
MLA_Q_LORA = 512
MLA_KV_LORA = 512
MLA_NOPE = 128
MLA_ROPE = 64
MLA_V = 128
ROPE_THETA = 10000.0
HGRN_CHUNK = 64
HGRN_SUB = 8
HGRN_GROUP = 8
ATTN_HEADS_PER_STEP = 2
SOFTMAX_HEADS_PER_STEP = 4
MOBA_BLOCK = 256
MOBA_TOPK = 3
ALPHA = float((2 * DEPTH) ** 0.25)
LN_EPS = 1e-5
RMS_EPS = 1e-6

V7X_VMEM_BYTES = 64 * 1024 * 1024
VMEM_LIMIT = V7X_VMEM_BYTES - 8 * 1024 * 1024
LANES = 128
MASKED = -1e30
LOG2E = 1.4426950408889634

_NT = (((1,), (1,)), ((), ()))


def _params(*sem):
    return pltpu.CompilerParams(dimension_semantics=sem, vmem_limit_bytes=VMEM_LIMIT)


def _layer_norm_rows(y, g, b):
    mu = jnp.mean(y, axis=-1, keepdims=True)
    d = y - mu
    var = jnp.mean(d * d, axis=-1, keepdims=True)
    return d * lax.rsqrt(var + LN_EPS) * g + b


def _rms_rows(x, g):
    return x * lax.rsqrt(jnp.mean(x * x, axis=-1, keepdims=True) + RMS_EPS) * g


def _proj_kernel(x_ref, w_ref, o_ref, wbf_ref, *, act, scaled_tiles, scale):
    @pl.when(pl.program_id(1) == 0)
    def _():
        wbf_ref[...] = w_ref[...].astype(BF16)

    acc = jnp.dot(x_ref[...].astype(BF16), wbf_ref[...], preferred_element_type=F32)
    if act == "relu2":
        r = jnp.maximum(acc, 0.0)
        acc = r * r
    if scaled_tiles:
        acc = acc * jnp.where(pl.program_id(0) < scaled_tiles, scale, 1.0)
    o_ref[...] = acc.astype(o_ref.dtype)


def _proj(x, w, out_dtype, *, layer=0, tm=1024, tn=1024, act=None, scaled_cols=0, scale=1.0):
    M, K = x.shape
    N = w.shape[-1]
    tm, tn = min(tm, M), min(tn, N)
    assert scaled_cols % tn == 0 and M % tm == 0 and N % tn == 0
    if w.ndim == 3:
        w_spec = pl.BlockSpec((None, K, tn), lambda j, i: (layer, 0, j))
    else:
        w_spec = pl.BlockSpec((K, tn), lambda j, i: (0, j))
    return pl.pallas_call(
        functools.partial(_proj_kernel, act=act, scaled_tiles=scaled_cols // tn, scale=scale),
        grid=(N // tn, M // tm),
        in_specs=[pl.BlockSpec((tm, K), lambda j, i: (i, 0)), w_spec],
        out_specs=pl.BlockSpec((tm, tn), lambda j, i: (i, j)),
        out_shape=jax.ShapeDtypeStruct((M, N), out_dtype),
        scratch_shapes=[pltpu.VMEM((K, tn), BF16)],
        compiler_params=_params("parallel", "arbitrary"),
        name="proj",
    )(x, w)


def _proj_res_ln_kernel(x_ref, w_ref, h_ref, g_ref, b_ref, o_ref, ob_ref, acc_ref, *, nk):
    k = pl.program_id(1)
    def part():
        return jnp.dot(x_ref[...], w_ref[...], preferred_element_type=F32)

    def finish(y):
        out = _layer_norm_rows(ALPHA * h_ref[...] + y, g_ref[...], b_ref[...])
        o_ref[...] = out
        ob_ref[...] = out.astype(BF16)

    if nk == 1:
        finish(part())
    else:
        @pl.when(k == 0)
        def _():
            acc_ref[...] = part()

        @pl.when(jnp.logical_and(k > 0, k < nk - 1))
        def _():
            acc_ref[...] += part()

        @pl.when(k == nk - 1)
        def _():
            finish(acc_ref[...] + part())


def _proj_res_ln(x, w, h, g, b, *, layer=0, tm=512, tk=2048):
    M, K = x.shape
    N = w.shape[-1]
    nk = K // tk
    if w.ndim == 3:
        w_spec = pl.BlockSpec((None, tk, N), lambda i, k: (layer, k, 0))
    else:
        w_spec = pl.BlockSpec((tk, N), lambda i, k: (k, 0))
    return pl.pallas_call(
        functools.partial(_proj_res_ln_kernel, nk=nk),
        grid=(M // tm, nk),
        in_specs=[pl.BlockSpec((tm, tk), lambda i, k: (i, k)),
                  w_spec,
                  pl.BlockSpec((tm, N), lambda i, k: (i, 0)),
                  pl.BlockSpec((1, N), lambda i, k: (0, 0)),
                  pl.BlockSpec((1, N), lambda i, k: (0, 0))],
        out_specs=[pl.BlockSpec((tm, N), lambda i, k: (i, 0)),
                   pl.BlockSpec((tm, N), lambda i, k: (i, 0))],
        out_shape=[jax.ShapeDtypeStruct((M, N), F32), jax.ShapeDtypeStruct((M, N), BF16)],
        scratch_shapes=[pltpu.VMEM((tm, N), F32)],
        compiler_params=_params("parallel", "arbitrary"),
        name="proj_res_ln",
    )(x, w, h, g.reshape(1, N), b.reshape(1, N))


def _mla_in_kernel(h_ref, w_ref, qg_ref, kvg_ref, ct_ref, st_ref, cq_ref, ckv_ref, kr_ref):
    acc = jnp.dot(h_ref[...].astype(BF16), w_ref[...], preferred_element_type=F32)
    ql, kvl = MLA_Q_LORA, MLA_KV_LORA
    cq_ref[...] = _rms_rows(acc[:, :ql], qg_ref[...]).astype(cq_ref.dtype)
    ckv_ref[...] = _rms_rows(acc[:, ql:ql + kvl], kvg_ref[...]).astype(ckv_ref.dtype)
    a = acc[:, ql + kvl:ql + kvl + LANES]
    a_sw = acc[:, ql + kvl + LANES:]
    kr_ref[...] = (a * ct_ref[...] + a_sw * st_ref[...]).astype(kr_ref.dtype)


def _mla_in(h, w_ext, qg, kvg, ct, st, S, *, tm=512):
    M, K = h.shape
    N = w_ext.shape[1]
    ns = S // tm
    return pl.pallas_call(
        _mla_in_kernel,
        grid=(M // tm,),
        in_specs=[pl.BlockSpec((tm, K), lambda i: (i, 0)),
                  pl.BlockSpec((K, N), lambda i: (0, 0)),
                  pl.BlockSpec((1, MLA_Q_LORA), lambda i: (0, 0)),
                  pl.BlockSpec((1, MLA_KV_LORA), lambda i: (0, 0)),
                  pl.BlockSpec((tm, LANES), lambda i: (i % ns, 0)),
                  pl.BlockSpec((tm, LANES), lambda i: (i % ns, 0))],
        out_specs=[pl.BlockSpec((tm, MLA_Q_LORA), lambda i: (i, 0)),
                   pl.BlockSpec((tm, MLA_KV_LORA), lambda i: (i, 0)),
                   pl.BlockSpec((tm, LANES), lambda i: (i, 0))],
        out_shape=[jax.ShapeDtypeStruct((M, MLA_Q_LORA), BF16),
                   jax.ShapeDtypeStruct((M, MLA_KV_LORA), BF16),
                   jax.ShapeDtypeStruct((M, LANES), BF16)],
        compiler_params=_params("parallel"),
        name="mla_in",
    )(h, w_ext, qg.reshape(1, -1), kvg.reshape(1, -1), ct, st)


def _mla_uq_kernel(cq_ref, wm_ref, ws_ref, ct_ref, st_ref, q_ref, *, heads, scale):
    x = cq_ref[...]
    a = jnp.dot(x, wm_ref[...], preferred_element_type=F32)
    a_sw = jnp.dot(x, ws_ref[...], preferred_element_type=F32)
    ct = ct_ref[...]
    st = st_ref[...]
    for hh in range(heads):
        lo = hh * 2 * LANES
        q_ref[:, lo:lo + LANES] = (a[:, lo:lo + LANES] * scale).astype(q_ref.dtype)
        rot = a[:, lo + LANES:lo + 2 * LANES] * ct + a_sw[:, hh * LANES:(hh + 1) * LANES] * st
        q_ref[:, lo + LANES:lo + 2 * LANES] = (rot * scale).astype(q_ref.dtype)


def _mla_uq(cq, w_main, w_sw, ct, st, S, scale, *, tm=512, heads_per_step=4):
    M, K = cq.shape
    hp = heads_per_step
    ns = S // tm
    return pl.pallas_call(
        functools.partial(_mla_uq_kernel, heads=hp, scale=scale),
        grid=(M // tm, N_HEADS // hp),
        in_specs=[pl.BlockSpec((tm, K), lambda i, j: (i, 0)),
                  pl.BlockSpec((K, hp * 2 * LANES), lambda i, j: (0, j)),
                  pl.BlockSpec((K, hp * LANES), lambda i, j: (0, j)),
                  pl.BlockSpec((tm, LANES), lambda i, j: (i % ns, 0)),
                  pl.BlockSpec((tm, LANES), lambda i, j: (i % ns, 0))],
        out_specs=pl.BlockSpec((tm, hp * 2 * LANES), lambda i, j: (i, j)),
        out_shape=jax.ShapeDtypeStruct((M, N_HEADS * 2 * LANES), BF16),
        compiler_params=_params("parallel", "parallel"),
        name="mla_uq",
    )(cq, w_main, w_sw, ct, st)


def _causal_sweep(qi, streams):
    def fill(which, ki):
        for stream in streams:
            stream[which][...] = stream[2](ki)

    def step(which, ki, states, diagonal):
        return tuple(stream[3](ki, stream[which], st, diagonal) for stream, st in zip(streams, states))

    fill(0, 0)

    def pair(p, states):
        k0 = 2 * p
        fill(1, k0 + 1)
        states = step(0, k0, states, False)
        fill(0, k0 + 2)
        return step(1, k0 + 1, states, False)

    states = lax.fori_loop(0, qi // 2, pair, tuple(stream[4] for stream in streams))

    def odd(states):
        fill(1, qi)
        states = step(0, qi - 1, states, False)
        return step(1, qi, states, True)

    def even(states):
        return step(0, qi, states, True)

    return lax.cond(qi % 2 == 1, odd, even, states)


def _softmax_update(s, state, v):
    m, acc = state
    m_new = jnp.maximum(m, jnp.max(s, axis=-1, keepdims=True))
    alpha = jnp.exp2(m - m_new)
    p = jnp.exp2(s - m_new).astype(BF16)
    v_ones = jnp.concatenate([v, jnp.ones_like(v)], axis=1)
    acc = alpha * acc + jnp.dot(p, v_ones, preferred_element_type=F32)
    return m_new, acc


def _softmax_update_diag(s_ref, state, v_rows, t):
    h = t // 2
    m, acc = state
    keep = (lax.broadcasted_iota(jnp.int32, (h, h), 1) <= lax.broadcasted_iota(jnp.int32, (h, h), 0))
    s_top = jnp.where(keep, s_ref[0:h, 0:h], -jnp.inf)
    top = _softmax_update(s_top, (m[:h], acc[:h]), v_rows(h))
    s_bot = jnp.concatenate([s_ref[h:t, 0:h], jnp.where(keep, s_ref[h:t, h:t], -jnp.inf)], axis=1)
    bot = _softmax_update(s_bot, (m[h:], acc[h:]), v_rows(t))
    return jnp.concatenate([top[0], bot[0]], axis=0), jnp.concatenate([top[1], bot[1]], axis=0)


def _softmax_init(t):
    return (jnp.full((t, 1), MASKED, F32), jnp.zeros((t, 2 * LANES), F32))


def _softmax_finish(state):
    _, acc = state
    return acc[:, :LANES] / acc[:, LANES:]


def _mla_attn_kernel(q_ref, kn_ref, kr_ref, v_ref, o_ref, *score_refs, t, heads):
    qi = pl.program_id(2)

    def stream(hd):
        lanes = slice(hd * LANES, (hd + 1) * LANES)
        q = q_ref[:, 2 * hd * LANES:2 * (hd + 1) * LANES]

        def scores(ki):
            off = pl.multiple_of(ki * t, t)
            k = jnp.concatenate([kn_ref[pl.ds(off, t), lanes], kr_ref[pl.ds(off, t), :]], axis=1)
            return lax.dot_general(q, k, _NT, preferred_element_type=F32)

        def update(ki, s_ref, state, diagonal):
            off = pl.multiple_of(ki * t, t)
            v_rows = lambda n: v_ref[pl.ds(off, n), lanes]
            if diagonal:
                return _softmax_update_diag(s_ref, state, v_rows, t)
            return _softmax_update(s_ref[...], state, v_rows(t))

        return (score_refs[2 * hd], score_refs[2 * hd + 1], scores, update, _softmax_init(t))

    states = _causal_sweep(qi, [stream(hd) for hd in range(heads)])
    for hd, state in enumerate(states):
        o_ref[:, hd * LANES:(hd + 1) * LANES] = _softmax_finish(state).astype(o_ref.dtype)


def _mla_attn(q, kv, kr, B, S, *, t=512, heads=SOFTMAX_HEADS_PER_STEP):
    nq = S // t
    G = N_HEADS // heads
    w = heads * LANES
    return pl.pallas_call(
        functools.partial(_mla_attn_kernel, t=t, heads=heads),
        grid=(B, G, nq),
        in_specs=[pl.BlockSpec((t, 2 * w), lambda b, g, i: (b * nq + i, g)),
                  pl.BlockSpec((S, w), lambda b, g, i: (b, g)),
                  pl.BlockSpec((S, LANES), lambda b, g, i: (b, 0)),
                  pl.BlockSpec((S, w), lambda b, g, i: (b, G + g))],
        out_specs=pl.BlockSpec((t, w), lambda b, g, i: (b * nq + i, g)),
        out_shape=jax.ShapeDtypeStruct((B * S, N_HEADS * MLA_V), BF16),
        scratch_shapes=[pltpu.VMEM((t, t), F32) for _ in range(2 * heads)],
        compiler_params=_params("parallel", "parallel", "arbitrary"),
        name="mla_attn",
    )(q, kv, kr, kv)


def _rope_tables(S):
    half = MLA_ROPE // 2
    inv = 1.0 / (ROPE_THETA ** (jnp.arange(0, MLA_ROPE, 2, dtype=F32) / MLA_ROPE))
    ang = jnp.arange(S, dtype=F32)[:, None] * inv[None, :]
    cos, sin = jnp.cos(ang), jnp.sin(ang)
    zeros = jnp.zeros((S, LANES - 2 * half), F32)
    return (jnp.concatenate([cos, cos, zeros], axis=1), jnp.concatenate([-sin, sin, zeros], axis=1))


def _mla_weights(w_in, w_uq, w_ukv):
    D = w_in.shape[0]
    half = MLA_ROPE // 2
    base = MLA_Q_LORA + MLA_KV_LORA
    x1, x2 = w_in[:, base:base + half], w_in[:, base + half:base + 2 * half]
    pad = jnp.zeros((D, LANES - 2 * half), w_in.dtype)
    w_in_ext = jnp.concatenate([w_in[:, :base], x1, x2, pad, x2, x1, pad], axis=1).astype(BF16)

    wq = w_uq.reshape(MLA_Q_LORA, N_HEADS, MLA_NOPE + MLA_ROPE)
    nope, r1, r2 = wq[..., :MLA_NOPE], wq[..., MLA_NOPE:MLA_NOPE + half], wq[..., MLA_NOPE + half:]
    padq = jnp.zeros((MLA_Q_LORA, N_HEADS, LANES - 2 * half), w_uq.dtype)
    w_main = jnp.concatenate([nope, r1, r2, padq], axis=-1).reshape(MLA_Q_LORA, N_HEADS * 2 * LANES).astype(BF16)
    w_sw = jnp.concatenate([r2, r1, padq], axis=-1).reshape(MLA_Q_LORA, N_HEADS * LANES).astype(BF16)

    wkv = w_ukv.reshape(MLA_KV_LORA, N_HEADS, MLA_NOPE + MLA_V)
    w_kv = jnp.concatenate([wkv[..., :MLA_NOPE].reshape(MLA_KV_LORA, -1),
                            wkv[..., MLA_NOPE:].reshape(MLA_KV_LORA, -1)], axis=1).astype(BF16)
    return w_in_ext, w_main, w_sw, w_kv


def _mla_mixer(h, B, S, w_in, q_norm, kv_norm, w_uq, w_ukv):
    w_in_ext, w_main, w_sw, w_kv = _mla_weights(w_in, w_uq, w_ukv)
    ct, st = _rope_tables(S)
    cq, ckv, kr = _mla_in(h, w_in_ext, q_norm, kv_norm, ct, st, S)
    scale = float((MLA_NOPE + MLA_ROPE) ** -0.5) * LOG2E
    q = _mla_uq(cq, w_main, w_sw, ct, st, S, scale)
    kv = _proj(ckv, w_kv, BF16)
    return _mla_attn(q, kv, kr, B, S)


def _hgrn_kernel(q_ref, f_ref, i_ref, g_ref, lbl_ref, on_ref, o_ref, state_ref, b_scr, k_scr,
                 *, layer, tile, chunk):
    C, SB = chunk, HGRN_SUB
    nb = C // SB

    @pl.when(pl.program_id(2) == 0)
    def _():
        state_ref[...] = jnp.zeros_like(state_ref)

    lg = lbl_ref[...]
    e = jnp.exp(lg - jnp.max(lg, axis=0, keepdims=True))
    p = e / jnp.sum(e, axis=0, keepdims=True)
    cs = p[0:1]
    for r in range(1, layer + 1):
        cs = cs + p[r:r + 1]
    lb = cs - p[0:1]
    log_lb = jnp.log(lb)
    log1m_lb = jnp.log1p(-lb)
    one_m_lb = 1.0 - lb
    onorm = on_ref[...]

    row = lax.broadcasted_iota(jnp.int32, (C, 3 * C), 0)
    col = lax.broadcasted_iota(jnp.int32, (C, 3 * C), 1) % C
    blk0 = (row // SB) * SB
    tri = jnp.concatenate([col <= row, col <= blk0, col <= jnp.minimum(blk0 + SB, C - 1)],
                          axis=0).astype(BF16)
    brow = lax.broadcasted_iota(jnp.int32, (C, C), 0) // SB
    bcol = lax.broadcasted_iota(jnp.int32, (C, C), 1) // SB
    bdiff = brow - bcol
    sub = lax.broadcasted_iota(jnp.int32, (SB, LANES), 0)
    causal_cap = [jnp.where(sub >= s, 0.0, -jnp.inf).astype(F32) for s in range(SB)]

    def chunk_step(c, slot):
        off = pl.multiple_of(c * C, C)
        q = q_ref[pl.ds(off, C), :]
        fp = f_ref[pl.ds(off, C), :]
        v = i_ref[pl.ds(off, C), :]
        g = g_ref[pl.ds(off, C), :]

        ls = jnp.minimum(fp, 0.0) - jnp.log(1.0 + jnp.exp(-jnp.abs(fp)))
        cc = log1m_lb + ls
        lf = jnp.maximum(log_lb, cc) + jnp.log(1.0 + jnp.exp(-jnp.abs(log_lb - cc)))
        kk = one_m_lb * jax.nn.sigmoid(-fp)

        p1 = lf.astype(BF16)
        r1 = lf - p1.astype(F32)
        p2 = r1.astype(BF16)
        p3 = (r1 - p2.astype(F32)).astype(BF16)
        cums = jnp.dot(tri, jnp.concatenate([p1, p2, p3], axis=0), preferred_element_type=F32)
        b, r, r_next = cums[:C], cums[C:2 * C], cums[2 * C:]
        bend = b[C - 1:C, :]
        b_scr[slot] = b
        k_scr[slot] = kk

        st_t = state_ref[...]
        qe = q * jnp.exp(b)
        o = lax.dot_general(qe.astype(BF16), st_t.astype(BF16), _NT, preferred_element_type=F32)
        kd = kk * jnp.exp(bend - b)
        state_ref[...] = st_t * jnp.exp(bend) + jnp.dot(v.T.astype(BF16), kd.astype(BF16),
                                                       preferred_element_type=F32)

        kt = kk * jnp.exp(r_next - b)
        gdec = jnp.exp(r_next - r)
        ql = q * jnp.exp(b - r)
        levels = [ql]
        for lvl in range(1, nb - 1):
            fac = jnp.concatenate([jnp.zeros((lvl * SB, LANES), F32), gdec[:C - lvl * SB]], axis=0)
            ql = ql * fac
            levels.append(ql)
        qs = jnp.concatenate(levels, axis=0).astype(BF16)
        rl = lax.dot_general(qs, kt.astype(BF16), _NT, preferred_element_type=F32)
        a = jnp.zeros((C, C), F32)
        for lvl in range(1, nb):
            a = a + jnp.where(bdiff == lvl, rl[(lvl - 1) * C:lvl * C, :], 0.0)
        o = o + jnp.dot(a.astype(BF16), v.astype(BF16), preferred_element_type=F32)

        diag = []
        for blk in range(nb):
            bq = b[blk * SB:(blk + 1) * SB]
            qq = q[blk * SB:(blk + 1) * SB]
            acc = jnp.zeros((SB, LANES), F32)
            for s in range(SB):
                rr = blk * SB + s
                dec = jnp.exp(jnp.minimum(bq - b_scr[slot, pl.ds(rr, 1), :], causal_cap[s]))
                a_ts = jnp.sum(qq * k_scr[slot, pl.ds(rr, 1), :] * dec, axis=-1, keepdims=True)
                acc = acc + a_ts * i_ref[pl.ds(off + rr, 1), :]
            diag.append(acc)
        o = o + jnp.concatenate(diag, axis=0)

        y = _rms_rows(o, onorm)
        o_ref[pl.ds(off, C), :] = (y * (g * jax.nn.sigmoid(g))).astype(o_ref.dtype)

    def chunk_group(grp, carry):
        for slot in range(HGRN_GROUP):
            chunk_step(HGRN_GROUP * grp + slot, slot)
        return carry

    lax.fori_loop(0, tile // (HGRN_GROUP * C), chunk_group, 0)


def _hgrn_mixer_core(proj, lb_logits, o_norm, B, S, layer, *, tile=512):
    nt = S // tile
    H = N_HEADS
    blk = lambda sec: pl.BlockSpec((tile, LANES), lambda b, h, t, sec=sec: (b * nt + t, sec * H + h))
    return pl.pallas_call(
        functools.partial(_hgrn_kernel, layer=layer, tile=tile, chunk=HGRN_CHUNK),
        grid=(B, H, nt),
        in_specs=[blk(0), blk(1), blk(2), blk(3),
                  pl.BlockSpec((DEPTH, LANES), lambda b, h, t: (0, h)),
                  pl.BlockSpec((1, LANES), lambda b, h, t: (0, h))],
        out_specs=pl.BlockSpec((tile, LANES), lambda b, h, t: (b * nt + t, h)),
        out_shape=jax.ShapeDtypeStruct((B * S, D_MODEL), BF16),
        scratch_shapes=[pltpu.VMEM((LANES, LANES), F32),
                        pltpu.VMEM((HGRN_GROUP, HGRN_CHUNK, LANES), F32),
                        pltpu.VMEM((HGRN_GROUP, HGRN_CHUNK, LANES), F32)],
        compiler_params=_params("parallel", "parallel", "arbitrary"),
        name="hgrn",
    )(proj, proj, proj, proj, lb_logits, o_norm.reshape(1, -1))


SB_CUMSUM_BLOCK = 256


def _sb_kernel(q_ref, k_ref, v_ref, o_ref, *score_refs, t, heads):
    cb = SB_CUMSUM_BLOCK
    qi = pl.program_id(2)
    row = lax.broadcasted_iota(jnp.int32, (t, t), 0)
    col = lax.broadcasted_iota(jnp.int32, (t, t), 1)
    strict = col < row
    jj = lax.broadcasted_iota(jnp.int32, (2 * cb, cb), 0) % cb
    ss = lax.broadcasted_iota(jnp.int32, (2 * cb, cb), 1)
    from_s2 = (jj >= ss).astype(BF16)

    def stream(hd):
        lanes = slice(hd * LANES, (hd + 1) * LANES)
        q = q_ref[:, lanes]

        def scores(ki):
            off = pl.multiple_of(ki * t, t)
            return lax.dot_general(q, k_ref[pl.ds(off, t), lanes], _NT, preferred_element_type=F32)

        def update(ki, zn_ref, acc, diagonal):
            off = pl.multiple_of(ki * t, t)
            tail = None
            new = None
            for j in reversed(range(t // cb)):
                cols = slice(j * cb, (j + 1) * cb)
                zn = zn_ref[:, cols]
                l1m = jnp.minimum(zn, 0.0) - jnp.log(1.0 + jnp.exp2(jnp.abs(zn) * (-LOG2E)))
                if diagonal:
                    l1m = jnp.where(strict[:, cols], l1m, 0.0)
                hi = l1m.astype(BF16)
                lo = (l1m - hi.astype(F32)).astype(BF16)
                x = jnp.dot(jnp.concatenate([hi, lo], axis=1), from_s2, preferred_element_type=F32)
                if tail is not None:
                    x = x + tail
                tail = x[:, 0:1]
                w = jnp.exp(x - zn)
                if diagonal:
                    w = jnp.where(strict[:, cols], w, 0.0)
                part = jnp.dot(w.astype(BF16), v_ref[pl.ds(off + j * cb, cb), lanes],
                               preferred_element_type=F32)
                new = part if new is None else new + part
            return acc * jnp.exp(tail) + new

        return (score_refs[2 * hd], score_refs[2 * hd + 1], scores, update, jnp.zeros((t, HEAD_DIM), F32))

    accs = _causal_sweep(qi, [stream(hd) for hd in range(heads)])
    for hd, acc in enumerate(accs):
        o_ref[:, hd * LANES:(hd + 1) * LANES] = acc.astype(o_ref.dtype)


def _sb_attn(qkv, B, S, *, t=512, heads=ATTN_HEADS_PER_STEP):
    nq = S // t
    G = N_HEADS // heads
    w = heads * LANES
    return pl.pallas_call(
        functools.partial(_sb_kernel, t=t, heads=heads),
        grid=(B, G, nq),
        in_specs=[pl.BlockSpec((t, w), lambda b, g, i: (b * nq + i, g)),
                  pl.BlockSpec((S, w), lambda b, g, i: (b, G + g)),
                  pl.BlockSpec((S, w), lambda b, g, i: (b, 2 * G + g))],
        out_specs=pl.BlockSpec((t, w), lambda b, g, i: (b * nq + i, g)),
        out_shape=jax.ShapeDtypeStruct((B * S, D_MODEL), BF16),
        scratch_shapes=[pltpu.VMEM((t, t), F32) for _ in range(2 * heads)],
        compiler_params=_params("parallel", "parallel", "arbitrary"),
        name="sb_attn",
    )(qkv, qkv, qkv)


MOBA_VETO = 2.0 ** 100


def _moba_kernel(q_ref, k_ref, v_ref, o_ref, kmean_ref, *score_refs, nblk, t, heads):
    bpt = t // MOBA_BLOCK
    qi = pl.program_id(2)

    @pl.when(qi == 0)
    def _():
        for hd in range(heads):
            k_all = k_ref[:, hd * LANES:(hd + 1) * LANES].astype(F32)
            kmean_ref[hd] = jnp.mean(k_all.reshape(nblk, MOBA_BLOCK, LANES), axis=1)

    blk = lax.broadcasted_iota(jnp.int32, (nblk, t), 0)
    own = qi * bpt + lax.broadcasted_iota(jnp.int32, (nblk, t), 1) // MOBA_BLOCK
    neg_inf = jnp.float32(-jnp.inf)
    lane_blk = lax.broadcasted_iota(jnp.int32, (t, LANES), 1)
    key_blk = lax.broadcasted_iota(jnp.int32, (t, LANES), 0) // MOBA_BLOCK

    def stream(hd):
        lanes = slice(hd * LANES, (hd + 1) * LANES)
        q = q_ref[:, lanes]

        gate = lax.dot_general(kmean_ref[hd], q.astype(F32), _NT, precision=lax.Precision.HIGHEST,
                               preferred_element_type=F32)
        gate = jnp.where(blk < own, gate, neg_inf)
        sel = (blk == own).astype(F32)
        for _ in range(MOBA_TOPK):
            mx = jnp.max(gate, axis=0, keepdims=True)
            first = jnp.min(jnp.where(gate == mx, blk, nblk), axis=0, keepdims=True)
            pick = jnp.logical_and(blk == first, mx > neg_inf)
            sel = jnp.where(pick, 1.0, sel)
            gate = jnp.where(pick, neg_inf, gate)

        veto = jnp.concatenate([sel - 1.0, jnp.zeros((LANES - nblk, t), F32)], axis=0)
        q_ext = jnp.concatenate([q, veto.T.astype(BF16)], axis=1)

        def scores(kc):
            off = pl.multiple_of(kc * t, t)
            hot = jnp.where(lane_blk == kc * bpt + key_blk, MOBA_VETO, 0.0).astype(BF16)
            k_ext = jnp.concatenate([k_ref[pl.ds(off, t), lanes], hot], axis=1)
            return lax.dot_general(q_ext, k_ext, _NT, preferred_element_type=F32)

        def update(kc, s_ref, state, diagonal):
            off = pl.multiple_of(kc * t, t)
            v_rows = lambda n: v_ref[pl.ds(off, n), lanes]
            if diagonal:
                return _softmax_update_diag(s_ref, state, v_rows, t)
            return _softmax_update(s_ref[...], state, v_rows(t))

        return (score_refs[2 * hd], score_refs[2 * hd + 1], scores, update, _softmax_init(t))

    states = _causal_sweep(qi, [stream(hd) for hd in range(heads)])
    for hd, state in enumerate(states):
        o_ref[:, hd * LANES:(hd + 1) * LANES] = _softmax_finish(state).astype(o_ref.dtype)


def _moba_attn(qkv, B, S, *, t=2 * MOBA_BLOCK, heads=SOFTMAX_HEADS_PER_STEP):
    assert t == 2 * MOBA_BLOCK
    nq = S // t
    nblk = S // MOBA_BLOCK
    G = N_HEADS // heads
    w = heads * LANES
    return pl.pallas_call(
        functools.partial(_moba_kernel, nblk=nblk, t=t, heads=heads),
        grid=(B, G, nq),
        in_specs=[pl.BlockSpec((t, w), lambda b, g, i: (b * nq + i, g)),
                  pl.BlockSpec((S, w), lambda b, g, i: (b, G + g)),
                  pl.BlockSpec((S, w), lambda b, g, i: (b, 2 * G + g))],
        out_specs=pl.BlockSpec((t, w), lambda b, g, i: (b * nq + i, g)),
        out_shape=jax.ShapeDtypeStruct((B * S, D_MODEL), BF16),
        scratch_shapes=[pltpu.VMEM((heads, nblk, LANES), F32)]
                       + [pltpu.VMEM((t, t), F32) for _ in range(2 * heads)],
        compiler_params=_params("parallel", "parallel", "arbitrary"),
        name="moba_attn",
    )(qkv, qkv, qkv)


def _qkv_proj(h, w_in, layer, q_scale):
    return _proj(h, w_in, BF16, layer=layer, scaled_cols=D_MODEL, scale=q_scale)


def _sb_mixer(h, B, S, w_in, layer=0):
    return _sb_attn(_qkv_proj(h, w_in, layer, -float(HEAD_DIM ** -0.5)), B, S)


def _moba_mixer(h, B, S, w_in, layer=0):
    return _moba_attn(_qkv_proj(h, w_in, layer, float(HEAD_DIM ** -0.5) * LOG2E), B, S)


def kernel(x, mla_w_in, mla_q_norm, mla_kv_norm, mla_w_uq, mla_w_ukv, mla_w_o, hgrn_w_in, hgrn_lb_logits, hgrn_o_norm, hgrn_w_o, sb_w_in, sb_w_o, moba_w_in, moba_w_o, ln_g, ln_b, mlp_w1, mlp_w2):
    B, S, D = x.shape
    assert D == D_MODEL and S % MOBA_BLOCK == 0 and S % 512 == 0
    h = x.reshape(B * S, D)
    hb = h
    mlp_w2_bf16 = mlp_w2.astype(BF16)
    n_mixers = 4
    for i in range(DEPTH):
        kind, slot = i % n_mixers, i // n_mixers
        if kind == 0:
            o = _mla_mixer(hb, B, S, mla_w_in[slot], mla_q_norm[slot], mla_kv_norm[slot],
                           mla_w_uq[slot], mla_w_ukv[slot])
            w_o = mla_w_o[slot]
        elif kind == 1:
            proj = _proj(hb, hgrn_w_in, F32, layer=slot)
            o = _hgrn_mixer_core(proj, hgrn_lb_logits, hgrn_o_norm[slot], B, S, i)
            w_o = hgrn_w_o[slot]
        elif kind == 2:
            o = _sb_mixer(hb, B, S, sb_w_in, slot)
            w_o = sb_w_o[slot]
        else:
            o = _moba_mixer(hb, B, S, moba_w_in, slot)
            w_o = moba_w_o[slot]
        h, hb = _proj_res_ln(o, w_o.astype(BF16), h, ln_g[i, 0], ln_b[i, 0])
        a = _proj(hb, mlp_w1, BF16, layer=i, act="relu2")
        h, hb = _proj_res_ln(a, mlp_w2_bf16, h, ln_g[i, 1], ln_b[i, 1], layer=i)
    return h.reshape(B, S, D)
```

```python
import functools

import jax
import jax.numpy as jnp
from jax import lax
from jax.experimental import pallas as pl
from jax.experimental.pallas import tpu as pltpu

F32 = jnp.float32
BF16 = jnp.bfloat16

D_MODEL = 2048
DEPTH = 4
N_HEADS = 16
HEAD_DIM = 128
MLA_Q_LORA = 512
MLA_KV_LORA = 512
MLA_NOPE = 128
MLA_ROPE = 64
MLA_V = 128
ROPE_THETA = 10000.0
HGRN_CHUNK = 64
HGRN_SUB = 8
HGRN_GROUP = 8
HGRN_HEADS_PER_STEP = 4
ATTN_HEADS_PER_STEP = 2
SOFTMAX_HEADS_PER_STEP = 4
MOBA_BLOCK = 256
MOBA_TOPK = 3
ALPHA = float((2 * DEPTH) ** 0.25)
LN_EPS = 1e-5
RMS_EPS = 1e-6

V7X_VMEM_BYTES = 64 * 1024 * 1024
VMEM_LIMIT = V7X_VMEM_BYTES - 8 * 1024 * 1024
LANES = 128
MASKED = -1e30
LOG2E = 1.4426950408889634

_NT = (((1,), (1,)), ((), ()))


def _params(*sem):
    return pltpu.CompilerParams(dimension_semantics=sem, vmem_limit_bytes=VMEM_LIMIT)


def _layer_norm_rows(y, g, b):
    mu = jnp.mean(y, axis=-1, keepdims=True)
    d = y - mu
    var = jnp.mean(d * d, axis=-1, keepdims=True)
    return d * lax.rsqrt(var + LN_EPS) * g + b


def _rms_rows(x, g):
    return x * lax.rsqrt(jnp.mean(x * x, axis=-1, keepdims=True) + RMS_EPS) * g


def _proj_kernel(x_ref, w_ref, o_ref, wbf_ref, *, act, scaled_tiles, scale):
    @pl.when(pl.program_id(1) == 0)
    def _():
        wbf_ref[...] = w_ref[...].astype(BF16)

    acc = jnp.dot(x_ref[...].astype(BF16), wbf_ref[...], preferred_element_type=F32)
    if act == "relu2":
        r = jnp.maximum(acc, 0.0)
        acc = r * r
    if scaled_tiles:
        acc = acc * jnp.where(pl.program_id(0) < scaled_tiles, scale, 1.0)
    o_ref[...] = acc.astype(o_ref.dtype)


def _proj(x, w, out_dtype, *, layer=0, tm=1024, tn=1024, act=None, scaled_cols=0, scale=1.0):
    M, K = x.shape
    N = w.shape[-1]
    tm, tn = min(tm, M), min(tn, N)
    assert scaled_cols % tn == 0 and M % tm == 0 and N % tn == 0
    if w.ndim == 3:
        w_spec = pl.BlockSpec((None, K, tn), lambda j, i: (layer, 0, j))
    else:
        w_spec = pl.BlockSpec((K, tn), lambda j, i: (0, j))
    return pl.pallas_call(
        functools.partial(_proj_kernel, act=act, scaled_tiles=scaled_cols // tn, scale=scale),
        grid=(N // tn, M // tm),
        in_specs=[pl.BlockSpec((tm, K), lambda j, i: (i, 0)), w_spec],
        out_specs=pl.BlockSpec((tm, tn), lambda j, i: (i, j)),
        out_shape=jax.ShapeDtypeStruct((M, N), out_dtype),
        scratch_shapes=[pltpu.VMEM((K, tn), BF16)],
        compiler_params=_params("parallel", "arbitrary"),
        name="proj",
    )(x, w)


def _proj_res_ln_kernel(x_ref, w_ref, h_ref, g_ref, b_ref, o_ref, ob_ref, acc_ref, *, nk):
    k = pl.program_id(1)
    def part():
        return jnp.dot(x_ref[...], w_ref[...], preferred_element_type=F32)

    def finish(y):
        out = _layer_norm_rows(ALPHA * h_ref[...] + y, g_ref[...], b_ref[...])
        o_ref[...] = out
        ob_ref[...] = out.astype(BF16)

    if nk == 1:
        finish(part())
    else:
        @pl.when(k == 0)
        def _():
            acc_ref[...] = part()

        @pl.when(jnp.logical_and(k > 0, k < nk - 1))
        def _():
            acc_ref[...] += part()

        @pl.when(k == nk - 1)
        def _():
            finish(acc_ref[...] + part())


def _proj_res_ln(x, w, h, g, b, *, layer=0, tm=512, tk=2048):
    M, K = x.shape
    N = w.shape[-1]
    nk = K // tk
    if w.ndim == 3:
        w_spec = pl.BlockSpec((None, tk, N), lambda i, k: (layer, k, 0))
    else:
        w_spec = pl.BlockSpec((tk, N), lambda i, k: (k, 0))
    return pl.pallas_call(
        functools.partial(_proj_res_ln_kernel, nk=nk),
        grid=(M // tm, nk),
        in_specs=[pl.BlockSpec((tm, tk), lambda i, k: (i, k)),
                  w_spec,
                  pl.BlockSpec((tm, N), lambda i, k: (i, 0)),
                  pl.BlockSpec((1, N), lambda i, k: (0, 0)),
                  pl.BlockSpec((1, N), lambda i, k: (0, 0))],
        out_specs=[pl.BlockSpec((tm, N), lambda i, k: (i, 0)),
                   pl.BlockSpec((tm, N), lambda i, k: (i, 0))],
        out_shape=[jax.ShapeDtypeStruct((M, N), F32), jax.ShapeDtypeStruct((M, N), BF16)],
        scratch_shapes=[pltpu.VMEM((tm, N), F32)],
        compiler_params=_params("parallel", "arbitrary"),
        name="proj_res_ln",
    )(x, w, h, g.reshape(1, N), b.reshape(1, N))


def _mla_in_kernel(h_ref, w_ref, qg_ref, kvg_ref, ct_ref, st_ref, cq_ref, ckv_ref, kr_ref):
    acc = jnp.dot(h_ref[...].astype(BF16), w_ref[...], preferred_element_type=F32)
    ql, kvl = MLA_Q_LORA, MLA_KV_LORA
    cq_ref[...] = _rms_rows(acc[:, :ql], qg_ref[...]).astype(cq_ref.dtype)
    ckv_ref[...] = _rms_rows(acc[:, ql:ql + kvl], kvg_ref[...]).astype(ckv_ref.dtype)
    a = acc[:, ql + kvl:ql + kvl + LANES]
    a_sw = acc[:, ql + kvl + LANES:]
    kr_ref[...] = (a * ct_ref[...] + a_sw * st_ref[...]).astype(kr_ref.dtype)


def _mla_in(h, w_ext, qg, kvg, ct, st, S, *, tm=512):
    M, K = h.shape
    N = w_ext.shape[1]
    ns = S // tm
    return pl.pallas_call(
        _mla_in_kernel,
        grid=(M // tm,),
        in_specs=[pl.BlockSpec((tm, K), lambda i: (i, 0)),
                  pl.BlockSpec((K, N), lambda i: (0, 0)),
                  pl.BlockSpec((1, MLA_Q_LORA), lambda i: (0, 0)),
                  pl.BlockSpec((1, MLA_KV_LORA), lambda i: (0, 0)),
                  pl.BlockSpec((tm, LANES), lambda i: (i % ns, 0)),
                  pl.BlockSpec((tm, LANES), lambda i: (i % ns, 0))],
        out_specs=[pl.BlockSpec((tm, MLA_Q_LORA), lambda i: (i, 0)),
                   pl.BlockSpec((tm, MLA_KV_LORA), lambda i: (i, 0)),
                   pl.BlockSpec((tm, LANES), lambda i: (i, 0))],
        out_shape=[jax.ShapeDtypeStruct((M, MLA_Q_LORA), BF16),
                   jax.ShapeDtypeStruct((M, MLA_KV_LORA), BF16),
                   jax.ShapeDtypeStruct((M, LANES), BF16)],
        compiler_params=_params("parallel"),
        name="mla_in",
    )(h, w_ext, qg.reshape(1, -1), kvg.reshape(1, -1), ct, st)


def _mla_uq_kernel(cq_ref, wm_ref, ws_ref, ct_ref, st_ref, q_ref, *, heads, scale):
    x = cq_ref[...]
    a = jnp.dot(x, wm_ref[...], preferred_element_type=F32)
    a_sw = jnp.dot(x, ws_ref[...], preferred_element_type=F32)
    ct = ct_ref[...]
    st = st_ref[...]
    for hh in range(heads):
        lo = hh * 2 * LANES
        q_ref[:, lo:lo + LANES] = (a[:, lo:lo + LANES] * scale).astype(q_ref.dtype)
        rot = a[:, lo + LANES:lo + 2 * LANES] * ct + a_sw[:, hh * LANES:(hh + 1) * LANES] * st
        q_ref[:, lo + LANES:lo + 2 * LANES] = (rot * scale).astype(q_ref.dtype)


def _mla_uq(cq, w_main, w_sw, ct, st, S, scale, *, tm=512, heads_per_step=4):
    M, K = cq.shape
    hp = heads_per_step
    ns = S // tm
    return pl.pallas_call(
        functools.partial(_mla_uq_kernel, heads=hp, scale=scale),
        grid=(M // tm, N_HEADS // hp),
        in_specs=[pl.BlockSpec((tm, K), lambda i, j: (i, 0)),
                  pl.BlockSpec((K, hp * 2 * LANES), lambda i, j: (0, j)),
                  pl.BlockSpec((K, hp * LANES), lambda i, j: (0, j)),
                  pl.BlockSpec((tm, LANES), lambda i, j: (i % ns, 0)),
                  pl.BlockSpec((tm, LANES), lambda i, j: (i % ns, 0))],
        out_specs=pl.BlockSpec((tm, hp * 2 * LANES), lambda i, j: (i, j)),
        out_shape=jax.ShapeDtypeStruct((M, N_HEADS * 2 * LANES), BF16),
        compiler_params=_params("parallel", "parallel"),
        name="mla_uq",
    )(cq, w_main, w_sw, ct, st)


def _causal_sweep(qi, streams):
    def fill(which, ki):
        for stream in streams:
            stream[which][...] = stream[2](ki)

    def step(which, ki, states, diagonal):
        return tuple(stream[3](ki, stream[which], st, diagonal) for stream, st in zip(streams, states))

    fill(0, 0)

    def pair(p, states):
        k0 = 2 * p
        fill(1, k0 + 1)
        states = step(0, k0, states, False)
        fill(0, k0 + 2)
        return step(1, k0 + 1, states, False)

    states = lax.fori_loop(0, qi // 2, pair, tuple(stream[4] for stream in streams))

    def odd(states):
        fill(1, qi)
        states = step(0, qi - 1, states, False)
        return step(1, qi, states, True)

    def even(states):
        return step(0, qi, states, True)

    return lax.cond(qi % 2 == 1, odd, even, states)


def _softmax_update(s, state, v):
    m, acc = state
    m_new = jnp.maximum(m, jnp.max(s, axis=-1, keepdims=True))
    alpha = jnp.exp2(m - m_new)
    p = jnp.exp2(s - m_new).astype(BF16)
    v_ones = jnp.concatenate([v, jnp.ones_like(v)], axis=1)
    acc = alpha * acc + jnp.dot(p, v_ones, preferred_element_type=F32)
    return m_new, acc


def _softmax_update_diag(s_ref, state, v_rows, t):
    h = t // 2
    m, acc = state
    keep = (lax.broadcasted_iota(jnp.int32, (h, h), 1) <= lax.broadcasted_iota(jnp.int32, (h, h), 0))
    s_top = jnp.where(keep, s_ref[0:h, 0:h], -jnp.inf)
    top = _softmax_update(s_top, (m[:h], acc[:h]), v_rows(h))
    s_bot = jnp.concatenate([s_ref[h:t, 0:h], jnp.where(keep, s_ref[h:t, h:t], -jnp.inf)], axis=1)
    bot = _softmax_update(s_bot, (m[h:], acc[h:]), v_rows(t))
    return jnp.concatenate([top[0], bot[0]], axis=0), jnp.concatenate([top[1], bot[1]], axis=0)


def _softmax_init(t):
    return (jnp.full((t, 1), MASKED, F32), jnp.zeros((t, 2 * LANES), F32))


def _softmax_finish(state):
    _, acc = state
    return acc[:, :LANES] / acc[:, LANES:]


def _mla_attn_kernel(q_ref, kn_ref, kr_ref, v_ref, o_ref, *score_refs, t, heads):
    qi = pl.program_id(2)

    def stream(hd):
        lanes = slice(hd * LANES, (hd + 1) * LANES)
        q = q_ref[:, 2 * hd * LANES:2 * (hd + 1) * LANES]

        def scores(ki):
            off = pl.multiple_of(ki * t, t)
            k = jnp.concatenate([kn_ref[pl.ds(off, t), lanes], kr_ref[pl.ds(off, t), :]], axis=1)
            return lax.dot_general(q, k, _NT, preferred_element_type=F32)

        def update(ki, s_ref, state, diagonal):
            off = pl.multiple_of(ki * t, t)
            v_rows = lambda n: v_ref[pl.ds(off, n), lanes]
            if diagonal:
                return _softmax_update_diag(s_ref, state, v_rows, t)
            return _softmax_update(s_ref[...], state, v_rows(t))

        return (score_refs[2 * hd], score_refs[2 * hd + 1], scores, update, _softmax_init(t))

    states = _causal_sweep(qi, [stream(hd) for hd in range(heads)])
    for hd, state in enumerate(states):
        o_ref[:, hd * LANES:(hd + 1) * LANES] = _softmax_finish(state).astype(o_ref.dtype)


def _mla_attn(q, kv, kr, B, S, *, t=512, heads=SOFTMAX_HEADS_PER_STEP):
    nq = S // t
    G = N_HEADS // heads
    w = heads * LANES
    return pl.pallas_call(
        functools.partial(_mla_attn_kernel, t=t, heads=heads),
        grid=(B, G, nq),
        in_specs=[pl.BlockSpec((t, 2 * w), lambda b, g, i: (b * nq + i, g)),
                  pl.BlockSpec((S, w), lambda b, g, i: (b, g)),
                  pl.BlockSpec((S, LANES), lambda b, g, i: (b, 0)),
                  pl.BlockSpec((S, w), lambda b, g, i: (b, G + g))],
        out_specs=pl.BlockSpec((t, w), lambda b, g, i: (b * nq + i, g)),
        out_shape=jax.ShapeDtypeStruct((B * S, N_HEADS * MLA_V), BF16),
        scratch_shapes=[pltpu.VMEM((t, t), F32) for _ in range(2 * heads)],
        compiler_params=_params("parallel", "parallel", "arbitrary"),
        name="mla_attn",
    )(q, kv, kr, kv)


def _rope_tables(S):
    half = MLA_ROPE // 2
    inv = 1.0 / (ROPE_THETA ** (jnp.arange(0, MLA_ROPE, 2, dtype=F32) / MLA_ROPE))
    ang = jnp.arange(S, dtype=F32)[:, None] * inv[None, :]
    cos, sin = jnp.cos(ang), jnp.sin(ang)
    zeros = jnp.zeros((S, LANES - 2 * half), F32)
    return (jnp.concatenate([cos, cos, zeros], axis=1), jnp.concatenate([-sin, sin, zeros], axis=1))


def _mla_weights(w_in, w_uq, w_ukv):
    D = w_in.shape[0]
    half = MLA_ROPE // 2
    base = MLA_Q_LORA + MLA_KV_LORA
    x1, x2 = w_in[:, base:base + half], w_in[:, base + half:base + 2 * half]
    pad = jnp.zeros((D, LANES - 2 * half), w_in.dtype)
    w_in_ext = jnp.concatenate([w_in[:, :base], x1, x2, pad, x2, x1, pad], axis=1).astype(BF16)

    wq = w_uq.reshape(MLA_Q_LORA, N_HEADS, MLA_NOPE + MLA_ROPE)
    nope, r1, r2 = wq[..., :MLA_NOPE], wq[..., MLA_NOPE:MLA_NOPE + half], wq[..., MLA_NOPE + half:]
    padq = jnp.zeros((MLA_Q_LORA, N_HEADS, LANES - 2 * half), w_uq.dtype)
    w_main = jnp.concatenate([nope, r1, r2, padq], axis=-1).reshape(MLA_Q_LORA, N_HEADS * 2 * LANES).astype(BF16)
    w_sw = jnp.concatenate([r2, r1, padq], axis=-1).reshape(MLA_Q_LORA, N_HEADS * LANES).astype(BF16)

    wkv = w_ukv.reshape(MLA_KV_LORA, N_HEADS, MLA_NOPE + MLA_V)
    w_kv = jnp.concatenate([wkv[..., :MLA_NOPE].reshape(MLA_KV_LORA, -1),
                            wkv[..., MLA_NOPE:].reshape(MLA_KV_LORA, -1)], axis=1).astype(BF16)
    return w_in_ext, w_main, w_sw, w_kv


def _mla_mixer(h, B, S, w_in, q_norm, kv_norm, w_uq, w_ukv):
    w_in_ext, w_main, w_sw, w_kv = _mla_weights(w_in, w_uq, w_ukv)
    ct, st = _rope_tables(S)
    cq, ckv, kr = _mla_in(h, w_in_ext, q_norm, kv_norm, ct, st, S)
    scale = float((MLA_NOPE + MLA_ROPE) ** -0.5) * LOG2E
    q = _mla_uq(cq, w_main, w_sw, ct, st, S, scale)
    kv = _proj(ckv, w_kv, BF16)
    return _mla_attn(q, kv, kr, B, S)


def _hgrn_kernel(q_ref, f_ref, i_ref, g_ref, lbl_ref, on_ref, o_ref, state_ref, b_scr, k_scr,
                 *, layer, tile, chunk, heads):
    C, SB = chunk, HGRN_SUB
    nb = C // SB

    @pl.when(pl.program_id(2) == 0)
    def _():
        state_ref[...] = jnp.zeros_like(state_ref)

    lg = lbl_ref[...]
    e = jnp.exp(lg - jnp.max(lg, axis=0, keepdims=True))
    p = e / jnp.sum(e, axis=0, keepdims=True)
    cs = p[0:1]
    for r in range(1, layer + 1):
        cs = cs + p[r:r + 1]
    lb_all = cs - p[0:1]
    log_lb_all = jnp.log(lb_all)
    log1m_lb_all = jnp.log1p(-lb_all)
    one_m_lb_all = 1.0 - lb_all
    onorm_all = on_ref[...]

    row = lax.broadcasted_iota(jnp.int32, (C, 3 * C), 0)
    col = lax.broadcasted_iota(jnp.int32, (C, 3 * C), 1) % C
    blk0 = (row // SB) * SB
    tri = jnp.concatenate([col <= row, col <= blk0, col <= jnp.minimum(blk0 + SB, C - 1)],
                          axis=0).astype(BF16)
    brow = lax.broadcasted_iota(jnp.int32, (C, C), 0) // SB
    bcol = lax.broadcasted_iota(jnp.int32, (C, C), 1) // SB
    bdiff = brow - bcol
    sub = lax.broadcasted_iota(jnp.int32, (SB, LANES), 0)
    causal_cap = [jnp.where(sub >= s, 0.0, -jnp.inf).astype(F32) for s in range(SB)]

    def chunk_step(c, slot, hd):
        lanes = slice(hd * LANES, (hd + 1) * LANES)
        log_lb, log1m_lb, one_m_lb = log_lb_all[:, lanes], log1m_lb_all[:, lanes], one_m_lb_all[:, lanes]
        onorm = onorm_all[:, lanes]
        slot = hd * HGRN_GROUP + slot
        off = pl.multiple_of(c * C, C)
        q = q_ref[pl.ds(off, C), lanes]
        fp = f_ref[pl.ds(off, C), lanes]
        v = i_ref[pl.ds(off, C), lanes]
        g = g_ref[pl.ds(off, C), lanes]

        ls = jnp.minimum(fp, 0.0) - jnp.log(1.0 + jnp.exp(-jnp.abs(fp)))
        cc = log1m_lb + ls
        lf = jnp.maximum(log_lb, cc) + jnp.log(1.0 + jnp.exp(-jnp.abs(log_lb - cc)))
        kk = one_m_lb * jax.nn.sigmoid(-fp)

        p1 = lf.astype(BF16)
        r1 = lf - p1.astype(F32)
        p2 = r1.astype(BF16)
        p3 = (r1 - p2.astype(F32)).astype(BF16)
        cums = jnp.dot(tri, jnp.concatenate([p1, p2, p3], axis=0), preferred_element_type=F32)
        b, r, r_next = cums[:C], cums[C:2 * C], cums[2 * C:]
        bend = b[C - 1:C, :]
        b_scr[slot] = b
        k_scr[slot] = kk

        st_t = state_ref[hd]
        qe = q * jnp.exp(b)
        o = lax.dot_general(qe.astype(BF16), st_t.astype(BF16), _NT, preferred_element_type=F32)
        kd = kk * jnp.exp(bend - b)
        state_ref[hd] = st_t * jnp.exp(bend) + jnp.dot(v.T.astype(BF16), kd.astype(BF16),
                                                      preferred_element_type=F32)

        kt = kk * jnp.exp(r_next - b)
        gdec = jnp.exp(r_next - r)
        ql = q * jnp.exp(b - r)
        levels = [ql]
        for lvl in range(1, nb - 1):
            fac = jnp.concatenate([jnp.zeros((lvl * SB, LANES), F32), gdec[:C - lvl * SB]], axis=0)
            ql = ql * fac
            levels.append(ql)
        qs = jnp.concatenate(levels, axis=0).astype(BF16)
        rl = lax.dot_general(qs, kt.astype(BF16), _NT, preferred_element_type=F32)
        a = jnp.zeros((C, C), F32)
        for lvl in range(1, nb):
            a = a + jnp.where(bdiff == lvl, rl[(lvl - 1) * C:lvl * C, :], 0.0)
        o = o + jnp.dot(a.astype(BF16), v.astype(BF16), preferred_element_type=F32)

        diag = []
        for blk in range(nb):
            bq = b[blk * SB:(blk + 1) * SB]
            qq = q[blk * SB:(blk + 1) * SB]
            acc = jnp.zeros((SB, LANES), F32)
            for s in range(SB):
                rr = blk * SB + s
                dec = jnp.exp(jnp.minimum(bq - b_scr[slot, pl.ds(rr, 1), :], causal_cap[s]))
                a_ts = jnp.sum(qq * k_scr[slot, pl.ds(rr, 1), :] * dec, axis=-1, keepdims=True)
                acc = acc + a_ts * i_ref[pl.ds(off + rr, 1), :][:, lanes]
            diag.append(acc)
        o = o + jnp.concatenate(diag, axis=0)

        y = _rms_rows(o, onorm)
        o_ref[pl.ds(off, C), lanes] = (y * (g * jax.nn.sigmoid(g))).astype(o_ref.dtype)

    def chunk_group(grp, carry):
        for slot in range(HGRN_GROUP):
            for hd in range(heads):
                chunk_step(HGRN_GROUP * grp + slot, slot, hd)
        return carry

    lax.fori_loop(0, tile // (HGRN_GROUP * C), chunk_group, 0)


def _hgrn_mixer_core(proj, lb_logits, o_norm, B, S, layer, *, tile=512, heads=HGRN_HEADS_PER_STEP):
    nt = S // tile
    G = N_HEADS // heads
    w = heads * LANES
    blk = lambda sec: pl.BlockSpec((tile, w), lambda b, g, t, sec=sec: (b * nt + t, sec * G + g))
    return pl.pallas_call(
        functools.partial(_hgrn_kernel, layer=layer, tile=tile, chunk=HGRN_CHUNK, heads=heads),
        grid=(B, G, nt),
        in_specs=[blk(0), blk(1), blk(2), blk(3),
                  pl.BlockSpec((DEPTH, w), lambda b, g, t: (0, g)),
                  pl.BlockSpec((1, w), lambda b, g, t: (0, g))],
        out_specs=pl.BlockSpec((tile, w), lambda b, g, t: (b * nt + t, g)),
        out_shape=jax.ShapeDtypeStruct((B * S, D_MODEL), BF16),
        scratch_shapes=[pltpu.VMEM((heads, LANES, LANES), F32),
                        pltpu.VMEM((heads * HGRN_GROUP, HGRN_CHUNK, LANES), F32),
                        pltpu.VMEM((heads * HGRN_GROUP, HGRN_CHUNK, LANES), F32)],
        compiler_params=_params("parallel", "parallel", "arbitrary"),
        name="hgrn",
    )(proj, proj, proj, proj, lb_logits, o_norm.reshape(1, -1))


SB_CUMSUM_BLOCK = 256


def _sb_kernel(q_ref, k_ref, v_ref, o_ref, *score_refs, t, heads):
    cb = SB_CUMSUM_BLOCK
    qi = pl.program_id(2)
    row = lax.broadcasted_iota(jnp.int32, (t, t), 0)
    col = lax.broadcasted_iota(jnp.int32, (t, t), 1)
    strict = col < row
    jj = lax.broadcasted_iota(jnp.int32, (2 * cb, cb), 0) % cb
    ss = lax.broadcasted_iota(jnp.int32, (2 * cb, cb), 1)
    from_s2 = (jj >= ss).astype(BF16)

    def stream(hd):
        lanes = slice(hd * LANES, (hd + 1) * LANES)
        q = q_ref[:, lanes]

        def scores(ki):
            off = pl.multiple_of(ki * t, t)
            return lax.dot_general(q, k_ref[pl.ds(off, t), lanes], _NT, preferred_element_type=F32)

        def update(ki, zn_ref, acc, diagonal):
            off = pl.multiple_of(ki * t, t)
            tail = None
            new = None
            for j in reversed(range(t // cb)):
                cols = slice(j * cb, (j + 1) * cb)
                zn = zn_ref[:, cols]
                l1m = jnp.minimum(zn, 0.0) - jnp.log(1.0 + jnp.exp2(jnp.abs(zn) * (-LOG2E)))
                if diagonal:
                    l1m = jnp.where(strict[:, cols], l1m, 0.0)
                hi = l1m.astype(BF16)
                lo = (l1m - hi.astype(F32)).astype(BF16)
                x = jnp.dot(jnp.concatenate([hi, lo], axis=1), from_s2, preferred_element_type=F32)
                if tail is not None:
                    x = x + tail
                tail = x[:, 0:1]
                w = jnp.exp(x - zn)
                if diagonal:
                    w = jnp.where(strict[:, cols], w, 0.0)
                part = jnp.dot(w.astype(BF16), v_ref[pl.ds(off + j * cb, cb), lanes],
                               preferred_element_type=F32)
                new = part if new is None else new + part
            return acc * jnp.exp(tail) + new

        return (score_refs[2 * hd], score_refs[2 * hd + 1], scores, update, jnp.zeros((t, HEAD_DIM), F32))

    accs = _causal_sweep(qi, [stream(hd) for hd in range(heads)])
    for hd, acc in enumerate(accs):
        o_ref[:, hd * LANES:(hd + 1) * LANES] = acc.astype(o_ref.dtype)


def _sb_attn(qkv, B, S, *, t=512, heads=ATTN_HEADS_PER_STEP):
    nq = S // t
    G = N_HEADS // heads
    w = heads * LANES
    return pl.pallas_call(
        functools.partial(_sb_kernel, t=t, heads=heads),
        grid=(B, G, nq),
        in_specs=[pl.BlockSpec((t, w), lambda b, g, i: (b * nq + i, g)),
                  pl.BlockSpec((S, w), lambda b, g, i: (b, G + g)),
                  pl.BlockSpec((S, w), lambda b, g, i: (b, 2 * G + g))],
        out_specs=pl.BlockSpec((t, w), lambda b, g, i: (b * nq + i, g)),
        out_shape=jax.ShapeDtypeStruct((B * S, D_MODEL), BF16),
        scratch_shapes=[pltpu.VMEM((t, t), F32) for _ in range(2 * heads)],
        compiler_params=_params("parallel", "parallel", "arbitrary"),
        name="sb_attn",
    )(qkv, qkv, qkv)


MOBA_VETO = 2.0 ** 100


def _moba_kernel(q_ref, k_ref, v_ref, o_ref, kmean_ref, *score_refs, nblk, t, heads):
    bpt = t // MOBA_BLOCK
    qi = pl.program_id(2)

    @pl.when(qi == 0)
    def _():
        for hd in range(heads):
            k_all = k_ref[:, hd * LANES:(hd + 1) * LANES].astype(F32)
            kmean_ref[hd] = jnp.mean(k_all.reshape(nblk, MOBA_BLOCK, LANES), axis=1)

    blk = lax.broadcasted_iota(jnp.int32, (nblk, t), 0)
    own = qi * bpt + lax.broadcasted_iota(jnp.int32, (nblk, t), 1) // MOBA_BLOCK
    neg_inf = jnp.float32(-jnp.inf)
    lane_blk = lax.broadcasted_iota(jnp.int32, (t, LANES), 1)
    key_blk = lax.broadcasted_iota(jnp.int32, (t, LANES), 0) // MOBA_BLOCK

    def stream(hd):
        lanes = slice(hd * LANES, (hd + 1) * LANES)
        q = q_ref[:, lanes]

        gate = lax.dot_general(kmean_ref[hd], q.astype(F32), _NT, precision=lax.Precision.HIGHEST,
                               preferred_element_type=F32)
        gate = jnp.where(blk < own, gate, neg_inf)
        sel = (blk == own).astype(F32)
        for _ in range(MOBA_TOPK):
            mx = jnp.max(gate, axis=0, keepdims=True)
            first = jnp.min(jnp.where(gate == mx, blk, nblk), axis=0, keepdims=True)
            pick = jnp.logical_and(blk == first, mx > neg_inf)
            sel = jnp.where(pick, 1.0, sel)
            gate = jnp.where(pick, neg_inf, gate)

        veto = jnp.concatenate([sel - 1.0, jnp.zeros((LANES - nblk, t), F32)], axis=0)
        q_ext = jnp.concatenate([q, veto.T.astype(BF16)], axis=1)

        def scores(kc):
            off = pl.multiple_of(kc * t, t)
            hot = jnp.where(lane_blk == kc * bpt + key_blk, MOBA_VETO, 0.0).astype(BF16)
            k_ext = jnp.concatenate([k_ref[pl.ds(off, t), lanes], hot], axis=1)
            return lax.dot_general(q_ext, k_ext, _NT, preferred_element_type=F32)

        def update(kc, s_ref, state, diagonal):
            off = pl.multiple_of(kc * t, t)
            v_rows = lambda n: v_ref[pl.ds(off, n), lanes]
            if diagonal:
                return _softmax_update_diag(s_ref, state, v_rows, t)
            return _softmax_update(s_ref[...], state, v_rows(t))

        return (score_refs[2 * hd], score_refs[2 * hd + 1], scores, update, _softmax_init(t))

    states = _causal_sweep(qi, [stream(hd) for hd in range(heads)])
    for hd, state in enumerate(states):
        o_ref[:, hd * LANES:(hd + 1) * LANES] = _softmax_finish(state).astype(o_ref.dtype)


def _moba_attn(qkv, B, S, *, t=2 * MOBA_BLOCK, heads=SOFTMAX_HEADS_PER_STEP):
    assert t == 2 * MOBA_BLOCK
    nq = S // t
    nblk = S // MOBA_BLOCK
    G = N_HEADS // heads
    w = heads * LANES
    return pl.pallas_call(
        functools.partial(_moba_kernel, nblk=nblk, t=t, heads=heads),
        grid=(B, G, nq),
        in_specs=[pl.BlockSpec((t, w), lambda b, g, i: (b * nq + i, g)),
                  pl.BlockSpec((S, w), lambda b, g, i: (b, G + g)),
                  pl.BlockSpec((S, w), lambda b, g, i: (b, 2 * G + g))],
        out_specs=pl.BlockSpec((t, w), lambda b, g, i: (b * nq + i, g)),
        out_shape=jax.ShapeDtypeStruct((B * S, D_MODEL), BF16),
        scratch_shapes=[pltpu.VMEM((heads, nblk, LANES), F32)]
                       + [pltpu.VMEM((t, t), F32) for _ in range(2 * heads)],
        compiler_params=_params("parallel", "parallel", "arbitrary"),
        name="moba_attn",
    )(qkv, qkv, qkv)


def _qkv_proj(h, w_in, layer, q_scale):
    return _proj(h, w_in, BF16, layer=layer, scaled_cols=D_MODEL, scale=q_scale)


def _sb_mixer(h, B, S, w_in, layer=0):
    return _sb_attn(_qkv_proj(h, w_in, layer, -float(HEAD_DIM ** -0.5)), B, S)


def _moba_mixer(h, B, S, w_in, layer=0):
    return _moba_attn(_qkv_proj(h, w_in, layer, float(HEAD_DIM ** -0.5) * LOG2E), B, S)


def kernel(x, mla_w_in, mla_q_norm, mla_kv_norm, mla_w_uq, mla_w_ukv, mla_w_o, hgrn_w_in, hgrn_lb_logits, hgrn_o_norm, hgrn_w_o, sb_w_in, sb_w_o, moba_w_in, moba_w_o, ln_g, ln_b, mlp_w1, mlp_w2):
    B, S, D = x.shape
    assert D == D_MODEL and S % MOBA_BLOCK == 0 and S % 512 == 0
    h = x.reshape(B * S, D)
    hb = h
    mlp_w2_bf16 = mlp_w2.astype(BF16)
    n_mixers = 4
    for i in range(DEPTH):
        kind, slot = i % n_mixers, i // n_mixers
        if kind == 0:
            o = _mla_mixer(hb, B, S, mla_w_in[slot], mla_q_norm[slot], mla_kv_norm[slot],
                           mla_w_uq[slot], mla_w_ukv[slot])
            w_o = mla_w_o[slot]
        elif kind == 1:
            proj = _proj(hb, hgrn_w_in, F32, layer=slot)
            o = _hgrn_mixer_core(proj, hgrn_lb_logits, hgrn_o_norm[slot], B, S, i)
            w_o = hgrn_w_o[slot]
        elif kind == 2:
            o = _sb_mixer(hb, B, S, sb_w_in, slot)
            w_o = sb_w_o[slot]
        else:
            o = _moba_mixer(hb, B, S, moba_w_in, slot)
            w_o = moba_w_o[slot]
        h, hb = _proj_res_ln(o, w_o.astype(BF16), h, ln_g[i, 0], ln_b[i, 0])
        a = _proj(hb, mlp_w1, BF16, layer=i, act="relu2")
        h, hb = _proj_res_ln(a, mlp_w2_bf16, h, ln_g[i, 1], ln_b[i, 1], layer=i)
    return h.reshape(B, S, D)
```

```python
import functools

import jax
import jax.numpy as jnp
from jax import lax
from jax.experimental import pallas as pl
from jax.experimental.pallas import tpu as pltpu

F32 = jnp.float32
BF16 = jnp.bfloat16

D_MODEL = 2048
DEPTH = 4
N_HEADS = 16
HEAD_DIM = 128
MLA_Q_LORA = 512
MLA_KV_LORA = 512
MLA_NOPE = 128
MLA_ROPE = 64
MLA_V = 128
ROPE_THETA = 10000.0
HGRN_CHUNK = 64
HGRN_SUB = 8
HGRN_GROUP = 8
HGRN_HEADS_PER_STEP = 4
ATTN_HEADS_PER_STEP = 4
SOFTMAX_HEADS_PER_STEP = 4
MOBA_BLOCK = 256
MOBA_TOPK = 3
ALPHA = float((2 * DEPTH) ** 0.25)
LN_EPS = 1e-5
RMS_EPS = 1e-6

V7X_VMEM_BYTES = 64 * 1024 * 1024
VMEM_LIMIT = V7X_VMEM_BYTES - 8 * 1024 * 1024
LANES = 128
MASKED = -1e30
LOG2E = 1.4426950408889634

_NT = (((1,), (1,)), ((), ()))


def _params(*sem):
    return pltpu.CompilerParams(dimension_semantics=sem, vmem_limit_bytes=VMEM_LIMIT)


def _layer_norm_rows(y, g, b):
    mu = jnp.mean(y, axis=-1, keepdims=True)
    d = y - mu
    var = jnp.mean(d * d, axis=-1, keepdims=True)
    return d * lax.rsqrt(var + LN_EPS) * g + b


def _rms_rows(x, g):
    return x * lax.rsqrt(jnp.mean(x * x, axis=-1, keepdims=True) + RMS_EPS) * g


def _proj_kernel(x_ref, w_ref, o_ref, wbf_ref, *, act, scaled_tiles, scale):
    @pl.when(pl.program_id(1) == 0)
    def _():
        wbf_ref[...] = w_ref[...].astype(BF16)

    acc = jnp.dot(x_ref[...].astype(BF16), wbf_ref[...], preferred_element_type=F32)
    if act == "relu2":
        r = jnp.maximum(acc, 0.0)
        acc = r * r
    if scaled_tiles:
        acc = acc * jnp.where(pl.program_id(0) < scaled_tiles, scale, 1.0)
    o_ref[...] = acc.astype(o_ref.dtype)


def _proj(x, w, out_dtype, *, layer=0, tm=1024, tn=1024, act=None, scaled_cols=0, scale=1.0):
    M, K = x.shape
    N = w.shape[-1]
    tm, tn = min(tm, M), min(tn, N)
    assert scaled_cols % tn == 0 and M % tm == 0 and N % tn == 0
    if w.ndim == 3:
        w_spec = pl.BlockSpec((None, K, tn), lambda j, i: (layer, 0, j))
    else:
        w_spec = pl.BlockSpec((K, tn), lambda j, i: (0, j))
    return pl.pallas_call(
        functools.partial(_proj_kernel, act=act, scaled_tiles=scaled_cols // tn, scale=scale),
        grid=(N // tn, M // tm),
        in_specs=[pl.BlockSpec((tm, K), lambda j, i: (i, 0)), w_spec],
        out_specs=pl.BlockSpec((tm, tn), lambda j, i: (i, j)),
        out_shape=jax.ShapeDtypeStruct((M, N), out_dtype),
        scratch_shapes=[pltpu.VMEM((K, tn), BF16)],
        compiler_params=_params("parallel", "arbitrary"),
        name="proj",
    )(x, w)


def _proj_res_ln_kernel(x_ref, w_ref, h_ref, g_ref, b_ref, o_ref, ob_ref, acc_ref, *, nk):
    k = pl.program_id(1)
    def part():
        return jnp.dot(x_ref[...], w_ref[...], preferred_element_type=F32)

    def finish(y):
        out = _layer_norm_rows(ALPHA * h_ref[...] + y, g_ref[...], b_ref[...])
        o_ref[...] = out
        ob_ref[...] = out.astype(BF16)

    if nk == 1:
        finish(part())
    else:
        @pl.when(k == 0)
        def _():
            acc_ref[...] = part()

        @pl.when(jnp.logical_and(k > 0, k < nk - 1))
        def _():
            acc_ref[...] += part()

        @pl.when(k == nk - 1)
        def _():
            finish(acc_ref[...] + part())


def _proj_res_ln(x, w, h, g, b, *, layer=0, tm=512, tk=2048):
    M, K = x.shape
    N = w.shape[-1]
    nk = K // tk
    if w.ndim == 3:
        w_spec = pl.BlockSpec((None, tk, N), lambda i, k: (layer, k, 0))
    else:
        w_spec = pl.BlockSpec((tk, N), lambda i, k: (k, 0))
    return pl.pallas_call(
        functools.partial(_proj_res_ln_kernel, nk=nk),
        grid=(M // tm, nk),
        in_specs=[pl.BlockSpec((tm, tk), lambda i, k: (i, k)),
                  w_spec,
                  pl.BlockSpec((tm, N), lambda i, k: (i, 0)),
                  pl.BlockSpec((1, N), lambda i, k: (0, 0)),
                  pl.BlockSpec((1, N), lambda i, k: (0, 0))],
        out_specs=[pl.BlockSpec((tm, N), lambda i, k: (i, 0)),
                   pl.BlockSpec((tm, N), lambda i, k: (i, 0))],
        out_shape=[jax.ShapeDtypeStruct((M, N), F32), jax.ShapeDtypeStruct((M, N), BF16)],
        scratch_shapes=[pltpu.VMEM((tm, N), F32)],
        compiler_params=_params("parallel", "arbitrary"),
        name="proj_res_ln",
    )(x, w, h, g.reshape(1, N), b.reshape(1, N))


def _mla_in_kernel(h_ref, w_ref, qg_ref, kvg_ref, ct_ref, st_ref, cq_ref, ckv_ref, kr_ref):
    acc = jnp.dot(h_ref[...].astype(BF16), w_ref[...], preferred_element_type=F32)
    ql, kvl = MLA_Q_LORA, MLA_KV_LORA
    cq_ref[...] = _rms_rows(acc[:, :ql], qg_ref[...]).astype(cq_ref.dtype)
    ckv_ref[...] = _rms_rows(acc[:, ql:ql + kvl], kvg_ref[...]).astype(ckv_ref.dtype)
    a = acc[:, ql + kvl:ql + kvl + LANES]
    a_sw = acc[:, ql + kvl + LANES:]
    kr_ref[...] = (a * ct_ref[...] + a_sw * st_ref[...]).astype(kr_ref.dtype)


def _mla_in(h, w_ext, qg, kvg, ct, st, S, *, tm=512):
    M, K = h.shape
    N = w_ext.shape[1]
    ns = S // tm
    return pl.pallas_call(
        _mla_in_kernel,
        grid=(M // tm,),
        in_specs=[pl.BlockSpec((tm, K), lambda i: (i, 0)),
                  pl.BlockSpec((K, N), lambda i: (0, 0)),
                  pl.BlockSpec((1, MLA_Q_LORA), lambda i: (0, 0)),
                  pl.BlockSpec((1, MLA_KV_LORA), lambda i: (0, 0)),
                  pl.BlockSpec((tm, LANES), lambda i: (i % ns, 0)),
                  pl.BlockSpec((tm, LANES), lambda i: (i % ns, 0))],
        out_specs=[pl.BlockSpec((tm, MLA_Q_LORA), lambda i: (i, 0)),
                   pl.BlockSpec((tm, MLA_KV_LORA), lambda i: (i, 0)),
                   pl.BlockSpec((tm, LANES), lambda i: (i, 0))],
        out_shape=[jax.ShapeDtypeStruct((M, MLA_Q_LORA), BF16),
                   jax.ShapeDtypeStruct((M, MLA_KV_LORA), BF16),
                   jax.ShapeDtypeStruct((M, LANES), BF16)],
        compiler_params=_params("parallel"),
        name="mla_in",
    )(h, w_ext, qg.reshape(1, -1), kvg.reshape(1, -1), ct, st)


def _mla_uq_kernel(cq_ref, wm_ref, ws_ref, ct_ref, st_ref, q_ref, *, heads, scale):
    x = cq_ref[...]
    a = jnp.dot(x, wm_ref[...], preferred_element_type=F32)
    a_sw = jnp.dot(x, ws_ref[...], preferred_element_type=F32)
    ct = ct_ref[...]
    st = st_ref[...]
    for hh in range(heads):
        lo = hh * 2 * LANES
        q_ref[:, lo:lo + LANES] = (a[:, lo:lo + LANES] * scale).astype(q_ref.dtype)
        rot = a[:, lo + LANES:lo + 2 * LANES] * ct + a_sw[:, hh * LANES:(hh + 1) * LANES] * st
        q_ref[:, lo + LANES:lo + 2 * LANES] = (rot * scale).astype(q_ref.dtype)


def _mla_uq(cq, w_main, w_sw, ct, st, S, scale, *, tm=1024, heads_per_step=4):
    M, K = cq.shape
    hp = heads_per_step
    tm = min(tm, S)
    ns = S // tm
    return pl.pallas_call(
        functools.partial(_mla_uq_kernel, heads=hp, scale=scale),
        grid=(M // tm, N_HEADS // hp),
        in_specs=[pl.BlockSpec((tm, K), lambda i, j: (i, 0)),
                  pl.BlockSpec((K, hp * 2 * LANES), lambda i, j: (0, j)),
                  pl.BlockSpec((K, hp * LANES), lambda i, j: (0, j)),
                  pl.BlockSpec((tm, LANES), lambda i, j: (i % ns, 0)),
                  pl.BlockSpec((tm, LANES), lambda i, j: (i % ns, 0))],
        out_specs=pl.BlockSpec((tm, hp * 2 * LANES), lambda i, j: (i, j)),
        out_shape=jax.ShapeDtypeStruct((M, N_HEADS * 2 * LANES), BF16),
        compiler_params=_params("parallel", "parallel"),
        name="mla_uq",
    )(cq, w_main, w_sw, ct, st)


def _causal_sweep(qi, streams):
    def fill(which, ki):
        for stream in streams:
            stream[which][...] = stream[2](ki)

    def step(which, ki, states, diagonal):
        return tuple(stream[3](ki, stream[which], st, diagonal) for stream, st in zip(streams, states))

    fill(0, 0)

    def pair(p, states):
        k0 = 2 * p
        fill(1, k0 + 1)
        states = step(0, k0, states, False)
        fill(0, k0 + 2)
        return step(1, k0 + 1, states, False)

    states = lax.fori_loop(0, qi // 2, pair, tuple(stream[4] for stream in streams))

    def odd(states):
        fill(1, qi)
        states = step(0, qi - 1, states, False)
        return step(1, qi, states, True)

    def even(states):
        return step(0, qi, states, True)

    return lax.cond(qi % 2 == 1, odd, even, states)


def _softmax_update(s, state, v):
    m, acc = state
    m_new = jnp.maximum(m, jnp.max(s, axis=-1, keepdims=True))
    alpha = jnp.exp2(m - m_new)
    p = jnp.exp2(s - m_new).astype(BF16)
    v_ones = jnp.concatenate([v, jnp.ones_like(v)], axis=1)
    acc = alpha * acc + jnp.dot(p, v_ones, preferred_element_type=F32)
    return m_new, acc


def _softmax_update_diag(s_ref, state, v_rows, t):
    h = t // 2
    m, acc = state
    keep = (lax.broadcasted_iota(jnp.int32, (h, h), 1) <= lax.broadcasted_iota(jnp.int32, (h, h), 0))
    s_top = jnp.where(keep, s_ref[0:h, 0:h], -jnp.inf)
    top = _softmax_update(s_top, (m[:h], acc[:h]), v_rows(h))
    s_bot = jnp.concatenate([s_ref[h:t, 0:h], jnp.where(keep, s_ref[h:t, h:t], -jnp.inf)], axis=1)
    bot = _softmax_update(s_bot, (m[h:], acc[h:]), v_rows(t))
    return jnp.concatenate([top[0], bot[0]], axis=0), jnp.concatenate([top[1], bot[1]], axis=0)


def _softmax_init(t):
    return (jnp.full((t, 1), MASKED, F32), jnp.zeros((t, 2 * LANES), F32))


def _softmax_finish(state):
    _, acc = state
    return acc[:, :LANES] / acc[:, LANES:]


def _mla_attn_kernel(q_ref, kn_ref, kr_ref, v_ref, o_ref, *score_refs, t, heads):
    qi = pl.program_id(2)

    def stream(hd):
        lanes = slice(hd * LANES, (hd + 1) * LANES)
        q = q_ref[:, 2 * hd * LANES:2 * (hd + 1) * LANES]

        def scores(ki):
            off = pl.multiple_of(ki * t, t)
            k = jnp.concatenate([kn_ref[pl.ds(off, t), lanes], kr_ref[pl.ds(off, t), :]], axis=1)
            return lax.dot_general(q, k, _NT, preferred_element_type=F32)

        def update(ki, s_ref, state, diagonal):
            off = pl.multiple_of(ki * t, t)
            v_rows = lambda n: v_ref[pl.ds(off, n), lanes]
            if diagonal:
                return _softmax_update_diag(s_ref, state, v_rows, t)
            return _softmax_update(s_ref[...], state, v_rows(t))

        return (score_refs[2 * hd], score_refs[2 * hd + 1], scores, update, _softmax_init(t))

    states = _causal_sweep(qi, [stream(hd) for hd in range(heads)])
    for hd, state in enumerate(states):
        o_ref[:, hd * LANES:(hd + 1) * LANES] = _softmax_finish(state).astype(o_ref.dtype)


def _mla_attn(q, kv, kr, B, S, *, t=512, heads=SOFTMAX_HEADS_PER_STEP):
    nq = S // t
    G = N_HEADS // heads
    w = heads * LANES
    return pl.pallas_call(
        functools.partial(_mla_attn_kernel, t=t, heads=heads),
        grid=(B, G, nq),
        in_specs=[pl.BlockSpec((t, 2 * w), lambda b, g, i: (b * nq + i, g)),
                  pl.BlockSpec((S, w), lambda b, g, i: (b, g)),
                  pl.BlockSpec((S, LANES), lambda b, g, i: (b, 0)),
                  pl.BlockSpec((S, w), lambda b, g, i: (b, G + g))],
        out_specs=pl.BlockSpec((t, w), lambda b, g, i: (b * nq + i, g)),
        out_shape=jax.ShapeDtypeStruct((B * S, N_HEADS * MLA_V), BF16),
        scratch_shapes=[pltpu.VMEM((t, t), F32) for _ in range(2 * heads)],
        compiler_params=_params("parallel", "parallel", "arbitrary"),
        name="mla_attn",
    )(q, kv, kr, kv)


def _rope_tables(S):
    half = MLA_ROPE // 2
    inv = 1.0 / (ROPE_THETA ** (jnp.arange(0, MLA_ROPE, 2, dtype=F32) / MLA_ROPE))
    ang = jnp.arange(S, dtype=F32)[:, None] * inv[None, :]
    cos, sin = jnp.cos(ang), jnp.sin(ang)
    zeros = jnp.zeros((S, LANES - 2 * half), F32)
    return (jnp.concatenate([cos, cos, zeros], axis=1), jnp.concatenate([-sin, sin, zeros], axis=1))


def _mla_weights(w_in, w_uq, w_ukv):
    D = w_in.shape[0]
    half = MLA_ROPE // 2
    base = MLA_Q_LORA + MLA_KV_LORA
    x1, x2 = w_in[:, base:base + half], w_in[:, base + half:base + 2 * half]
    pad = jnp.zeros((D, LANES - 2 * half), w_in.dtype)
    w_in_ext = jnp.concatenate([w_in[:, :base], x1, x2, pad, x2, x1, pad], axis=1).astype(BF16)

    wq = w_uq.reshape(MLA_Q_LORA, N_HEADS, MLA_NOPE + MLA_ROPE)
    nope, r1, r2 = wq[..., :MLA_NOPE], wq[..., MLA_NOPE:MLA_NOPE + half], wq[..., MLA_NOPE + half:]
    padq = jnp.zeros((MLA_Q_LORA, N_HEADS, LANES - 2 * half), w_uq.dtype)
    w_main = jnp.concatenate([nope, r1, r2, padq], axis=-1).reshape(MLA_Q_LORA, N_HEADS * 2 * LANES).astype(BF16)
    w_sw = jnp.concatenate([r2, r1, padq], axis=-1).reshape(MLA_Q_LORA, N_HEADS * LANES).astype(BF16)

    wkv = w_ukv.reshape(MLA_KV_LORA, N_HEADS, MLA_NOPE + MLA_V)
    w_kv = jnp.concatenate([wkv[..., :MLA_NOPE].reshape(MLA_KV_LORA, -1),
                            wkv[..., MLA_NOPE:].reshape(MLA_KV_LORA, -1)], axis=1).astype(BF16)
    return w_in_ext, w_main, w_sw, w_kv


def _mla_mixer(h, B, S, w_in, q_norm, kv_norm, w_uq, w_ukv):
    w_in_ext, w_main, w_sw, w_kv = _mla_weights(w_in, w_uq, w_ukv)
    ct, st = _rope_tables(S)
    cq, ckv, kr = _mla_in(h, w_in_ext, q_norm, kv_norm, ct, st, S)
    scale = float((MLA_NOPE + MLA_ROPE) ** -0.5) * LOG2E
    q = _mla_uq(cq, w_main, w_sw, ct, st, S, scale)
    kv = _proj(ckv, w_kv, BF16)
    return _mla_attn(q, kv, kr, B, S)


def _hgrn_kernel(q_ref, f_ref, i_ref, g_ref, lbl_ref, on_ref, o_ref, state_ref, b_scr, k_scr,
                 *, layer, tile, chunk, heads):
    C, SB = chunk, HGRN_SUB
    nb = C // SB

    @pl.when(pl.program_id(2) == 0)
    def _():
        state_ref[...] = jnp.zeros_like(state_ref)

    lg = lbl_ref[...]
    e = jnp.exp(lg - jnp.max(lg, axis=0, keepdims=True))
    p = e / jnp.sum(e, axis=0, keepdims=True)
    cs = p[0:1]
    for r in range(1, layer + 1):
        cs = cs + p[r:r + 1]
    lb_all = cs - p[0:1]
    log_lb_all = jnp.log(lb_all)
    log1m_lb_all = jnp.log1p(-lb_all)
    one_m_lb_all = 1.0 - lb_all
    onorm_all = on_ref[...]

    row = lax.broadcasted_iota(jnp.int32, (C, 3 * C), 0)
    col = lax.broadcasted_iota(jnp.int32, (C, 3 * C), 1) % C
    blk0 = (row // SB) * SB
    tri = jnp.concatenate([col <= row, col <= blk0, col <= jnp.minimum(blk0 + SB, C - 1)],
                          axis=0).astype(BF16)
    brow = lax.broadcasted_iota(jnp.int32, (C, C), 0) // SB
    bcol = lax.broadcasted_iota(jnp.int32, (C, C), 1) // SB
    bdiff = brow - bcol
    sub = lax.broadcasted_iota(jnp.int32, (SB, LANES), 0)
    causal_cap = [jnp.where(sub >= s, 0.0, -jnp.inf).astype(F32) for s in range(SB)]

    def chunk_step(c, slot, hd):
        lanes = slice(hd * LANES, (hd + 1) * LANES)
        log_lb, log1m_lb, one_m_lb = log_lb_all[:, lanes], log1m_lb_all[:, lanes], one_m_lb_all[:, lanes]
        onorm = onorm_all[:, lanes]
        slot = hd * HGRN_GROUP + slot
        off = pl.multiple_of(c * C, C)
        q = q_ref[pl.ds(off, C), lanes]
        fp = f_ref[pl.ds(off, C), lanes]
        v = i_ref[pl.ds(off, C), lanes]
        g = g_ref[pl.ds(off, C), lanes]

        ls = jnp.minimum(fp, 0.0) - jnp.log(1.0 + jnp.exp(-jnp.abs(fp)))
        cc = log1m_lb + ls
        lf = jnp.maximum(log_lb, cc) + jnp.log(1.0 + jnp.exp(-jnp.abs(log_lb - cc)))
        kk = one_m_lb * jax.nn.sigmoid(-fp)

        p1 = lf.astype(BF16)
        r1 = lf - p1.astype(F32)
        p2 = r1.astype(BF16)
        p3 = (r1 - p2.astype(F32)).astype(BF16)
        cums = jnp.dot(tri, jnp.concatenate([p1, p2, p3], axis=0), preferred_element_type=F32)
        b, r, r_next = cums[:C], cums[C:2 * C], cums[2 * C:]
        bend = b[C - 1:C, :]
        b_scr[slot] = b
        k_scr[slot] = kk

        st_t = state_ref[hd]
        qe = q * jnp.exp(b)
        o = lax.dot_general(qe.astype(BF16), st_t.astype(BF16), _NT, preferred_element_type=F32)
        kd = kk * jnp.exp(bend - b)
        state_ref[hd] = st_t * jnp.exp(bend) + jnp.dot(v.T.astype(BF16), kd.astype(BF16),
                                                      preferred_element_type=F32)

        kt = kk * jnp.exp(r_next - b)
        gdec = jnp.exp(r_next - r)
        ql = q * jnp.exp(b - r)
        levels = [ql]
        for lvl in range(1, nb - 1):
            fac = jnp.concatenate([jnp.zeros((lvl * SB, LANES), F32), gdec[:C - lvl * SB]], axis=0)
            ql = ql * fac
            levels.append(ql)
        qs = jnp.concatenate(levels, axis=0).astype(BF16)
        rl = lax.dot_general(qs, kt.astype(BF16), _NT, preferred_element_type=F32)
        a = jnp.zeros((C, C), F32)
        for lvl in range(1, nb):
            a = a + jnp.where(bdiff == lvl, rl[(lvl - 1) * C:lvl * C, :], 0.0)
        o = o + jnp.dot(a.astype(BF16), v.astype(BF16), preferred_element_type=F32)

        diag = []
        for blk in range(nb):
            bq = b[blk * SB:(blk + 1) * SB]
            qq = q[blk * SB:(blk + 1) * SB]
            acc = jnp.zeros((SB, LANES), F32)
            for s in range(SB):
                rr = blk * SB + s
                dec = jnp.exp(jnp.minimum(bq - b_scr[slot, pl.ds(rr, 1), :], causal_cap[s]))
                a_ts = jnp.sum(qq * k_scr[slot, pl.ds(rr, 1), :] * dec, axis=-1, keepdims=True)
                acc = acc + a_ts * i_ref[pl.ds(off + rr, 1), :][:, lanes]
            diag.append(acc)
        o = o + jnp.concatenate(diag, axis=0)

        y = _rms_rows(o, onorm)
        o_ref[pl.ds(off, C), lanes] = (y * (g * jax.nn.sigmoid(g))).astype(o_ref.dtype)

    def chunk_group(grp, carry):
        for slot in range(HGRN_GROUP):
            for hd in range(heads):
                chunk_step(HGRN_GROUP * grp + slot, slot, hd)
        return carry

    lax.fori_loop(0, tile // (HGRN_GROUP * C), chunk_group, 0)


def _hgrn_mixer_core(proj, lb_logits, o_norm, B, S, layer, *, tile=512, heads=HGRN_HEADS_PER_STEP):
    nt = S // tile
    G = N_HEADS // heads
    w = heads * LANES
    blk = lambda sec: pl.BlockSpec((tile, w), lambda b, g, t, sec=sec: (b * nt + t, sec * G + g))
    return pl.pallas_call(
        functools.partial(_hgrn_kernel, layer=layer, tile=tile, chunk=HGRN_CHUNK, heads=heads),
        grid=(B, G, nt),
        in_specs=[blk(0), blk(1), blk(2), blk(3),
                  pl.BlockSpec((DEPTH, w), lambda b, g, t: (0, g)),
                  pl.BlockSpec((1, w), lambda b, g, t: (0, g))],
        out_specs=pl.BlockSpec((tile, w), lambda b, g, t: (b * nt + t, g)),
        out_shape=jax.ShapeDtypeStruct((B * S, D_MODEL), BF16),
        scratch_shapes=[pltpu.VMEM((heads, LANES, LANES), F32),
                        pltpu.VMEM((heads * HGRN_GROUP, HGRN_CHUNK, LANES), F32),
                        pltpu.VMEM((heads * HGRN_GROUP, HGRN_CHUNK, LANES), F32)],
        compiler_params=_params("parallel", "parallel", "arbitrary"),
        name="hgrn",
    )(proj, proj, proj, proj, lb_logits, o_norm.reshape(1, -1))


SB_CUMSUM_BLOCK = 256


def _sb_kernel(q_ref, k_ref, v_ref, o_ref, *score_refs, t, heads):
    cb = SB_CUMSUM_BLOCK
    qi = pl.program_id(2)
    row = lax.broadcasted_iota(jnp.int32, (t, t), 0)
    col = lax.broadcasted_iota(jnp.int32, (t, t), 1)
    strict = col < row
    jj = lax.broadcasted_iota(jnp.int32, (2 * cb, cb), 0) % cb
    ss = lax.broadcasted_iota(jnp.int32, (2 * cb, cb), 1)
    from_s2 = (jj >= ss).astype(BF16)

    def stream(hd):
        lanes = slice(hd * LANES, (hd + 1) * LANES)
        q = q_ref[:, lanes]

        def scores(ki):
            off = pl.multiple_of(ki * t, t)
            return lax.dot_general(q, k_ref[pl.ds(off, t), lanes], _NT, preferred_element_type=F32)

        def update(ki, zn_ref, acc, diagonal):
            off = pl.multiple_of(ki * t, t)
            tail = None
            new = None
            for j in reversed(range(t // cb)):
                cols = slice(j * cb, (j + 1) * cb)
                zn = zn_ref[:, cols]
                l1m = jnp.minimum(zn, 0.0) - jnp.log(1.0 + jnp.exp2(jnp.abs(zn) * (-LOG2E)))
                if diagonal:
                    l1m = jnp.where(strict[:, cols], l1m, 0.0)
                hi = l1m.astype(BF16)
                lo = (l1m - hi.astype(F32)).astype(BF16)
                x = jnp.dot(jnp.concatenate([hi, lo], axis=1), from_s2, preferred_element_type=F32)
                if tail is not None:
                    x = x + tail
                tail = x[:, 0:1]
                w = jnp.exp(x - zn)
                if diagonal:
                    w = jnp.where(strict[:, cols], w, 0.0)
                part = jnp.dot(w.astype(BF16), v_ref[pl.ds(off + j * cb, cb), lanes],
                               preferred_element_type=F32)
                new = part if new is None else new + part
            return acc * jnp.exp(tail) + new

        return (score_refs[2 * hd], score_refs[2 * hd + 1], scores, update, jnp.zeros((t, HEAD_DIM), F32))

    accs = _causal_sweep(qi, [stream(hd) for hd in range(heads)])
    for hd, acc in enumerate(accs):
        o_ref[:, hd * LANES:(hd + 1) * LANES] = acc.astype(o_ref.dtype)


def _sb_attn(qkv, B, S, *, t=512, heads=ATTN_HEADS_PER_STEP):
    nq = S // t
    G = N_HEADS // heads
    w = heads * LANES
    return pl.pallas_call(
        functools.partial(_sb_kernel, t=t, heads=heads),
        grid=(B, G, nq),
        in_specs=[pl.BlockSpec((t, w), lambda b, g, i: (b * nq + i, g)),
                  pl.BlockSpec((S, w), lambda b, g, i: (b, G + g)),
                  pl.BlockSpec((S, w), lambda b, g, i: (b, 2 * G + g))],
        out_specs=pl.BlockSpec((t, w), lambda b, g, i: (b * nq + i, g)),
        out_shape=jax.ShapeDtypeStruct((B * S, D_MODEL), BF16),
        scratch_shapes=[pltpu.VMEM((t, t), F32) for _ in range(2 * heads)],
        compiler_params=_params("parallel", "parallel", "arbitrary"),
        name="sb_attn",
    )(qkv, qkv, qkv)


MOBA_VETO = 2.0 ** 100


def _moba_kernel(q_ref, k_ref, v_ref, o_ref, kmean_ref, *score_refs, nblk, t, heads):
    bpt = t // MOBA_BLOCK
    qi = pl.program_id(2)

    @pl.when(qi == 0)
    def _():
        for hd in range(heads):
            k_all = k_ref[:, hd * LANES:(hd + 1) * LANES].astype(F32)
            kmean_ref[hd] = jnp.mean(k_all.reshape(nblk, MOBA_BLOCK, LANES), axis=1)

    blk = lax.broadcasted_iota(jnp.int32, (nblk, t), 0)
    own = qi * bpt + lax.broadcasted_iota(jnp.int32, (nblk, t), 1) // MOBA_BLOCK
    neg_inf = jnp.float32(-jnp.inf)
    lane_blk = lax.broadcasted_iota(jnp.int32, (t, LANES), 1)
    key_blk = lax.broadcasted_iota(jnp.int32, (t, LANES), 0) // MOBA_BLOCK

    def stream(hd):
        lanes = slice(hd * LANES, (hd + 1) * LANES)
        q = q_ref[:, lanes]

        gate = lax.dot_general(kmean_ref[hd], q.astype(F32), _NT, precision=lax.Precision.HIGHEST,
                               preferred_element_type=F32)
        gate = jnp.where(blk < own, gate, neg_inf)
        sel = (blk == own).astype(F32)
        for _ in range(MOBA_TOPK):
            mx = jnp.max(gate, axis=0, keepdims=True)
            first = jnp.min(jnp.where(gate == mx, blk, nblk), axis=0, keepdims=True)
            pick = jnp.logical_and(blk == first, mx > neg_inf)
            sel = jnp.where(pick, 1.0, sel)
            gate = jnp.where(pick, neg_inf, gate)

        veto = jnp.concatenate([sel - 1.0, jnp.zeros((LANES - nblk, t), F32)], axis=0)
        q_ext = jnp.concatenate([q, veto.T.astype(BF16)], axis=1)

        def scores(kc):
            off = pl.multiple_of(kc * t, t)
            hot = jnp.where(lane_blk == kc * bpt + key_blk, MOBA_VETO, 0.0).astype(BF16)
            k_ext = jnp.concatenate([k_ref[pl.ds(off, t), lanes], hot], axis=1)
            return lax.dot_general(q_ext, k_ext, _NT, preferred_element_type=F32)

        def update(kc, s_ref, state, diagonal):
            off = pl.multiple_of(kc * t, t)
            v_rows = lambda n: v_ref[pl.ds(off, n), lanes]
            if diagonal:
                return _softmax_update_diag(s_ref, state, v_rows, t)
            return _softmax_update(s_ref[...], state, v_rows(t))

        return (score_refs[2 * hd], score_refs[2 * hd + 1], scores, update, _softmax_init(t))

    states = _causal_sweep(qi, [stream(hd) for hd in range(heads)])
    for hd, state in enumerate(states):
        o_ref[:, hd * LANES:(hd + 1) * LANES] = _softmax_finish(state).astype(o_ref.dtype)


def _moba_attn(qkv, B, S, *, t=2 * MOBA_BLOCK, heads=SOFTMAX_HEADS_PER_STEP):
    assert t == 2 * MOBA_BLOCK
    nq = S // t
    nblk = S // MOBA_BLOCK
    G = N_HEADS // heads
    w = heads * LANES
    return pl.pallas_call(
        functools.partial(_moba_kernel, nblk=nblk, t=t, heads=heads),
        grid=(B, G, nq),
        in_specs=[pl.BlockSpec((t, w), lambda b, g, i: (b * nq + i, g)),
                  pl.BlockSpec((S, w), lambda b, g, i: (b, G + g)),
                  pl.BlockSpec((S, w), lambda b, g, i: (b, 2 * G + g))],
        out_specs=pl.BlockSpec((t, w), lambda b, g, i: (b * nq + i, g)),
        out_shape=jax.ShapeDtypeStruct((B * S, D_MODEL), BF16),
        scratch_shapes=[pltpu.VMEM((heads, nblk, LANES), F32)]
                       + [pltpu.VMEM((t, t), F32) for _ in range(2 * heads)],
        compiler_params=_params("parallel", "parallel", "arbitrary"),
        name="moba_attn",
    )(qkv, qkv, qkv)


def _qkv_proj(h, w_in, layer, q_scale):
    return _proj(h, w_in, BF16, layer=layer, scaled_cols=D_MODEL, scale=q_scale)


def _sb_mixer(h, B, S, w_in, layer=0):
    return _sb_attn(_qkv_proj(h, w_in, layer, -float(HEAD_DIM ** -0.5)), B, S)


def _moba_mixer(h, B, S, w_in, layer=0):
    return _moba_attn(_qkv_proj(h, w_in, layer, float(HEAD_DIM ** -0.5) * LOG2E), B, S)


def kernel(x, mla_w_in, mla_q_norm, mla_kv_norm, mla_w_uq, mla_w_ukv, mla_w_o, hgrn_w_in, hgrn_lb_logits, hgrn_o_norm, hgrn_w_o, sb_w_in, sb_w_o, moba_w_in, moba_w_o, ln_g, ln_b, mlp_w1, mlp_w2):
    B, S, D = x.shape
    assert D == D_MODEL and S % MOBA_BLOCK == 0 and S % 512 == 0
    h = x.reshape(B * S, D)
    hb = h
    mlp_w2_bf16 = mlp_w2.astype(BF16)
    n_mixers = 4
    for i in range(DEPTH):
        kind, slot = i % n_mixers, i // n_mixers
        if kind == 0:
            o = _mla_mixer(hb, B, S, mla_w_in[slot], mla_q_norm[slot], mla_kv_norm[slot],
                           mla_w_uq[slot], mla_w_ukv[slot])
            w_o = mla_w_o[slot]
        elif kind == 1:
            proj = _proj(hb, hgrn_w_in, F32, layer=slot)
            o = _hgrn_mixer_core(proj, hgrn_lb_logits, hgrn_o_norm[slot], B, S, i)
            w_o = hgrn_w_o[slot]
        elif kind == 2:
            o = _sb_mixer(hb, B, S, sb_w_in, slot)
            w_o = sb_w_o[slot]
        else:
            o = _moba_mixer(hb, B, S, moba_w_in, slot)
            w_o = moba_w_o[slot]
        h, hb = _proj_res_ln(o, w_o.astype(BF16), h, ln_g[i, 0], ln_b[i, 0])
        a = _proj(hb, mlp_w1, BF16, layer=i, act="relu2")
        h, hb = _proj_res_ln(a, mlp_w2_bf16, h, ln_g[i, 1], ln_b[i, 1], layer=i)
    return h.reshape(B, S, D)
```

```python
import functools

import jax
import jax.numpy as jnp
from jax import lax
from jax.experimental import pallas as pl
from jax.experimental.pallas import tpu as pltpu

F32 = jnp.float32
BF16 = jnp.bfloat16

D_MODEL = 2048
DEPTH = 4
N_HEADS = 16
HEAD_DIM = 128
MLA_Q_LORA = 512
MLA_KV_LORA = 512
MLA_NOPE = 128
MLA_ROPE = 64
MLA_V = 128
ROPE_THETA = 10000.0
HGRN_CHUNK = 64
HGRN_SUB = 8
HGRN_GROUP = 8
HGRN_HEADS_PER_STEP = 4
SB_HEADS_PER_STEP = 4
SOFTMAX_HEADS_PER_STEP = 4
MOBA_BLOCK = 256
MOBA_TOPK = 3
ALPHA = float((2 * DEPTH) ** 0.25)
LN_EPS = 1e-5
RMS_EPS = 1e-6

V7X_VMEM_BYTES = 64 * 1024 * 1024
VMEM_LIMIT = V7X_VMEM_BYTES - 8 * 1024 * 1024
PROJ_VMEM_BUDGET = VMEM_LIMIT - 8 * 1024 * 1024
LANES = 128
MASKED = -1e30
LOG2E = 1.4426950408889634

_NT = (((1,), (1,)), ((), ()))


def _params(*sem):
    return pltpu.CompilerParams(dimension_semantics=sem, vmem_limit_bytes=VMEM_LIMIT)


def _layer_norm_rows(y, g, b):
    mu = jnp.mean(y, axis=-1, keepdims=True)
    d = y - mu
    var = jnp.mean(d * d, axis=-1, keepdims=True)
    return d * lax.rsqrt(var + LN_EPS) * g + b


def _rms_rows(x, g):
    return x * lax.rsqrt(jnp.mean(x * x, axis=-1, keepdims=True) + RMS_EPS) * g


def _proj_kernel(x_ref, w_ref, o_ref, wbf_ref, *, act, scaled_tiles, scale):
    @pl.when(pl.program_id(1) == 0)
    def _():
        wbf_ref[...] = w_ref[...].astype(BF16)

    acc = jnp.dot(x_ref[...].astype(BF16), wbf_ref[...], preferred_element_type=F32)
    if act == "relu2":
        r = jnp.maximum(acc, 0.0)
        acc = r * r
    if scaled_tiles:
        acc = acc * jnp.where(pl.program_id(0) < scaled_tiles, scale, 1.0)
    o_ref[...] = acc.astype(o_ref.dtype)


def _proj_row_tile(M, K, tn, x_dtype, w_dtype, out_dtype):
    size = lambda dt: jnp.dtype(dt).itemsize
    fixed = 2 * K * tn * size(w_dtype) + K * tn * size(BF16)
    for tm in (2048, 1024, 512, 256):
        if M % tm == 0 and fixed + 2 * tm * (K * size(x_dtype) + tn * size(out_dtype)) <= PROJ_VMEM_BUDGET:
            return tm
    raise ValueError("no row tile fits")


def _proj(x, w, out_dtype, *, layer=0, tn=1024, act=None, scaled_cols=0, scale=1.0):
    M, K = x.shape
    N = w.shape[-1]
    tn = min(tn, N)
    tm = _proj_row_tile(M, K, tn, x.dtype, w.dtype, out_dtype)
    assert scaled_cols % tn == 0 and N % tn == 0
    if w.ndim == 3:
        w_spec = pl.BlockSpec((None, K, tn), lambda j, i: (layer, 0, j))
    else:
        w_spec = pl.BlockSpec((K, tn), lambda j, i: (0, j))
    return pl.pallas_call(
        functools.partial(_proj_kernel, act=act, scaled_tiles=scaled_cols // tn, scale=scale),
        grid=(N // tn, M // tm),
        in_specs=[pl.BlockSpec((tm, K), lambda j, i: (i, 0)), w_spec],
        out_specs=pl.BlockSpec((tm, tn), lambda j, i: (i, j)),
        out_shape=jax.ShapeDtypeStruct((M, N), out_dtype),
        scratch_shapes=[pltpu.VMEM((K, tn), BF16)],
        compiler_params=_params("parallel", "arbitrary"),
        name="proj",
    )(x, w)


def _proj_res_ln_kernel(x_ref, w_ref, h_ref, g_ref, b_ref, o_ref, ob_ref, acc_ref, *, nk):
    k = pl.program_id(1)
    def part():
        return jnp.dot(x_ref[...], w_ref[...], preferred_element_type=F32)

    def finish(y):
        out = _layer_norm_rows(ALPHA * h_ref[...] + y, g_ref[...], b_ref[...])
        o_ref[...] = out
        ob_ref[...] = out.astype(BF16)

    if nk == 1:
        finish(part())
    else:
        @pl.when(k == 0)
        def _():
            acc_ref[...] = part()

        @pl.when(jnp.logical_and(k > 0, k < nk - 1))
        def _():
            acc_ref[...] += part()

        @pl.when(k == nk - 1)
        def _():
            finish(acc_ref[...] + part())


def _proj_res_ln(x, w, h, g, b, *, layer=0, tm=512, tk=2048):
    M, K = x.shape
    N = w.shape[-1]
    nk = K // tk
    if w.ndim == 3:
        w_spec = pl.BlockSpec((None, tk, N), lambda i, k: (layer, k, 0))
    else:
        w_spec = pl.BlockSpec((tk, N), lambda i, k: (k, 0))
    return pl.pallas_call(
        functools.partial(_proj_res_ln_kernel, nk=nk),
        grid=(M // tm, nk),
        in_specs=[pl.BlockSpec((tm, tk), lambda i, k: (i, k)),
                  w_spec,
                  pl.BlockSpec((tm, N), lambda i, k: (i, 0)),
                  pl.BlockSpec((1, N), lambda i, k: (0, 0)),
                  pl.BlockSpec((1, N), lambda i, k: (0, 0))],
        out_specs=[pl.BlockSpec((tm, N), lambda i, k: (i, 0)),
                   pl.BlockSpec((tm, N), lambda i, k: (i, 0))],
        out_shape=[jax.ShapeDtypeStruct((M, N), F32), jax.ShapeDtypeStruct((M, N), BF16)],
        scratch_shapes=[pltpu.VMEM((tm, N), F32)],
        compiler_params=_params("parallel", "arbitrary"),
        name="proj_res_ln",
    )(x, w, h, g.reshape(1, N), b.reshape(1, N))


def _mla_in_kernel(h_ref, w_ref, qg_ref, kvg_ref, ct_ref, st_ref, cq_ref, ckv_ref, kr_ref):
    acc = jnp.dot(h_ref[...].astype(BF16), w_ref[...], preferred_element_type=F32)
    ql, kvl = MLA_Q_LORA, MLA_KV_LORA
    cq_ref[...] = _rms_rows(acc[:, :ql], qg_ref[...]).astype(cq_ref.dtype)
    ckv_ref[...] = _rms_rows(acc[:, ql:ql + kvl], kvg_ref[...]).astype(ckv_ref.dtype)
    a = acc[:, ql + kvl:ql + kvl + LANES]
    a_sw = acc[:, ql + kvl + LANES:]
    kr_ref[...] = (a * ct_ref[...] + a_sw * st_ref[...]).astype(kr_ref.dtype)


def _mla_in(h, w_ext, qg, kvg, ct, st, S, *, tm=1024):
    M, K = h.shape
    N = w_ext.shape[1]
    tm = min(tm, S)
    ns = S // tm
    return pl.pallas_call(
        _mla_in_kernel,
        grid=(M // tm,),
        in_specs=[pl.BlockSpec((tm, K), lambda i: (i, 0)),
                  pl.BlockSpec((K, N), lambda i: (0, 0)),
                  pl.BlockSpec((1, MLA_Q_LORA), lambda i: (0, 0)),
                  pl.BlockSpec((1, MLA_KV_LORA), lambda i: (0, 0)),
                  pl.BlockSpec((tm, LANES), lambda i: (i % ns, 0)),
                  pl.BlockSpec((tm, LANES), lambda i: (i % ns, 0))],
        out_specs=[pl.BlockSpec((tm, MLA_Q_LORA), lambda i: (i, 0)),
                   pl.BlockSpec((tm, MLA_KV_LORA), lambda i: (i, 0)),
                   pl.BlockSpec((tm, LANES), lambda i: (i, 0))],
        out_shape=[jax.ShapeDtypeStruct((M, MLA_Q_LORA), BF16),
                   jax.ShapeDtypeStruct((M, MLA_KV_LORA), BF16),
                   jax.ShapeDtypeStruct((M, LANES), BF16)],
        compiler_params=_params("parallel"),
        name="mla_in",
    )(h, w_ext, qg.reshape(1, -1), kvg.reshape(1, -1), ct, st)


def _mla_uq_kernel(cq_ref, wm_ref, ws_ref, ct_ref, st_ref, q_ref, *, heads, scale):
    x = cq_ref[...]
    a = jnp.dot(x, wm_ref[...], preferred_element_type=F32)
    a_sw = jnp.dot(x, ws_ref[...], preferred_element_type=F32)
    ct = ct_ref[...]
    st = st_ref[...]
    for hh in range(heads):
        lo = hh * 2 * LANES
        q_ref[:, lo:lo + LANES] = (a[:, lo:lo + LANES] * scale).astype(q_ref.dtype)
        rot = a[:, lo + LANES:lo + 2 * LANES] * ct + a_sw[:, hh * LANES:(hh + 1) * LANES] * st
        q_ref[:, lo + LANES:lo + 2 * LANES] = (rot * scale).astype(q_ref.dtype)


def _mla_uq(cq, w_main, w_sw, ct, st, S, scale, *, tm=1024, heads_per_step=4):
    M, K = cq.shape
    hp = heads_per_step
    tm = min(tm, S)
    ns = S // tm
    return pl.pallas_call(
        functools.partial(_mla_uq_kernel, heads=hp, scale=scale),
        grid=(M // tm, N_HEADS // hp),
        in_specs=[pl.BlockSpec((tm, K), lambda i, j: (i, 0)),
                  pl.BlockSpec((K, hp * 2 * LANES), lambda i, j: (0, j)),
                  pl.BlockSpec((K, hp * LANES), lambda i, j: (0, j)),
                  pl.BlockSpec((tm, LANES), lambda i, j: (i % ns, 0)),
                  pl.BlockSpec((tm, LANES), lambda i, j: (i % ns, 0))],
        out_specs=pl.BlockSpec((tm, hp * 2 * LANES), lambda i, j: (i, j)),
        out_shape=jax.ShapeDtypeStruct((M, N_HEADS * 2 * LANES), BF16),
        compiler_params=_params("parallel", "parallel"),
        name="mla_uq",
    )(cq, w_main, w_sw, ct, st)


def _causal_sweep(qi, streams):
    def fill(which, ki):
        for stream in streams:
            stream[which][...] = stream[2](ki)

    def step(which, ki, states, diagonal):
        return tuple(stream[3](ki, stream[which], st, diagonal) for stream, st in zip(streams, states))

    fill(0, 0)

    def pair(p, states):
        k0 = 2 * p
        fill(1, k0 + 1)
        states = step(0, k0, states, False)
        fill(0, k0 + 2)
        return step(1, k0 + 1, states, False)

    states = lax.fori_loop(0, qi // 2, pair, tuple(stream[4] for stream in streams))

    def odd(states):
        fill(1, qi)
        states = step(0, qi - 1, states, False)
        return step(1, qi, states, True)

    def even(states):
        return step(0, qi, states, True)

    return lax.cond(qi % 2 == 1, odd, even, states)


def _softmax_update(s, state, v):
    m, acc = state
    m_new = jnp.maximum(m, jnp.max(s, axis=-1, keepdims=True))
    alpha = jnp.exp2(m - m_new)
    p = jnp.exp2(s - m_new).astype(BF16)
    v_ones = jnp.concatenate([v, jnp.ones_like(v)], axis=1)
    acc = alpha * acc + jnp.dot(p, v_ones, preferred_element_type=F32)
    return m_new, acc


def _softmax_update_diag(s_ref, state, v_rows, t):
    h = t // 2
    m, acc = state
    keep = (lax.broadcasted_iota(jnp.int32, (h, h), 1) <= lax.broadcasted_iota(jnp.int32, (h, h), 0))
    s_top = jnp.where(keep, s_ref[0:h, 0:h], -jnp.inf)
    top = _softmax_update(s_top, (m[:h], acc[:h]), v_rows(h))
    s_bot = jnp.concatenate([s_ref[h:t, 0:h], jnp.where(keep, s_ref[h:t, h:t], -jnp.inf)], axis=1)
    bot = _softmax_update(s_bot, (m[h:], acc[h:]), v_rows(t))
    return jnp.concatenate([top[0], bot[0]], axis=0), jnp.concatenate([top[1], bot[1]], axis=0)


def _softmax_init(t):
    return (jnp.full((t, 1), MASKED, F32), jnp.zeros((t, 2 * LANES), F32))


def _softmax_finish(state):
    _, acc = state
    return acc[:, :LANES] / acc[:, LANES:]


def _mla_attn_kernel(q_ref, kn_ref, kr_ref, v_ref, o_ref, *score_refs, t, heads):
    qi = pl.program_id(2)

    def stream(hd):
        lanes = slice(hd * LANES, (hd + 1) * LANES)
        q = q_ref[:, 2 * hd * LANES:2 * (hd + 1) * LANES]

        def scores(ki):
            off = pl.multiple_of(ki * t, t)
            k = jnp.concatenate([kn_ref[pl.ds(off, t), lanes], kr_ref[pl.ds(off, t), :]], axis=1)
            return lax.dot_general(q, k, _NT, preferred_element_type=F32)

        def update(ki, s_ref, state, diagonal):
            off = pl.multiple_of(ki * t, t)
            v_rows = lambda n: v_ref[pl.ds(off, n), lanes]
            if diagonal:
                return _softmax_update_diag(s_ref, state, v_rows, t)
            return _softmax_update(s_ref[...], state, v_rows(t))

        return (score_refs[2 * hd], score_refs[2 * hd + 1], scores, update, _softmax_init(t))

    states = _causal_sweep(qi, [stream(hd) for hd in range(heads)])
    for hd, state in enumerate(states):
        o_ref[:, hd * LANES:(hd + 1) * LANES] = _softmax_finish(state).astype(o_ref.dtype)


def _mla_attn(q, kv, kr, B, S, *, t=512, heads=SOFTMAX_HEADS_PER_STEP):
    nq = S // t
    G = N_HEADS // heads
    w = heads * LANES
    return pl.pallas_call(
        functools.partial(_mla_attn_kernel, t=t, heads=heads),
        grid=(B, G, nq),
        in_specs=[pl.BlockSpec((t, 2 * w), lambda b, g, i: (b * nq + i, g)),
                  pl.BlockSpec((S, w), lambda b, g, i: (b, g)),
                  pl.BlockSpec((S, LANES), lambda b, g, i: (b, 0)),
                  pl.BlockSpec((S, w), lambda b, g, i: (b, G + g))],
        out_specs=pl.BlockSpec((t, w), lambda b, g, i: (b * nq + i, g)),
        out_shape=jax.ShapeDtypeStruct((B * S, N_HEADS * MLA_V), BF16),
        scratch_shapes=[pltpu.VMEM((t, t), F32) for _ in range(2 * heads)],
        compiler_params=_params("parallel", "parallel", "arbitrary"),
        name="mla_attn",
    )(q, kv, kr, kv)


def _rope_tables(S):
    half = MLA_ROPE // 2
    inv = 1.0 / (ROPE_THETA ** (jnp.arange(0, MLA_ROPE, 2, dtype=F32) / MLA_ROPE))
    ang = jnp.arange(S, dtype=F32)[:, None] * inv[None, :]
    cos, sin = jnp.cos(ang), jnp.sin(ang)
    zeros = jnp.zeros((S, LANES - 2 * half), F32)
    return (jnp.concatenate([cos, cos, zeros], axis=1), jnp.concatenate([-sin, sin, zeros], axis=1))


def _mla_weights(w_in, w_uq, w_ukv):
    D = w_in.shape[0]
    half = MLA_ROPE // 2
    base = MLA_Q_LORA + MLA_KV_LORA
    x1, x2 = w_in[:, base:base + half], w_in[:, base + half:base + 2 * half]
    pad = jnp.zeros((D, LANES - 2 * half), w_in.dtype)
    w_in_ext = jnp.concatenate([w_in[:, :base], x1, x2, pad, x2, x1, pad], axis=1).astype(BF16)

    wq = w_uq.reshape(MLA_Q_LORA, N_HEADS, MLA_NOPE + MLA_ROPE)
    nope, r1, r2 = wq[..., :MLA_NOPE], wq[..., MLA_NOPE:MLA_NOPE + half], wq[..., MLA_NOPE + half:]
    padq = jnp.zeros((MLA_Q_LORA, N_HEADS, LANES - 2 * half), w_uq.dtype)
    w_main = jnp.concatenate([nope, r1, r2, padq], axis=-1).reshape(MLA_Q_LORA, N_HEADS * 2 * LANES).astype(BF16)
    w_sw = jnp.concatenate([r2, r1, padq], axis=-1).reshape(MLA_Q_LORA, N_HEADS * LANES).astype(BF16)

    wkv = w_ukv.reshape(MLA_KV_LORA, N_HEADS, MLA_NOPE + MLA_V)
    w_kv = jnp.concatenate([wkv[..., :MLA_NOPE].reshape(MLA_KV_LORA, -1),
                            wkv[..., MLA_NOPE:].reshape(MLA_KV_LORA, -1)], axis=1).astype(BF16)
    return w_in_ext, w_main, w_sw, w_kv


def _mla_mixer(h, B, S, w_in, q_norm, kv_norm, w_uq, w_ukv):
    w_in_ext, w_main, w_sw, w_kv = _mla_weights(w_in, w_uq, w_ukv)
    ct, st = _rope_tables(S)
    cq, ckv, kr = _mla_in(h, w_in_ext, q_norm, kv_norm, ct, st, S)
    scale = float((MLA_NOPE + MLA_ROPE) ** -0.5) * LOG2E
    q = _mla_uq(cq, w_main, w_sw, ct, st, S, scale)
    kv = _proj(ckv, w_kv, BF16)
    return _mla_attn(q, kv, kr, B, S)


def _hgrn_kernel(q_ref, f_ref, i_ref, g_ref, lbl_ref, on_ref, o_ref, state_ref, b_scr, k_scr,
                 *, layer, tile, chunk, heads):
    C, SB = chunk, HGRN_SUB
    nb = C // SB

    @pl.when(pl.program_id(2) == 0)
    def _():
        state_ref[...] = jnp.zeros_like(state_ref)

    lg = lbl_ref[...]
    e = jnp.exp(lg - jnp.max(lg, axis=0, keepdims=True))
    p = e / jnp.sum(e, axis=0, keepdims=True)
    cs = p[0:1]
    for r in range(1, layer + 1):
        cs = cs + p[r:r + 1]
    lb_all = cs - p[0:1]
    log_lb_all = jnp.log(lb_all)
    log1m_lb_all = jnp.log1p(-lb_all)
    one_m_lb_all = 1.0 - lb_all
    onorm_all = on_ref[...]

    row = lax.broadcasted_iota(jnp.int32, (C, 3 * C), 0)
    col = lax.broadcasted_iota(jnp.int32, (C, 3 * C), 1) % C
    blk0 = (row // SB) * SB
    tri = jnp.concatenate([col <= row, col <= blk0, col <= jnp.minimum(blk0 + SB, C - 1)],
                          axis=0).astype(BF16)
    brow = lax.broadcasted_iota(jnp.int32, (C, C), 0) // SB
    bcol = lax.broadcasted_iota(jnp.int32, (C, C), 1) // SB
    bdiff = brow - bcol
    sub = lax.broadcasted_iota(jnp.int32, (SB, LANES), 0)
    causal_cap = [jnp.where(sub >= s, 0.0, -jnp.inf).astype(F32) for s in range(SB)]

    def chunk_step(c, slot, hd):
        lanes = slice(hd * LANES, (hd + 1) * LANES)
        log_lb, log1m_lb, one_m_lb = log_lb_all[:, lanes], log1m_lb_all[:, lanes], one_m_lb_all[:, lanes]
        onorm = onorm_all[:, lanes]
        slot = hd * HGRN_GROUP + slot
        off = pl.multiple_of(c * C, C)
        q = q_ref[pl.ds(off, C), lanes]
        fp = f_ref[pl.ds(off, C), lanes]
        v = i_ref[pl.ds(off, C), lanes]
        g = g_ref[pl.ds(off, C), lanes]

        ls = jnp.minimum(fp, 0.0) - jnp.log(1.0 + jnp.exp(-jnp.abs(fp)))
        cc = log1m_lb + ls
        lf = jnp.maximum(log_lb, cc) + jnp.log(1.0 + jnp.exp(-jnp.abs(log_lb - cc)))
        kk = one_m_lb * jax.nn.sigmoid(-fp)

        p1 = lf.astype(BF16)
        r1 = lf - p1.astype(F32)
        p2 = r1.astype(BF16)
        p3 = (r1 - p2.astype(F32)).astype(BF16)
        cums = jnp.dot(tri, jnp.concatenate([p1, p2, p3], axis=0), preferred_element_type=F32)
        b, r, r_next = cums[:C], cums[C:2 * C], cums[2 * C:]
        bend = b[C - 1:C, :]
        b_scr[slot] = b
        k_scr[slot] = kk

        st_t = state_ref[hd]
        qe = q * jnp.exp(b)
        o = lax.dot_general(qe.astype(BF16), st_t.astype(BF16), _NT, preferred_element_type=F32)
        kd = kk * jnp.exp(bend - b)
        state_ref[hd] = st_t * jnp.exp(bend) + jnp.dot(v.T.astype(BF16), kd.astype(BF16),
                                                      preferred_element_type=F32)

        kt = kk * jnp.exp(r_next - b)
        gdec = jnp.exp(r_next - r)
        ql = q * jnp.exp(b - r)
        levels = [ql]
        for lvl in range(1, nb - 1):
            fac = jnp.concatenate([jnp.zeros((lvl * SB, LANES), F32), gdec[:C - lvl * SB]], axis=0)
            ql = ql * fac
            levels.append(ql)
        qs = jnp.concatenate(levels, axis=0).astype(BF16)
        rl = lax.dot_general(qs, kt.astype(BF16), _NT, preferred_element_type=F32)
        a = jnp.zeros((C, C), F32)
        for lvl in range(1, nb):
            a = a + jnp.where(bdiff == lvl, rl[(lvl - 1) * C:lvl * C, :], 0.0)
        o = o + jnp.dot(a.astype(BF16), v.astype(BF16), preferred_element_type=F32)

        diag = []
        for blk in range(nb):
            bq = b[blk * SB:(blk + 1) * SB]
            qq = q[blk * SB:(blk + 1) * SB]
            acc = jnp.zeros((SB, LANES), F32)
            for s in range(SB):
                rr = blk * SB + s
                dec = jnp.exp(jnp.minimum(bq - b_scr[slot, pl.ds(rr, 1), :], causal_cap[s]))
                a_ts = jnp.sum(qq * k_scr[slot, pl.ds(rr, 1), :] * dec, axis=-1, keepdims=True)
                acc = acc + a_ts * i_ref[pl.ds(off + rr, 1), :][:, lanes]
            diag.append(acc)
        o = o + jnp.concatenate(diag, axis=0)

        y = _rms_rows(o, onorm)
        o_ref[pl.ds(off, C), lanes] = (y * (g * jax.nn.sigmoid(g))).astype(o_ref.dtype)

    def chunk_group(grp, carry):
        for slot in range(HGRN_GROUP):
            for hd in range(heads):
                chunk_step(HGRN_GROUP * grp + slot, slot, hd)
        return carry

    lax.fori_loop(0, tile // (HGRN_GROUP * C), chunk_group, 0)


def _hgrn_mixer_core(proj, lb_logits, o_norm, B, S, layer, *, tile=512, heads=HGRN_HEADS_PER_STEP):
    nt = S // tile
    G = N_HEADS // heads
    w = heads * LANES
    blk = lambda sec: pl.BlockSpec((tile, w), lambda b, g, t, sec=sec: (b * nt + t, sec * G + g))
    return pl.pallas_call(
        functools.partial(_hgrn_kernel, layer=layer, tile=tile, chunk=HGRN_CHUNK, heads=heads),
        grid=(B, G, nt),
        in_specs=[blk(0), blk(1), blk(2), blk(3),
                  pl.BlockSpec((DEPTH, w), lambda b, g, t: (0, g)),
                  pl.BlockSpec((1, w), lambda b, g, t: (0, g))],
        out_specs=pl.BlockSpec((tile, w), lambda b, g, t: (b * nt + t, g)),
        out_shape=jax.ShapeDtypeStruct((B * S, D_MODEL), BF16),
        scratch_shapes=[pltpu.VMEM((heads, LANES, LANES), F32),
                        pltpu.VMEM((heads * HGRN_GROUP, HGRN_CHUNK, LANES), F32),
                        pltpu.VMEM((heads * HGRN_GROUP, HGRN_CHUNK, LANES), F32)],
        compiler_params=_params("parallel", "parallel", "arbitrary"),
        name="hgrn",
    )(proj, proj, proj, proj, lb_logits, o_norm.reshape(1, -1))


SB_CUMSUM_BLOCK = 256


def _sb_kernel(q_ref, k_ref, v_ref, o_ref, *score_refs, t, heads):
    cb = SB_CUMSUM_BLOCK
    qi = pl.program_id(2)
    row = lax.broadcasted_iota(jnp.int32, (t, t), 0)
    col = lax.broadcasted_iota(jnp.int32, (t, t), 1)
    strict = col < row
    jj = lax.broadcasted_iota(jnp.int32, (2 * cb, cb), 0) % cb
    ss = lax.broadcasted_iota(jnp.int32, (2 * cb, cb), 1)
    from_s2 = (jj >= ss).astype(BF16)

    def stream(hd):
        lanes = slice(hd * LANES, (hd + 1) * LANES)
        q = q_ref[:, lanes]

        def scores(ki):
            off = pl.multiple_of(ki * t, t)
            return lax.dot_general(q, k_ref[pl.ds(off, t), lanes], _NT, preferred_element_type=F32)

        def update(ki, zn_ref, acc, diagonal):
            off = pl.multiple_of(ki * t, t)
            tail = None
            new = None
            for j in reversed(range(t // cb)):
                cols = slice(j * cb, (j + 1) * cb)
                zn = zn_ref[:, cols]
                l1m = jnp.minimum(zn, 0.0) - jnp.log(1.0 + jnp.exp2(jnp.abs(zn) * (-LOG2E)))
                if diagonal:
                    l1m = jnp.where(strict[:, cols], l1m, 0.0)
                hi = l1m.astype(BF16)
                lo = (l1m - hi.astype(F32)).astype(BF16)
                x = jnp.dot(jnp.concatenate([hi, lo], axis=1), from_s2, preferred_element_type=F32)
                if tail is not None:
                    x = x + tail
                tail = x[:, 0:1]
                w = jnp.exp(x - zn)
                if diagonal:
                    w = jnp.where(strict[:, cols], w, 0.0)
                part = jnp.dot(w.astype(BF16), v_ref[pl.ds(off + j * cb, cb), lanes],
                               preferred_element_type=F32)
                new = part if new is None else new + part
            return acc * jnp.exp(tail) + new

        return (score_refs[2 * hd], score_refs[2 * hd + 1], scores, update, jnp.zeros((t, HEAD_DIM), F32))

    accs = _causal_sweep(qi, [stream(hd) for hd in range(heads)])
    for hd, acc in enumerate(accs):
        o_ref[:, hd * LANES:(hd + 1) * LANES] = acc.astype(o_ref.dtype)


def _sb_attn(qkv, B, S, *, t=512, heads=SB_HEADS_PER_STEP):
    nq = S // t
    G = N_HEADS // heads
    w = heads * LANES
    return pl.pallas_call(
        functools.partial(_sb_kernel, t=t, heads=heads),
        grid=(B, G, nq),
        in_specs=[pl.BlockSpec((t, w), lambda b, g, i: (b * nq + i, g)),
                  pl.BlockSpec((S, w), lambda b, g, i: (b, G + g)),
                  pl.BlockSpec((S, w), lambda b, g, i: (b, 2 * G + g))],
        out_specs=pl.BlockSpec((t, w), lambda b, g, i: (b * nq + i, g)),
        out_shape=jax.ShapeDtypeStruct((B * S, D_MODEL), BF16),
        scratch_shapes=[pltpu.VMEM((t, t), F32) for _ in range(2 * heads)],
        compiler_params=_params("parallel", "parallel", "arbitrary"),
        name="sb_attn",
    )(qkv, qkv, qkv)


MOBA_VETO = 2.0 ** 100


def _moba_kernel(q_ref, k_ref, v_ref, o_ref, kmean_ref, *score_refs, nblk, t, heads):
    bpt = t // MOBA_BLOCK
    qi = pl.program_id(2)

    @pl.when(qi == 0)
    def _():
        for hd in range(heads):
            k_all = k_ref[:, hd * LANES:(hd + 1) * LANES].astype(F32)
            kmean_ref[hd] = jnp.mean(k_all.reshape(nblk, MOBA_BLOCK, LANES), axis=1)

    blk = lax.broadcasted_iota(jnp.int32, (nblk, t), 0)
    own = qi * bpt + lax.broadcasted_iota(jnp.int32, (nblk, t), 1) // MOBA_BLOCK
    neg_inf = jnp.float32(-jnp.inf)
    lane_blk = lax.broadcasted_iota(jnp.int32, (t, LANES), 1)
    key_blk = lax.broadcasted_iota(jnp.int32, (t, LANES), 0) // MOBA_BLOCK

    def stream(hd):
        lanes = slice(hd * LANES, (hd + 1) * LANES)
        q = q_ref[:, lanes]

        gate = lax.dot_general(kmean_ref[hd], q.astype(F32), _NT, precision=lax.Precision.HIGHEST,
                               preferred_element_type=F32)
        gate = jnp.where(blk < own, gate, neg_inf)
        sel = (blk == own).astype(F32)
        for _ in range(MOBA_TOPK):
            mx = jnp.max(gate, axis=0, keepdims=True)
            first = jnp.min(jnp.where(gate == mx, blk, nblk), axis=0, keepdims=True)
            pick = jnp.logical_and(blk == first, mx > neg_inf)
            sel = jnp.where(pick, 1.0, sel)
            gate = jnp.where(pick, neg_inf, gate)

        veto = jnp.concatenate([sel - 1.0, jnp.zeros((LANES - nblk, t), F32)], axis=0)
        q_ext = jnp.concatenate([q, veto.T.astype(BF16)], axis=1)

        def scores(kc):
            off = pl.multiple_of(kc * t, t)
            hot = jnp.where(lane_blk == kc * bpt + key_blk, MOBA_VETO, 0.0).astype(BF16)
            k_ext = jnp.concatenate([k_ref[pl.ds(off, t), lanes], hot], axis=1)
            return lax.dot_general(q_ext, k_ext, _NT, preferred_element_type=F32)

        def update(kc, s_ref, state, diagonal):
            off = pl.multiple_of(kc * t, t)
            v_rows = lambda n: v_ref[pl.ds(off, n), lanes]
            if diagonal:
                return _softmax_update_diag(s_ref, state, v_rows, t)
            return _softmax_update(s_ref[...], state, v_rows(t))

        return (score_refs[2 * hd], score_refs[2 * hd + 1], scores, update, _softmax_init(t))

    states = _causal_sweep(qi, [stream(hd) for hd in range(heads)])
    for hd, state in enumerate(states):
        o_ref[:, hd * LANES:(hd + 1) * LANES] = _softmax_finish(state).astype(o_ref.dtype)


def _moba_attn(qkv, B, S, *, t=2 * MOBA_BLOCK, heads=SOFTMAX_HEADS_PER_STEP):
    assert t == 2 * MOBA_BLOCK
    nq = S // t
    nblk = S // MOBA_BLOCK
    G = N_HEADS // heads
    w = heads * LANES
    return pl.pallas_call(
        functools.partial(_moba_kernel, nblk=nblk, t=t, heads=heads),
        grid=(B, G, nq),
        in_specs=[pl.BlockSpec((t, w), lambda b, g, i: (b * nq + i, g)),
                  pl.BlockSpec((S, w), lambda b, g, i: (b, G + g)),
                  pl.BlockSpec((S, w), lambda b, g, i: (b, 2 * G + g))],
        out_specs=pl.BlockSpec((t, w), lambda b, g, i: (b * nq + i, g)),
        out_shape=jax.ShapeDtypeStruct((B * S, D_MODEL), BF16),
        scratch_shapes=[pltpu.VMEM((heads, nblk, LANES), F32)]
                       + [pltpu.VMEM((t, t), F32) for _ in range(2 * heads)],
        compiler_params=_params("parallel", "parallel", "arbitrary"),
        name="moba_attn",
    )(qkv, qkv, qkv)


def _qkv_proj(h, w_in, layer, q_scale):
    return _proj(h, w_in, BF16, layer=layer, scaled_cols=D_MODEL, scale=q_scale)


def _sb_mixer(h, B, S, w_in, layer=0):
    return _sb_attn(_qkv_proj(h, w_in, layer, -float(HEAD_DIM ** -0.5)), B, S)


def _moba_mixer(h, B, S, w_in, layer=0):
    return _moba_attn(_qkv_proj(h, w_in, layer, float(HEAD_DIM ** -0.5) * LOG2E), B, S)


def kernel(x, mla_w_in, mla_q_norm, mla_kv_norm, mla_w_uq, mla_w_ukv, mla_w_o, hgrn_w_in, hgrn_lb_logits, hgrn_o_norm, hgrn_w_o, sb_w_in, sb_w_o, moba_w_in, moba_w_o, ln_g, ln_b, mlp_w1, mlp_w2):
    B, S, D = x.shape
    assert D == D_MODEL and S % MOBA_BLOCK == 0 and S % 512 == 0
    h = x.reshape(B * S, D)
    hb = h
    mlp_w2_bf16 = mlp_w2.astype(BF16)
    n_mixers = 4
    for i in range(DEPTH):
        kind, slot = i % n_mixers, i // n_mixers
        if kind == 0:
            o = _mla_mixer(hb, B, S, mla_w_in[slot], mla_q_norm[slot], mla_kv_norm[slot],
                           mla_w_uq[slot], mla_w_ukv[slot])
            w_o = mla_w_o[slot]
        elif kind == 1:
            proj = _proj(hb, hgrn_w_in, F32, layer=slot)
            o = _hgrn_mixer_core(proj, hgrn_lb_logits, hgrn_o_norm[slot], B, S, i)
            w_o = hgrn_w_o[slot]
        elif kind == 2:
            o = _sb_mixer(hb, B, S, sb_w_in, slot)
            w_o = sb_w_o[slot]
        else:
            o = _moba_mixer(hb, B, S, moba_w_in, slot)
            w_o = moba_w_o[slot]
        h, hb = _proj_res_ln(o, w_o.astype(BF16), h, ln_g[i, 0], ln_b[i, 0])
        a = _proj(hb, mlp_w1, BF16, layer=i, act="relu2")
        h, hb = _proj_res_ln(a, mlp_w2_bf16, h, ln_g[i, 1], ln_b[i, 1], layer=i)
    return h.reshape(B, S, D)
```

```python
import functools

import jax
import jax.numpy as jnp
from jax import lax
from jax.experimental import pallas as pl
from jax.experimental.pallas import tpu as pltpu

F32 = jnp.float32
BF16 = jnp.bfloat16

D_MODEL = 2048
DEPTH = 4
N_HEADS = 16
HEAD_DIM = 128
MLA_Q_LORA = 512
MLA_KV_LORA = 512
MLA_NOPE = 128
MLA_ROPE = 64
MLA_V = 128
ROPE_THETA = 10000.0
HGRN_CHUNK = 64
HGRN_SUB = 8
HGRN_GROUP = 8
HGRN_HEADS_PER_STEP = 4
SB_HEADS_PER_STEP = 4
SOFTMAX_HEADS_PER_STEP = 4
MOBA_BLOCK = 256
MOBA_TOPK = 3
ALPHA = float((2 * DEPTH) ** 0.25)
LN_EPS = 1e-5
RMS_EPS = 1e-6

V7X_VMEM_BYTES = 64 * 1024 * 1024
VMEM_LIMIT = V7X_VMEM_BYTES - 8 * 1024 * 1024
PROJ_VMEM_BUDGET = VMEM_LIMIT - 8 * 1024 * 1024
LANES = 128
MASKED = -1e30
LOG2E = 1.4426950408889634

_NT = (((1,), (1,)), ((), ()))


def _params(*sem):
    return pltpu.CompilerParams(dimension_semantics=sem, vmem_limit_bytes=VMEM_LIMIT)


def _layer_norm_rows(y, g, b):
    mu = jnp.mean(y, axis=-1, keepdims=True)
    d = y - mu
    var = jnp.mean(d * d, axis=-1, keepdims=True)
    return d * lax.rsqrt(var + LN_EPS) * g + b


def _rms_rows(x, g):
    return x * lax.rsqrt(jnp.mean(x * x, axis=-1, keepdims=True) + RMS_EPS) * g


def _proj_kernel(x_ref, w_ref, o_ref, wbf_ref, *, act, scaled_tiles, scale):
    @pl.when(pl.program_id(1) == 0)
    def _():
        wbf_ref[...] = w_ref[...].astype(BF16)

    acc = jnp.dot(x_ref[...].astype(BF16), wbf_ref[...], preferred_element_type=F32)
    if act == "relu2":
        r = jnp.maximum(acc, 0.0)
        acc = r * r
    if scaled_tiles:
        acc = acc * jnp.where(pl.program_id(0) < scaled_tiles, scale, 1.0)
    o_ref[...] = acc.astype(o_ref.dtype)


def _proj_row_tile(M, K, tn, x_dtype, w_dtype, out_dtype):
    size = lambda dt: jnp.dtype(dt).itemsize
    fixed = 2 * K * tn * size(w_dtype) + K * tn * size(BF16)
    for tm in (2048, 1024, 512, 256):
        if M % tm == 0 and fixed + 2 * tm * (K * size(x_dtype) + tn * size(out_dtype)) <= PROJ_VMEM_BUDGET:
            return tm
    raise ValueError("no row tile fits")


def _proj(x, w, out_dtype, *, layer=0, tn=1024, act=None, scaled_cols=0, scale=1.0):
    M, K = x.shape
    N = w.shape[-1]
    tn = min(tn, N)
    tm = _proj_row_tile(M, K, tn, x.dtype, w.dtype, out_dtype)
    assert scaled_cols % tn == 0 and N % tn == 0
    if w.ndim == 3:
        w_spec = pl.BlockSpec((None, K, tn), lambda j, i: (layer, 0, j))
    else:
        w_spec = pl.BlockSpec((K, tn), lambda j, i: (0, j))
    return pl.pallas_call(
        functools.partial(_proj_kernel, act=act, scaled_tiles=scaled_cols // tn, scale=scale),
        grid=(N // tn, M // tm),
        in_specs=[pl.BlockSpec((tm, K), lambda j, i: (i, 0)), w_spec],
        out_specs=pl.BlockSpec((tm, tn), lambda j, i: (i, j)),
        out_shape=jax.ShapeDtypeStruct((M, N), out_dtype),
        scratch_shapes=[pltpu.VMEM((K, tn), BF16)],
        compiler_params=_params("parallel", "arbitrary"),
        name="proj",
    )(x, w)


def _proj_res_ln_kernel(x_ref, w_ref, h_ref, g_ref, b_ref, o_ref, ob_ref, *, nk):
    k = pl.program_id(1)

    def part():
        return jnp.dot(x_ref[...], w_ref[...], preferred_element_type=F32)

    def finish(y):
        out = _layer_norm_rows(ALPHA * h_ref[...] + y, g_ref[...], b_ref[...])
        o_ref[...] = out
        ob_ref[...] = out.astype(BF16)

    if nk == 1:
        finish(part())
    else:
        @pl.when(k == 0)
        def _():
            o_ref[...] = part()

        @pl.when(jnp.logical_and(k > 0, k < nk - 1))
        def _():
            o_ref[...] += part()

        @pl.when(k == nk - 1)
        def _():
            finish(o_ref[...] + part())


def _proj_res_ln(x, w, h, g, b, *, layer=0, tm=1024, tk=512):
    M, K = x.shape
    N = w.shape[-1]
    tm, tk = min(tm, M), min(tk, K)
    nk = K // tk
    if w.ndim == 3:
        w_spec = pl.BlockSpec((None, tk, N), lambda i, k: (layer, k, 0))
    else:
        w_spec = pl.BlockSpec((tk, N), lambda i, k: (k, 0))
    return pl.pallas_call(
        functools.partial(_proj_res_ln_kernel, nk=nk),
        grid=(M // tm, nk),
        in_specs=[pl.BlockSpec((tm, tk), lambda i, k: (i, k)),
                  w_spec,
                  pl.BlockSpec((tm, N), lambda i, k: (i, 0)),
                  pl.BlockSpec((1, N), lambda i, k: (0, 0)),
                  pl.BlockSpec((1, N), lambda i, k: (0, 0))],
        out_specs=[pl.BlockSpec((tm, N), lambda i, k: (i, 0)),
                   pl.BlockSpec((tm, N), lambda i, k: (i, 0))],
        out_shape=[jax.ShapeDtypeStruct((M, N), F32), jax.ShapeDtypeStruct((M, N), BF16)],
        compiler_params=_params("parallel", "arbitrary"),
        name="proj_res_ln",
    )(x, w, h, g.reshape(1, N), b.reshape(1, N))


def _mla_in_kernel(h_ref, w_ref, qg_ref, kvg_ref, ct_ref, st_ref, cq_ref, ckv_ref, kr_ref):
    acc = jnp.dot(h_ref[...].astype(BF16), w_ref[...], preferred_element_type=F32)
    ql, kvl = MLA_Q_LORA, MLA_KV_LORA
    cq_ref[...] = _rms_rows(acc[:, :ql], qg_ref[...]).astype(cq_ref.dtype)
    ckv_ref[...] = _rms_rows(acc[:, ql:ql + kvl], kvg_ref[...]).astype(ckv_ref.dtype)
    a = acc[:, ql + kvl:ql + kvl + LANES]
    a_sw = acc[:, ql + kvl + LANES:]
    kr_ref[...] = (a * ct_ref[...] + a_sw * st_ref[...]).astype(kr_ref.dtype)


def _mla_in(h, w_ext, qg, kvg, ct, st, S, *, tm=1024):
    M, K = h.shape
    N = w_ext.shape[1]
    tm = min(tm, S)
    ns = S // tm
    return pl.pallas_call(
        _mla_in_kernel,
        grid=(M // tm,),
        in_specs=[pl.BlockSpec((tm, K), lambda i: (i, 0)),
                  pl.BlockSpec((K, N), lambda i: (0, 0)),
                  pl.BlockSpec((1, MLA_Q_LORA), lambda i: (0, 0)),
                  pl.BlockSpec((1, MLA_KV_LORA), lambda i: (0, 0)),
                  pl.BlockSpec((tm, LANES), lambda i: (i % ns, 0)),
                  pl.BlockSpec((tm, LANES), lambda i: (i % ns, 0))],
        out_specs=[pl.BlockSpec((tm, MLA_Q_LORA), lambda i: (i, 0)),
                   pl.BlockSpec((tm, MLA_KV_LORA), lambda i: (i, 0)),
                   pl.BlockSpec((tm, LANES), lambda i: (i, 0))],
        out_shape=[jax.ShapeDtypeStruct((M, MLA_Q_LORA), BF16),
                   jax.ShapeDtypeStruct((M, MLA_KV_LORA), BF16),
                   jax.ShapeDtypeStruct((M, LANES), BF16)],
        compiler_params=_params("parallel"),
        name="mla_in",
    )(h, w_ext, qg.reshape(1, -1), kvg.reshape(1, -1), ct, st)


def _mla_uq_kernel(cq_ref, wm_ref, ws_ref, ct_ref, st_ref, q_ref, *, heads, scale):
    x = cq_ref[...]
    a = jnp.dot(x, wm_ref[...], preferred_element_type=F32)
    a_sw = jnp.dot(x, ws_ref[...], preferred_element_type=F32)
    ct = ct_ref[...]
    st = st_ref[...]
    for hh in range(heads):
        lo = hh * 2 * LANES
        q_ref[:, lo:lo + LANES] = (a[:, lo:lo + LANES] * scale).astype(q_ref.dtype)
        rot = a[:, lo + LANES:lo + 2 * LANES] * ct + a_sw[:, hh * LANES:(hh + 1) * LANES] * st
        q_ref[:, lo + LANES:lo + 2 * LANES] = (rot * scale).astype(q_ref.dtype)


def _mla_uq(cq, w_main, w_sw, ct, st, S, scale, *, tm=1024, heads_per_step=4):
    M, K = cq.shape
    hp = heads_per_step
    tm = min(tm, S)
    ns = S // tm
    return pl.pallas_call(
        functools.partial(_mla_uq_kernel, heads=hp, scale=scale),
        grid=(M // tm, N_HEADS // hp),
        in_specs=[pl.BlockSpec((tm, K), lambda i, j: (i, 0)),
                  pl.BlockSpec((K, hp * 2 * LANES), lambda i, j: (0, j)),
                  pl.BlockSpec((K, hp * LANES), lambda i, j: (0, j)),
                  pl.BlockSpec((tm, LANES), lambda i, j: (i % ns, 0)),
                  pl.BlockSpec((tm, LANES), lambda i, j: (i % ns, 0))],
        out_specs=pl.BlockSpec((tm, hp * 2 * LANES), lambda i, j: (i, j)),
        out_shape=jax.ShapeDtypeStruct((M, N_HEADS * 2 * LANES), BF16),
        compiler_params=_params("parallel", "parallel"),
        name="mla_uq",
    )(cq, w_main, w_sw, ct, st)


def _causal_sweep(qi, streams):
    def fill(which, ki):
        for stream in streams:
            stream[which][...] = stream[2](ki)

    def step(which, ki, states, diagonal):
        return tuple(stream[3](ki, stream[which], st, diagonal) for stream, st in zip(streams, states))

    fill(0, 0)

    def pair(p, states):
        k0 = 2 * p
        fill(1, k0 + 1)
        states = step(0, k0, states, False)
        fill(0, k0 + 2)
        return step(1, k0 + 1, states, False)

    states = lax.fori_loop(0, qi // 2, pair, tuple(stream[4] for stream in streams))

    def odd(states):
        fill(1, qi)
        states = step(0, qi - 1, states, False)
        return step(1, qi, states, True)

    def even(states):
        return step(0, qi, states, True)

    return lax.cond(qi % 2 == 1, odd, even, states)


def _softmax_update(s, state, v):
    m, acc = state
    m_new = jnp.maximum(m, jnp.max(s, axis=-1, keepdims=True))
    alpha = jnp.exp2(m - m_new)
    p = jnp.exp2(s - m_new).astype(BF16)
    v_ones = jnp.concatenate([v, jnp.ones_like(v)], axis=1)
    acc = alpha * acc + jnp.dot(p, v_ones, preferred_element_type=F32)
    return m_new, acc


def _softmax_update_diag(s_ref, state, v_rows, t):
    h = t // 2
    m, acc = state
    keep = (lax.broadcasted_iota(jnp.int32, (h, h), 1) <= lax.broadcasted_iota(jnp.int32, (h, h), 0))
    s_top = jnp.where(keep, s_ref[0:h, 0:h], -jnp.inf)
    top = _softmax_update(s_top, (m[:h], acc[:h]), v_rows(h))
    s_bot = jnp.concatenate([s_ref[h:t, 0:h], jnp.where(keep, s_ref[h:t, h:t], -jnp.inf)], axis=1)
    bot = _softmax_update(s_bot, (m[h:], acc[h:]), v_rows(t))
    return jnp.concatenate([top[0], bot[0]], axis=0), jnp.concatenate([top[1], bot[1]], axis=0)


def _softmax_init(t):
    return (jnp.full((t, 1), MASKED, F32), jnp.zeros((t, 2 * LANES), F32))


def _softmax_finish(state):
    _, acc = state
    return acc[:, :LANES] / acc[:, LANES:]


def _mla_attn_kernel(q_ref, kn_ref, kr_ref, v_ref, o_ref, *score_refs, t, heads):
    qi = pl.program_id(2)

    def stream(hd):
        lanes = slice(hd * LANES, (hd + 1) * LANES)
        q = q_ref[:, 2 * hd * LANES:2 * (hd + 1) * LANES]

        def scores(ki):
            off = pl.multiple_of(ki * t, t)
            k = jnp.concatenate([kn_ref[pl.ds(off, t), lanes], kr_ref[pl.ds(off, t), :]], axis=1)
            return lax.dot_general(q, k, _NT, preferred_element_type=F32)

        def update(ki, s_ref, state, diagonal):
            off = pl.multiple_of(ki * t, t)
            v_rows = lambda n: v_ref[pl.ds(off, n), lanes]
            if diagonal:
                return _softmax_update_diag(s_ref, state, v_rows, t)
            return _softmax_update(s_ref[...], state, v_rows(t))

        return (score_refs[2 * hd], score_refs[2 * hd + 1], scores, update, _softmax_init(t))

    states = _causal_sweep(qi, [stream(hd) for hd in range(heads)])
    for hd, state in enumerate(states):
        o_ref[:, hd * LANES:(hd + 1) * LANES] = _softmax_finish(state).astype(o_ref.dtype)


def _mla_attn(q, kv, kr, B, S, *, t=512, heads=SOFTMAX_HEADS_PER_STEP):
    nq = S // t
    G = N_HEADS // heads
    w = heads * LANES
    return pl.pallas_call(
        functools.partial(_mla_attn_kernel, t=t, heads=heads),
        grid=(B, G, nq),
        in_specs=[pl.BlockSpec((t, 2 * w), lambda b, g, i: (b * nq + i, g)),
                  pl.BlockSpec((S, w), lambda b, g, i: (b, g)),
                  pl.BlockSpec((S, LANES), lambda b, g, i: (b, 0)),
                  pl.BlockSpec((S, w), lambda b, g, i: (b, G + g))],
        out_specs=pl.BlockSpec((t, w), lambda b, g, i: (b * nq + i, g)),
        out_shape=jax.ShapeDtypeStruct((B * S, N_HEADS * MLA_V), BF16),
        scratch_shapes=[pltpu.VMEM((t, t), F32) for _ in range(2 * heads)],
        compiler_params=_params("parallel", "parallel", "arbitrary"),
        name="mla_attn",
    )(q, kv, kr, kv)


def _rope_tables(S):
    half = MLA_ROPE // 2
    inv = 1.0 / (ROPE_THETA ** (jnp.arange(0, MLA_ROPE, 2, dtype=F32) / MLA_ROPE))
    ang = jnp.arange(S, dtype=F32)[:, None] * inv[None, :]
    cos, sin = jnp.cos(ang), jnp.sin(ang)
    zeros = jnp.zeros((S, LANES - 2 * half), F32)
    return (jnp.concatenate([cos, cos, zeros], axis=1), jnp.concatenate([-sin, sin, zeros], axis=1))


def _mla_weights(w_in, w_uq, w_ukv):
    D = w_in.shape[0]
    half = MLA_ROPE // 2
    base = MLA_Q_LORA + MLA_KV_LORA
    x1, x2 = w_in[:, base:base + half], w_in[:, base + half:base + 2 * half]
    pad = jnp.zeros((D, LANES - 2 * half), w_in.dtype)
    w_in_ext = jnp.concatenate([w_in[:, :base], x1, x2, pad, x2, x1, pad], axis=1).astype(BF16)

    wq = w_uq.reshape(MLA_Q_LORA, N_HEADS, MLA_NOPE + MLA_ROPE)
    nope, r1, r2 = wq[..., :MLA_NOPE], wq[..., MLA_NOPE:MLA_NOPE + half], wq[..., MLA_NOPE + half:]
    padq = jnp.zeros((MLA_Q_LORA, N_HEADS, LANES - 2 * half), w_uq.dtype)
    w_main = jnp.concatenate([nope, r1, r2, padq], axis=-1).reshape(MLA_Q_LORA, N_HEADS * 2 * LANES).astype(BF16)
    w_sw = jnp.concatenate([r2, r1, padq], axis=-1).reshape(MLA_Q_LORA, N_HEADS * LANES).astype(BF16)

    wkv = w_ukv.reshape(MLA_KV_LORA, N_HEADS, MLA_NOPE + MLA_V)
    w_kv = jnp.concatenate([wkv[..., :MLA_NOPE].reshape(MLA_KV_LORA, -1),
                            wkv[..., MLA_NOPE:].reshape(MLA_KV_LORA, -1)], axis=1).astype(BF16)
    return w_in_ext, w_main, w_sw, w_kv


def _mla_mixer(h, B, S, w_in, q_norm, kv_norm, w_uq, w_ukv):
    w_in_ext, w_main, w_sw, w_kv = _mla_weights(w_in, w_uq, w_ukv)
    ct, st = _rope_tables(S)
    cq, ckv, kr = _mla_in(h, w_in_ext, q_norm, kv_norm, ct, st, S)
    scale = float((MLA_NOPE + MLA_ROPE) ** -0.5) * LOG2E
    q = _mla_uq(cq, w_main, w_sw, ct, st, S, scale)
    kv = _proj(ckv, w_kv, BF16)
    return _mla_attn(q, kv, kr, B, S)


def _hgrn_kernel(q_ref, f_ref, i_ref, g_ref, lbl_ref, on_ref, o_ref, state_ref, b_scr, k_scr,
                 *, layer, tile, chunk, heads):
    C, SB = chunk, HGRN_SUB
    nb = C // SB

    @pl.when(pl.program_id(2) == 0)
    def _():
        state_ref[...] = jnp.zeros_like(state_ref)

    lg = lbl_ref[...]
    e = jnp.exp(lg - jnp.max(lg, axis=0, keepdims=True))
    p = e / jnp.sum(e, axis=0, keepdims=True)
    cs = p[0:1]
    for r in range(1, layer + 1):
        cs = cs + p[r:r + 1]
    lb_all = cs - p[0:1]
    log_lb_all = jnp.log(lb_all)
    log1m_lb_all = jnp.log1p(-lb_all)
    one_m_lb_all = 1.0 - lb_all
    onorm_all = on_ref[...]

    row = lax.broadcasted_iota(jnp.int32, (C, 3 * C), 0)
    col = lax.broadcasted_iota(jnp.int32, (C, 3 * C), 1) % C
    blk0 = (row // SB) * SB
    tri = jnp.concatenate([col <= row, col <= blk0, col <= jnp.minimum(blk0 + SB, C - 1)],
                          axis=0).astype(BF16)
    brow = lax.broadcasted_iota(jnp.int32, (C, C), 0) // SB
    bcol = lax.broadcasted_iota(jnp.int32, (C, C), 1) // SB
    bdiff = brow - bcol
    sub = lax.broadcasted_iota(jnp.int32, (SB, LANES), 0)
    causal_cap = [jnp.where(sub >= s, 0.0, -jnp.inf).astype(F32) for s in range(SB)]

    def chunk_step(c, slot, hd):
        lanes = slice(hd * LANES, (hd + 1) * LANES)
        log_lb, log1m_lb, one_m_lb = log_lb_all[:, lanes], log1m_lb_all[:, lanes], one_m_lb_all[:, lanes]
        onorm = onorm_all[:, lanes]
        slot = hd * HGRN_GROUP + slot
        off = pl.multiple_of(c * C, C)
        q = q_ref[pl.ds(off, C), lanes]
        fp = f_ref[pl.ds(off, C), lanes]
        v = i_ref[pl.ds(off, C), lanes]
        g = g_ref[pl.ds(off, C), lanes]

        ls = jnp.minimum(fp, 0.0) - jnp.log(1.0 + jnp.exp(-jnp.abs(fp)))
        cc = log1m_lb + ls
        lf = jnp.maximum(log_lb, cc) + jnp.log(1.0 + jnp.exp(-jnp.abs(log_lb - cc)))
        kk = one_m_lb * jax.nn.sigmoid(-fp)

        p1 = lf.astype(BF16)
        r1 = lf - p1.astype(F32)
        p2 = r1.astype(BF16)
        p3 = (r1 - p2.astype(F32)).astype(BF16)
        cums = jnp.dot(tri, jnp.concatenate([p1, p2, p3], axis=0), preferred_element_type=F32)
        b, r, r_next = cums[:C], cums[C:2 * C], cums[2 * C:]
        bend = b[C - 1:C, :]
        b_scr[slot] = b
        k_scr[slot] = kk

        st_t = state_ref[hd]
        qe = q * jnp.exp(b)
        o = lax.dot_general(qe.astype(BF16), st_t.astype(BF16), _NT, preferred_element_type=F32)
        kd = kk * jnp.exp(bend - b)
        state_ref[hd] = st_t * jnp.exp(bend) + jnp.dot(v.T.astype(BF16), kd.astype(BF16),
                                                      preferred_element_type=F32)

        kt = kk * jnp.exp(r_next - b)
        gdec = jnp.exp(r_next - r)
        ql = q * jnp.exp(b - r)
        levels = [ql]
        for lvl in range(1, nb - 1):
            fac = jnp.concatenate([jnp.zeros((lvl * SB, LANES), F32), gdec[:C - lvl * SB]], axis=0)
            ql = ql * fac
            levels.append(ql)
        qs = jnp.concatenate(levels, axis=0).astype(BF16)
        rl = lax.dot_general(qs, kt.astype(BF16), _NT, preferred_element_type=F32)
        a = jnp.zeros((C, C), F32)
        for lvl in range(1, nb):
            a = a + jnp.where(bdiff == lvl, rl[(lvl - 1) * C:lvl * C, :], 0.0)
        o = o + jnp.dot(a.astype(BF16), v.astype(BF16), preferred_element_type=F32)

        diag = []
        for blk in range(nb):
            bq = b[blk * SB:(blk + 1) * SB]
            qq = q[blk * SB:(blk + 1) * SB]
            acc = jnp.zeros((SB, LANES), F32)
            for s in range(SB):
                rr = blk * SB + s
                dec = jnp.exp(jnp.minimum(bq - b_scr[slot, pl.ds(rr, 1), :], causal_cap[s]))
                a_ts = jnp.sum(qq * k_scr[slot, pl.ds(rr, 1), :] * dec, axis=-1, keepdims=True)
                acc = acc + a_ts * i_ref[pl.ds(off + rr, 1), :][:, lanes]
            diag.append(acc)
        o = o + jnp.concatenate(diag, axis=0)

        y = _rms_rows(o, onorm)
        o_ref[pl.ds(off, C), lanes] = (y * (g * jax.nn.sigmoid(g))).astype(o_ref.dtype)

    def chunk_group(grp, carry):
        for slot in range(HGRN_GROUP):
            for hd in range(heads):
                chunk_step(HGRN_GROUP * grp + slot, slot, hd)
        return carry

    lax.fori_loop(0, tile // (HGRN_GROUP * C), chunk_group, 0)


def _hgrn_mixer_core(proj, lb_logits, o_norm, B, S, layer, *, tile=512, heads=HGRN_HEADS_PER_STEP):
    nt = S // tile
    G = N_HEADS // heads
    w = heads * LANES
    blk = lambda sec: pl.BlockSpec((tile, w), lambda b, g, t, sec=sec: (b * nt + t, sec * G + g))
    return pl.pallas_call(
        functools.partial(_hgrn_kernel, layer=layer, tile=tile, chunk=HGRN_CHUNK, heads=heads),
        grid=(B, G, nt),
        in_specs=[blk(0), blk(1), blk(2), blk(3),
                  pl.BlockSpec((DEPTH, w), lambda b, g, t: (0, g)),
                  pl.BlockSpec((1, w), lambda b, g, t: (0, g))],
        out_specs=pl.BlockSpec((tile, w), lambda b, g, t: (b * nt + t, g)),
        out_shape=jax.ShapeDtypeStruct((B * S, D_MODEL), BF16),
        scratch_shapes=[pltpu.VMEM((heads, LANES, LANES), F32),
                        pltpu.VMEM((heads * HGRN_GROUP, HGRN_CHUNK, LANES), F32),
                        pltpu.VMEM((heads * HGRN_GROUP, HGRN_CHUNK, LANES), F32)],
        compiler_params=_params("parallel", "parallel", "arbitrary"),
        name="hgrn",
    )(proj, proj, proj, proj, lb_logits, o_norm.reshape(1, -1))


SB_CUMSUM_BLOCK = 256


def _sb_kernel(q_ref, k_ref, v_ref, o_ref, *score_refs, t, heads):
    cb = SB_CUMSUM_BLOCK
    qi = pl.program_id(2)
    row = lax.broadcasted_iota(jnp.int32, (t, t), 0)
    col = lax.broadcasted_iota(jnp.int32, (t, t), 1)
    strict = col < row
    jj = lax.broadcasted_iota(jnp.int32, (2 * cb, cb), 0) % cb
    ss = lax.broadcasted_iota(jnp.int32, (2 * cb, cb), 1)
    from_s2 = (jj >= ss).astype(BF16)

    def stream(hd):
        lanes = slice(hd * LANES, (hd + 1) * LANES)
        q = q_ref[:, lanes]

        def scores(ki):
            off = pl.multiple_of(ki * t, t)
            return lax.dot_general(q, k_ref[pl.ds(off, t), lanes], _NT, preferred_element_type=F32)

        def update(ki, zn_ref, acc, diagonal):
            off = pl.multiple_of(ki * t, t)
            tail = None
            new = None
            for j in reversed(range(t // cb)):
                cols = slice(j * cb, (j + 1) * cb)
                zn = zn_ref[:, cols]
                l1m = jnp.minimum(zn, 0.0) - jnp.log(1.0 + jnp.exp2(jnp.abs(zn) * (-LOG2E)))
                if diagonal:
                    l1m = jnp.where(strict[:, cols], l1m, 0.0)
                hi = l1m.astype(BF16)
                lo = (l1m - hi.astype(F32)).astype(BF16)
                x = jnp.dot(jnp.concatenate([hi, lo], axis=1), from_s2, preferred_element_type=F32)
                if tail is not None:
                    x = x + tail
                tail = x[:, 0:1]
                w = jnp.exp(x - zn)
                if diagonal:
                    w = jnp.where(strict[:, cols], w, 0.0)
                part = jnp.dot(w.astype(BF16), v_ref[pl.ds(off + j * cb, cb), lanes],
                               preferred_element_type=F32)
                new = part if new is None else new + part
            return acc * jnp.exp(tail) + new

        return (score_refs[2 * hd], score_refs[2 * hd + 1], scores, update, jnp.zeros((t, HEAD_DIM), F32))

    accs = _causal_sweep(qi, [stream(hd) for hd in range(heads)])
    for hd, acc in enumerate(accs):
        o_ref[:, hd * LANES:(hd + 1) * LANES] = acc.astype(o_ref.dtype)


def _sb_attn(qkv, B, S, *, t=512, heads=SB_HEADS_PER_STEP):
    nq = S // t
    G = N_HEADS // heads
    w = heads * LANES
    return pl.pallas_call(
        functools.partial(_sb_kernel, t=t, heads=heads),
        grid=(B, G, nq),
        in_specs=[pl.BlockSpec((t, w), lambda b, g, i: (b * nq + i, g)),
                  pl.BlockSpec((S, w), lambda b, g, i: (b, G + g)),
                  pl.BlockSpec((S, w), lambda b, g, i: (b, 2 * G + g))],
        out_specs=pl.BlockSpec((t, w), lambda b, g, i: (b * nq + i, g)),
        out_shape=jax.ShapeDtypeStruct((B * S, D_MODEL), BF16),
        scratch_shapes=[pltpu.VMEM((t, t), F32) for _ in range(2 * heads)],
        compiler_params=_params("parallel", "parallel", "arbitrary"),
        name="sb_attn",
    )(qkv, qkv, qkv)


MOBA_VETO = 2.0 ** 100


def _moba_kernel(q_ref, k_ref, v_ref, o_ref, kmean_ref, *score_refs, nblk, t, heads):
    bpt = t // MOBA_BLOCK
    qi = pl.program_id(2)

    @pl.when(qi == 0)
    def _():
        for hd in range(heads):
            k_all = k_ref[:, hd * LANES:(hd + 1) * LANES].astype(F32)
            kmean_ref[hd] = jnp.mean(k_all.reshape(nblk, MOBA_BLOCK, LANES), axis=1)

    blk = lax.broadcasted_iota(jnp.int32, (nblk, t), 0)
    own = qi * bpt + lax.broadcasted_iota(jnp.int32, (nblk, t), 1) // MOBA_BLOCK
    neg_inf = jnp.float32(-jnp.inf)
    lane_blk = lax.broadcasted_iota(jnp.int32, (t, LANES), 1)
    key_blk = lax.broadcasted_iota(jnp.int32, (t, LANES), 0) // MOBA_BLOCK

    def stream(hd):
        lanes = slice(hd * LANES, (hd + 1) * LANES)
        q = q_ref[:, lanes]

        gate = lax.dot_general(kmean_ref[hd], q.astype(F32), _NT, precision=lax.Precision.HIGHEST,
                               preferred_element_type=F32)
        gate = jnp.where(blk < own, gate, neg_inf)
        sel = (blk == own).astype(F32)
        for _ in range(MOBA_TOPK):
            mx = jnp.max(gate, axis=0, keepdims=True)
            first = jnp.min(jnp.where(gate == mx, blk, nblk), axis=0, keepdims=True)
            pick = jnp.logical_and(blk == first, mx > neg_inf)
            sel = jnp.where(pick, 1.0, sel)
            gate = jnp.where(pick, neg_inf, gate)

        veto = jnp.concatenate([sel - 1.0, jnp.zeros((LANES - nblk, t), F32)], axis=0)
        q_ext = jnp.concatenate([q, veto.T.astype(BF16)], axis=1)

        def scores(kc):
            off = pl.multiple_of(kc * t, t)
            hot = jnp.where(lane_blk == kc * bpt + key_blk, MOBA_VETO, 0.0).astype(BF16)
            k_ext = jnp.concatenate([k_ref[pl.ds(off, t), lanes], hot], axis=1)
            return lax.dot_general(q_ext, k_ext, _NT, preferred_element_type=F32)

        def update(kc, s_ref, state, diagonal):
            off = pl.multiple_of(kc * t, t)
            v_rows = lambda n: v_ref[pl.ds(off, n), lanes]
            if diagonal:
                return _softmax_update_diag(s_ref, state, v_rows, t)
            return _softmax_update(s_ref[...], state, v_rows(t))

        return (score_refs[2 * hd], score_refs[2 * hd + 1], scores, update, _softmax_init(t))

    states = _causal_sweep(qi, [stream(hd) for hd in range(heads)])
    for hd, state in enumerate(states):
        o_ref[:, hd * LANES:(hd + 1) * LANES] = _softmax_finish(state).astype(o_ref.dtype)


def _moba_attn(qkv, B, S, *, t=2 * MOBA_BLOCK, heads=SOFTMAX_HEADS_PER_STEP):
    assert t == 2 * MOBA_BLOCK
    nq = S // t
    nblk = S // MOBA_BLOCK
    G = N_HEADS // heads
    w = heads * LANES
    return pl.pallas_call(
        functools.partial(_moba_kernel, nblk=nblk, t=t, heads=heads),
        grid=(B, G, nq),
        in_specs=[pl.BlockSpec((t, w), lambda b, g, i: (b * nq + i, g)),
                  pl.BlockSpec((S, w), lambda b, g, i: (b, G + g)),
                  pl.BlockSpec((S, w), lambda b, g, i: (b, 2 * G + g))],
        out_specs=pl.BlockSpec((t, w), lambda b, g, i: (b * nq + i, g)),
        out_shape=jax.ShapeDtypeStruct((B * S, D_MODEL), BF16),
        scratch_shapes=[pltpu.VMEM((heads, nblk, LANES), F32)]
                       + [pltpu.VMEM((t, t), F32) for _ in range(2 * heads)],
        compiler_params=_params("parallel", "parallel", "arbitrary"),
        name="moba_attn",
    )(qkv, qkv, qkv)


def _qkv_proj(h, w_in, layer, q_scale):
    return _proj(h, w_in, BF16, layer=layer, scaled_cols=D_MODEL, scale=q_scale)


def _sb_mixer(h, B, S, w_in, layer=0):
    return _sb_attn(_qkv_proj(h, w_in, layer, -float(HEAD_DIM ** -0.5)), B, S)


def _moba_mixer(h, B, S, w_in, layer=0):
    return _moba_attn(_qkv_proj(h, w_in, layer, float(HEAD_DIM ** -0.5) * LOG2E), B, S)


def kernel(x, mla_w_in, mla_q_norm, mla_kv_norm, mla_w_uq, mla_w_ukv, mla_w_o, hgrn_w_in, hgrn_lb_logits, hgrn_o_norm, hgrn_w_o, sb_w_in, sb_w_o, moba_w_in, moba_w_o, ln_g, ln_b, mlp_w1, mlp_w2):
    B, S, D = x.shape
    assert D == D_MODEL and S % MOBA_BLOCK == 0 and S % 512 == 0
    h = x.reshape(B * S, D)
    hb = h
    mlp_w2_bf16 = mlp_w2.astype(BF16)
    n_mixers = 4
    for i in range(DEPTH):
        kind, slot = i % n_mixers, i // n_mixers
        if kind == 0:
            o = _mla_mixer(hb, B, S, mla_w_in[slot], mla_q_norm[slot], mla_kv_norm[slot],
                           mla_w_uq[slot], mla_w_ukv[slot])
            w_o = mla_w_o[slot]
        elif kind == 1:
            proj = _proj(hb, hgrn_w_in, F32, layer=slot)
            o = _hgrn_mixer_core(proj, hgrn_lb_logits, hgrn_o_norm[slot], B, S, i)
            w_o = hgrn_w_o[slot]
        elif kind == 2:
            o = _sb_mixer(hb, B, S, sb_w_in, slot)
            w_o = sb_w_o[slot]
        else:
            o = _moba_mixer(hb, B, S, moba_w_in, slot)
            w_o = moba_w_o[slot]
        h, hb = _proj_res_ln(o, w_o.astype(BF16), h, ln_g[i, 0], ln_b[i, 0])
        a = _proj(hb, mlp_w1, BF16, layer=i, act="relu2")
        h, hb = _proj_res_ln(a, mlp_w2_bf16, h, ln_g[i, 1], ln_b[i, 1], layer=i)
    return h.reshape(B, S, D)
```

```python
import functools

import jax
import jax.numpy as jnp
from jax import lax
from jax.experimental import pallas as pl
from jax.experimental.pallas import tpu as pltpu

F32 = jnp.float32
BF16 = jnp.bfloat16

D_MODEL = 2048
DEPTH = 4
N_HEADS = 16
HEAD_DIM = 128
MLA_Q_LORA = 512
MLA_KV_LORA = 512
MLA_NOPE = 128
MLA_ROPE = 64
MLA_V = 128
ROPE_THETA = 10000.0
HGRN_CHUNK = 64
HGRN_SUB = 8
HGRN_GROUP = 8
HGRN_HEADS_PER_STEP = 4
SB_HEADS_PER_STEP = 4
SOFTMAX_HEADS_PER_STEP = 4
MOBA_BLOCK = 256
MOBA_TOPK = 3
ALPHA = float((2 * DEPTH) ** 0.25)
LN_EPS = 1e-5
RMS_EPS = 1e-6

V7X_VMEM_BYTES = 64 * 1024 * 1024
VMEM_LIMIT = V7X_VMEM_BYTES - 8 * 1024 * 1024
PROJ_VMEM_BUDGET = VMEM_LIMIT - 8 * 1024 * 1024
LANES = 128
MASKED = -1e30
LOG2E = 1.4426950408889634

_NT = (((1,), (1,)), ((), ()))


def _params(*sem):
    return pltpu.CompilerParams(dimension_semantics=sem, vmem_limit_bytes=VMEM_LIMIT)


def _layer_norm_rows(y, g, b):
    mu = jnp.mean(y, axis=-1, keepdims=True)
    d = y - mu
    var = jnp.mean(d * d, axis=-1, keepdims=True)
    return d * lax.rsqrt(var + LN_EPS) * g + b


def _rms_rows(x, g):
    return x * lax.rsqrt(jnp.mean(x * x, axis=-1, keepdims=True) + RMS_EPS) * g


def _proj_kernel(*refs, act, scaled_tiles, scale, side_cast):
    if side_cast:
        x_ref, w_ref, c_ref, o_ref, co_ref, wbf_ref = refs
        co_ref[...] = c_ref[...].astype(BF16)
    else:
        x_ref, w_ref, o_ref, wbf_ref = refs

    @pl.when(pl.program_id(1) == 0)
    def _():
        wbf_ref[...] = w_ref[...].astype(BF16)

    acc = jnp.dot(x_ref[...].astype(BF16), wbf_ref[...], preferred_element_type=F32)
    if act == "relu2":
        r = jnp.maximum(acc, 0.0)
        acc = r * r
    if scaled_tiles:
        acc = acc * jnp.where(pl.program_id(0) < scaled_tiles, scale, 1.0)
    o_ref[...] = acc.astype(o_ref.dtype)


def _proj_row_tile(M, K, N, tn, x_dtype, w_dtype, out_dtype, cast_stack):
    size = lambda dt: jnp.dtype(dt).itemsize
    fixed = 2 * K * tn * size(w_dtype) + K * tn * size(BF16)
    for tm in (2048, 1024, 512, 256):
        if M % tm:
            continue
        need = fixed + 2 * tm * (K * size(x_dtype) + tn * size(out_dtype))
        if cast_stack is not None:
            slab = cast_stack.shape[1] * cast_stack.shape[2] // ((N // tn) * (M // tm))
            need += 2 * slab * (size(cast_stack.dtype) + size(BF16))
        if need <= PROJ_VMEM_BUDGET:
            return tm
    raise ValueError("no row tile fits")


def _proj(x, w, out_dtype, *, layer=0, tn=1024, act=None, scaled_cols=0, scale=1.0, cast_stack=None):
    M, K = x.shape
    N = w.shape[-1]
    tn = min(tn, N)
    tm = _proj_row_tile(M, K, N, tn, x.dtype, w.dtype, out_dtype, cast_stack)
    assert scaled_cols % tn == 0 and N % tn == 0
    ni = M // tm
    if w.ndim == 3:
        w_spec = pl.BlockSpec((None, K, tn), lambda j, i: (layer, 0, j))
    else:
        w_spec = pl.BlockSpec((K, tn), lambda j, i: (0, j))
    in_specs = [pl.BlockSpec((tm, K), lambda j, i: (i, 0)), w_spec]
    out_specs = [pl.BlockSpec((tm, tn), lambda j, i: (i, j))]
    out_shape = [jax.ShapeDtypeStruct((M, N), out_dtype)]
    operands = [x, w]
    if cast_stack is not None:
        _, R, C = cast_stack.shape
        steps = (N // tn) * ni
        assert R % steps == 0
        in_specs.append(pl.BlockSpec((None, R // steps, C), lambda j, i: (layer, j * ni + i, 0)))
        out_specs.append(pl.BlockSpec((R // steps, C), lambda j, i: (j * ni + i, 0)))
        out_shape.append(jax.ShapeDtypeStruct((R, C), BF16))
        operands.append(cast_stack)
    outs = pl.pallas_call(
        functools.partial(_proj_kernel, act=act, scaled_tiles=scaled_cols // tn, scale=scale,
                          side_cast=cast_stack is not None),
        grid=(N // tn, ni),
        in_specs=in_specs,
        out_specs=out_specs,
        out_shape=out_shape,
        scratch_shapes=[pltpu.VMEM((K, tn), BF16)],
        compiler_params=_params("parallel", "arbitrary"),
        name="proj",
    )(*operands)
    return outs[0] if cast_stack is None else outs


def _proj_res_ln_kernel(x_ref, w_ref, h_ref, g_ref, b_ref, o_ref, ob_ref, acc_ref, *, nk):
    k = pl.program_id(1)

    def part():
        return jnp.dot(x_ref[...], w_ref[...], preferred_element_type=F32)

    def finish(y):
        out = _layer_norm_rows(ALPHA * h_ref[...] + y, g_ref[...], b_ref[...])
        o_ref[...] = out
        ob_ref[...] = out.astype(BF16)

    if nk == 1:
        finish(part())
    else:
        @pl.when(k == 0)
        def _():
            acc_ref[...] = part()

        @pl.when(jnp.logical_and(k > 0, k < nk - 1))
        def _():
            acc_ref[...] += part()

        @pl.when(k == nk - 1)
        def _():
            finish(acc_ref[...] + part())


def _proj_res_ln(x, w, h, g, b, *, layer=0, tm=512, tk=2048):
    M, K = x.shape
    N = w.shape[-1]
    tm, tk = min(tm, M), min(tk, K)
    nk = K // tk
    if w.ndim == 3:
        w_spec = pl.BlockSpec((None, tk, N), lambda i, k: (layer, k, 0))
    else:
        w_spec = pl.BlockSpec((tk, N), lambda i, k: (k, 0))
    return pl.pallas_call(
        functools.partial(_proj_res_ln_kernel, nk=nk),
        grid=(M // tm, nk),
        in_specs=[pl.BlockSpec((tm, tk), lambda i, k: (i, k)),
                  w_spec,
                  pl.BlockSpec((tm, N), lambda i, k: (i, 0)),
                  pl.BlockSpec((1, N), lambda i, k: (0, 0)),
                  pl.BlockSpec((1, N), lambda i, k: (0, 0))],
        out_specs=[pl.BlockSpec((tm, N), lambda i, k: (i, 0)),
                   pl.BlockSpec((tm, N), lambda i, k: (i, 0))],
        out_shape=[jax.ShapeDtypeStruct((M, N), F32), jax.ShapeDtypeStruct((M, N), BF16)],
        scratch_shapes=[pltpu.VMEM((tm, N), F32)],
        compiler_params=_params("parallel", "arbitrary"),
        name="proj_res_ln",
    )(x, w, h, g.reshape(1, N), b.reshape(1, N))


def _mla_in_kernel(h_ref, w_ref, qg_ref, kvg_ref, ct_ref, st_ref, cq_ref, ckv_ref, kr_ref):
    acc = jnp.dot(h_ref[...].astype(BF16), w_ref[...], preferred_element_type=F32)
    ql, kvl = MLA_Q_LORA, MLA_KV_LORA
    cq_ref[...] = _rms_rows(acc[:, :ql], qg_ref[...]).astype(cq_ref.dtype)
    ckv_ref[...] = _rms_rows(acc[:, ql:ql + kvl], kvg_ref[...]).astype(ckv_ref.dtype)
    a = acc[:, ql + kvl:ql + kvl + LANES]
    a_sw = acc[:, ql + kvl + LANES:]
    kr_ref[...] = (a * ct_ref[...] + a_sw * st_ref[...]).astype(kr_ref.dtype)


def _mla_in(h, w_ext, qg, kvg, ct, st, S, *, tm=1024):
    M, K = h.shape
    N = w_ext.shape[1]
    tm = min(tm, S)
    ns = S // tm
    return pl.pallas_call(
        _mla_in_kernel,
        grid=(M // tm,),
        in_specs=[pl.BlockSpec((tm, K), lambda i: (i, 0)),
                  pl.BlockSpec((K, N), lambda i: (0, 0)),
                  pl.BlockSpec((1, MLA_Q_LORA), lambda i: (0, 0)),
                  pl.BlockSpec((1, MLA_KV_LORA), lambda i: (0, 0)),
                  pl.BlockSpec((tm, LANES), lambda i: (i % ns, 0)),
                  pl.BlockSpec((tm, LANES), lambda i: (i % ns, 0))],
        out_specs=[pl.BlockSpec((tm, MLA_Q_LORA), lambda i: (i, 0)),
                   pl.BlockSpec((tm, MLA_KV_LORA), lambda i: (i, 0)),
                   pl.BlockSpec((tm, LANES), lambda i: (i, 0))],
        out_shape=[jax.ShapeDtypeStruct((M, MLA_Q_LORA), BF16),
                   jax.ShapeDtypeStruct((M, MLA_KV_LORA), BF16),
                   jax.ShapeDtypeStruct((M, LANES), BF16)],
        compiler_params=_params("parallel"),
        name="mla_in",
    )(h, w_ext, qg.reshape(1, -1), kvg.reshape(1, -1), ct, st)


def _mla_uq_kernel(cq_ref, wm_ref, ws_ref, ct_ref, st_ref, q_ref, *, heads, scale):
    x = cq_ref[...]
    a = jnp.dot(x, wm_ref[...], preferred_element_type=F32)
    a_sw = jnp.dot(x, ws_ref[...], preferred_element_type=F32)
    ct = ct_ref[...]
    st = st_ref[...]
    for hh in range(heads):
        lo = hh * 2 * LANES
        q_ref[:, lo:lo + LANES] = (a[:, lo:lo + LANES] * scale).astype(q_ref.dtype)
        rot = a[:, lo + LANES:lo + 2 * LANES] * ct + a_sw[:, hh * LANES:(hh + 1) * LANES] * st
        q_ref[:, lo + LANES:lo + 2 * LANES] = (rot * scale).astype(q_ref.dtype)


def _mla_uq(cq, w_main, w_sw, ct, st, S, scale, *, tm=1024, heads_per_step=4):
    M, K = cq.shape
    hp = heads_per_step
    tm = min(tm, S)
    ns = S // tm
    return pl.pallas_call(
        functools.partial(_mla_uq_kernel, heads=hp, scale=scale),
        grid=(M // tm, N_HEADS // hp),
        in_specs=[pl.BlockSpec((tm, K), lambda i, j: (i, 0)),
                  pl.BlockSpec((K, hp * 2 * LANES), lambda i, j: (0, j)),
                  pl.BlockSpec((K, hp * LANES), lambda i, j: (0, j)),
                  pl.BlockSpec((tm, LANES), lambda i, j: (i % ns, 0)),
                  pl.BlockSpec((tm, LANES), lambda i, j: (i % ns, 0))],
        out_specs=pl.BlockSpec((tm, hp * 2 * LANES), lambda i, j: (i, j)),
        out_shape=jax.ShapeDtypeStruct((M, N_HEADS * 2 * LANES), BF16),
        compiler_params=_params("parallel", "parallel"),
        name="mla_uq",
    )(cq, w_main, w_sw, ct, st)


def _causal_sweep(qi, streams):
    def fill(which, ki):
        for stream in streams:
            stream[which][...] = stream[2](ki)

    def step(which, ki, states, diagonal):
        return tuple(stream[3](ki, stream[which], st, diagonal) for stream, st in zip(streams, states))

    fill(0, 0)

    def pair(p, states):
        k0 = 2 * p
        fill(1, k0 + 1)
        states = step(0, k0, states, False)
        fill(0, k0 + 2)
        return step(1, k0 + 1, states, False)

    states = lax.fori_loop(0, qi // 2, pair, tuple(stream[4] for stream in streams))

    def odd(states):
        fill(1, qi)
        states = step(0, qi - 1, states, False)
        return step(1, qi, states, True)

    def even(states):
        return step(0, qi, states, True)

    return lax.cond(qi % 2 == 1, odd, even, states)


def _softmax_update(s, state, v):
    m, acc = state
    m_new = jnp.maximum(m, jnp.max(s, axis=-1, keepdims=True))
    alpha = jnp.exp2(m - m_new)
    p = jnp.exp2(s - m_new).astype(BF16)
    v_ones = jnp.concatenate([v, jnp.ones_like(v)], axis=1)
    acc = alpha * acc + jnp.dot(p, v_ones, preferred_element_type=F32)
    return m_new, acc


def _softmax_update_diag(s_ref, state, v_rows, t):
    h = t // 2
    m, acc = state
    keep = (lax.broadcasted_iota(jnp.int32, (h, h), 1) <= lax.broadcasted_iota(jnp.int32, (h, h), 0))
    s_top = jnp.where(keep, s_ref[0:h, 0:h], -jnp.inf)
    top = _softmax_update(s_top, (m[:h], acc[:h]), v_rows(h))
    s_bot = jnp.concatenate([s_ref[h:t, 0:h], jnp.where(keep, s_ref[h:t, h:t], -jnp.inf)], axis=1)
    bot = _softmax_update(s_bot, (m[h:], acc[h:]), v_rows(t))
    return jnp.concatenate([top[0], bot[0]], axis=0), jnp.concatenate([top[1], bot[1]], axis=0)


def _softmax_init(t):
    return (jnp.full((t, 1), MASKED, F32), jnp.zeros((t, 2 * LANES), F32))


def _softmax_finish(state):
    _, acc = state
    return acc[:, :LANES] / acc[:, LANES:]


def _mla_attn_kernel(q_ref, kn_ref, kr_ref, v_ref, o_ref, *score_refs, t, heads):
    qi = pl.program_id(2)

    def stream(hd):
        lanes = slice(hd * LANES, (hd + 1) * LANES)
        q = q_ref[:, 2 * hd * LANES:2 * (hd + 1) * LANES]

        def scores(ki):
            off = pl.multiple_of(ki * t, t)
            k = jnp.concatenate([kn_ref[pl.ds(off, t), lanes], kr_ref[pl.ds(off, t), :]], axis=1)
            return lax.dot_general(q, k, _NT, preferred_element_type=F32)

        def update(ki, s_ref, state, diagonal):
            off = pl.multiple_of(ki * t, t)
            v_rows = lambda n: v_ref[pl.ds(off, n), lanes]
            if diagonal:
                return _softmax_update_diag(s_ref, state, v_rows, t)
            return _softmax_update(s_ref[...], state, v_rows(t))

        return (score_refs[2 * hd], score_refs[2 * hd + 1], scores, update, _softmax_init(t))

    states = _causal_sweep(qi, [stream(hd) for hd in range(heads)])
    for hd, state in enumerate(states):
        o_ref[:, hd * LANES:(hd + 1) * LANES] = _softmax_finish(state).astype(o_ref.dtype)


def _mla_attn(q, kv, kr, B, S, *, t=512, heads=SOFTMAX_HEADS_PER_STEP):
    nq = S // t
    G = N_HEADS // heads
    w = heads * LANES
    return pl.pallas_call(
        functools.partial(_mla_attn_kernel, t=t, heads=heads),
        grid=(B, G, nq),
        in_specs=[pl.BlockSpec((t, 2 * w), lambda b, g, i: (b * nq + i, g)),
                  pl.BlockSpec((S, w), lambda b, g, i: (b, g)),
                  pl.BlockSpec((S, LANES), lambda b, g, i: (b, 0)),
                  pl.BlockSpec((S, w), lambda b, g, i: (b, G + g))],
        out_specs=pl.BlockSpec((t, w), lambda b, g, i: (b * nq + i, g)),
        out_shape=jax.ShapeDtypeStruct((B * S, N_HEADS * MLA_V), BF16),
        scratch_shapes=[pltpu.VMEM((t, t), F32) for _ in range(2 * heads)],
        compiler_params=_params("parallel", "parallel", "arbitrary"),
        name="mla_attn",
    )(q, kv, kr, kv)


def _rope_tables(S):
    half = MLA_ROPE // 2
    inv = 1.0 / (ROPE_THETA ** (jnp.arange(0, MLA_ROPE, 2, dtype=F32) / MLA_ROPE))
    ang = jnp.arange(S, dtype=F32)[:, None] * inv[None, :]
    cos, sin = jnp.cos(ang), jnp.sin(ang)
    zeros = jnp.zeros((S, LANES - 2 * half), F32)
    return (jnp.concatenate([cos, cos, zeros], axis=1), jnp.concatenate([-sin, sin, zeros], axis=1))


def _mla_weights(w_in, w_uq, w_ukv):
    D = w_in.shape[0]
    half = MLA_ROPE // 2
    base = MLA_Q_LORA + MLA_KV_LORA
    x1, x2 = w_in[:, base:base + half], w_in[:, base + half:base + 2 * half]
    pad = jnp.zeros((D, LANES - 2 * half), w_in.dtype)
    w_in_ext = jnp.concatenate([w_in[:, :base], x1, x2, pad, x2, x1, pad], axis=1).astype(BF16)

    wq = w_uq.reshape(MLA_Q_LORA, N_HEADS, MLA_NOPE + MLA_ROPE)
    nope, r1, r2 = wq[..., :MLA_NOPE], wq[..., MLA_NOPE:MLA_NOPE + half], wq[..., MLA_NOPE + half:]
    padq = jnp.zeros((MLA_Q_LORA, N_HEADS, LANES - 2 * half), w_uq.dtype)
    w_main = jnp.concatenate([nope, r1, r2, padq], axis=-1).reshape(MLA_Q_LORA, N_HEADS * 2 * LANES).astype(BF16)
    w_sw = jnp.concatenate([r2, r1, padq], axis=-1).reshape(MLA_Q_LORA, N_HEADS * LANES).astype(BF16)

    wkv = w_ukv.reshape(MLA_KV_LORA, N_HEADS, MLA_NOPE + MLA_V)
    w_kv = jnp.concatenate([wkv[..., :MLA_NOPE].reshape(MLA_KV_LORA, -1),
                            wkv[..., MLA_NOPE:].reshape(MLA_KV_LORA, -1)], axis=1).astype(BF16)
    return w_in_ext, w_main, w_sw, w_kv


def _mla_mixer(h, B, S, w_in, q_norm, kv_norm, w_uq, w_ukv):
    w_in_ext, w_main, w_sw, w_kv = _mla_weights(w_in, w_uq, w_ukv)
    ct, st = _rope_tables(S)
    cq, ckv, kr = _mla_in(h, w_in_ext, q_norm, kv_norm, ct, st, S)
    scale = float((MLA_NOPE + MLA_ROPE) ** -0.5) * LOG2E
    q = _mla_uq(cq, w_main, w_sw, ct, st, S, scale)
    kv = _proj(ckv, w_kv, BF16)
    return _mla_attn(q, kv, kr, B, S)


def _hgrn_kernel(q_ref, f_ref, i_ref, g_ref, lbl_ref, on_ref, o_ref, state_ref, b_scr, k_scr,
                 *, layer, tile, chunk, heads):
    C, SB = chunk, HGRN_SUB
    nb = C // SB

    @pl.when(pl.program_id(2) == 0)
    def _():
        state_ref[...] = jnp.zeros_like(state_ref)

    lg = lbl_ref[...]
    e = jnp.exp(lg - jnp.max(lg, axis=0, keepdims=True))
    p = e / jnp.sum(e, axis=0, keepdims=True)
    cs = p[0:1]
    for r in range(1, layer + 1):
        cs = cs + p[r:r + 1]
    lb_all = cs - p[0:1]
    log_lb_all = jnp.log(lb_all)
    log1m_lb_all = jnp.log1p(-lb_all)
    one_m_lb_all = 1.0 - lb_all
    onorm_all = on_ref[...]

    row = lax.broadcasted_iota(jnp.int32, (C, 3 * C), 0)
    col = lax.broadcasted_iota(jnp.int32, (C, 3 * C), 1) % C
    blk0 = (row // SB) * SB
    tri = jnp.concatenate([col <= row, col <= blk0, col <= jnp.minimum(blk0 + SB, C - 1)],
                          axis=0).astype(BF16)
    brow = lax.broadcasted_iota(jnp.int32, (C, C), 0) // SB
    bcol = lax.broadcasted_iota(jnp.int32, (C, C), 1) // SB
    bdiff = brow - bcol
    sub = lax.broadcasted_iota(jnp.int32, (SB, LANES), 0)
    causal_cap = [jnp.where(sub >= s, 0.0, -jnp.inf).astype(F32) for s in range(SB)]

    def chunk_step(c, slot, hd):
        lanes = slice(hd * LANES, (hd + 1) * LANES)
        log_lb, log1m_lb, one_m_lb = log_lb_all[:, lanes], log1m_lb_all[:, lanes], one_m_lb_all[:, lanes]
        onorm = onorm_all[:, lanes]
        slot = hd * HGRN_GROUP + slot
        off = pl.multiple_of(c * C, C)
        q = q_ref[pl.ds(off, C), lanes]
        fp = f_ref[pl.ds(off, C), lanes]
        v = i_ref[pl.ds(off, C), lanes]
        g = g_ref[pl.ds(off, C), lanes]

        ls = jnp.minimum(fp, 0.0) - jnp.log(1.0 + jnp.exp(-jnp.abs(fp)))
        cc = log1m_lb + ls
        lf = jnp.maximum(log_lb, cc) + jnp.log(1.0 + jnp.exp(-jnp.abs(log_lb - cc)))
        kk = one_m_lb * jax.nn.sigmoid(-fp)

        p1 = lf.astype(BF16)
        r1 = lf - p1.astype(F32)
        p2 = r1.astype(BF16)
        p3 = (r1 - p2.astype(F32)).astype(BF16)
        cums = jnp.dot(tri, jnp.concatenate([p1, p2, p3], axis=0), preferred_element_type=F32)
        b, r, r_next = cums[:C], cums[C:2 * C], cums[2 * C:]
        bend = b[C - 1:C, :]
        b_scr[slot] = b
        k_scr[slot] = kk

        st_t = state_ref[hd]
        qe = q * jnp.exp(b)
        o = lax.dot_general(qe.astype(BF16), st_t.astype(BF16), _NT, preferred_element_type=F32)
        kd = kk * jnp.exp(bend - b)
        state_ref[hd] = st_t * jnp.exp(bend) + jnp.dot(v.T.astype(BF16), kd.astype(BF16),
                                                      preferred_element_type=F32)

        kt = kk * jnp.exp(r_next - b)
        gdec = jnp.exp(r_next - r)
        ql = q * jnp.exp(b - r)
        levels = [ql]
        for lvl in range(1, nb - 1):
            fac = jnp.concatenate([jnp.zeros((lvl * SB, LANES), F32), gdec[:C - lvl * SB]], axis=0)
            ql = ql * fac
            levels.append(ql)
        qs = jnp.concatenate(levels, axis=0).astype(BF16)
        rl = lax.dot_general(qs, kt.astype(BF16), _NT, preferred_element_type=F32)
        a = jnp.zeros((C, C), F32)
        for lvl in range(1, nb):
            a = a + jnp.where(bdiff == lvl, rl[(lvl - 1) * C:lvl * C, :], 0.0)
        o = o + jnp.dot(a.astype(BF16), v.astype(BF16), preferred_element_type=F32)

        diag = []
        for blk in range(nb):
            bq = b[blk * SB:(blk + 1) * SB]
            qq = q[blk * SB:(blk + 1) * SB]
            acc = jnp.zeros((SB, LANES), F32)
            for s in range(SB):
                rr = blk * SB + s
                dec = jnp.exp(jnp.minimum(bq - b_scr[slot, pl.ds(rr, 1), :], causal_cap[s]))
                a_ts = jnp.sum(qq * k_scr[slot, pl.ds(rr, 1), :] * dec, axis=-1, keepdims=True)
                acc = acc + a_ts * i_ref[pl.ds(off + rr, 1), :][:, lanes]
            diag.append(acc)
        o = o + jnp.concatenate(diag, axis=0)

        y = _rms_rows(o, onorm)
        o_ref[pl.ds(off, C), lanes] = (y * (g * jax.nn.sigmoid(g))).astype(o_ref.dtype)

    def chunk_group(grp, carry):
        for slot in range(HGRN_GROUP):
            for hd in range(heads):
                chunk_step(HGRN_GROUP * grp + slot, slot, hd)
        return carry

    lax.fori_loop(0, tile // (HGRN_GROUP * C), chunk_group, 0)


def _hgrn_mixer_core(proj, lb_logits, o_norm, B, S, layer, *, tile=512, heads=HGRN_HEADS_PER_STEP):
    nt = S // tile
    G = N_HEADS // heads
    w = heads * LANES
    blk = lambda sec: pl.BlockSpec((tile, w), lambda b, g, t, sec=sec: (b * nt + t, sec * G + g))
    return pl.pallas_call(
        functools.partial(_hgrn_kernel, layer=layer, tile=tile, chunk=HGRN_CHUNK, heads=heads),
        grid=(B, G, nt),
        in_specs=[blk(0), blk(1), blk(2), blk(3),
                  pl.BlockSpec((DEPTH, w), lambda b, g, t: (0, g)),
                  pl.BlockSpec((1, w), lambda b, g, t: (0, g))],
        out_specs=pl.BlockSpec((tile, w), lambda b, g, t: (b * nt + t, g)),
        out_shape=jax.ShapeDtypeStruct((B * S, D_MODEL), BF16),
        scratch_shapes=[pltpu.VMEM((heads, LANES, LANES), F32),
                        pltpu.VMEM((heads * HGRN_GROUP, HGRN_CHUNK, LANES), F32),
                        pltpu.VMEM((heads * HGRN_GROUP, HGRN_CHUNK, LANES), F32)],
        compiler_params=_params("parallel", "parallel", "arbitrary"),
        name="hgrn",
    )(proj, proj, proj, proj, lb_logits, o_norm.reshape(1, -1))


SB_CUMSUM_BLOCK = 256


def _sb_kernel(q_ref, k_ref, v_ref, o_ref, *score_refs, t, heads):
    cb = SB_CUMSUM_BLOCK
    qi = pl.program_id(2)
    row = lax.broadcasted_iota(jnp.int32, (t, t), 0)
    col = lax.broadcasted_iota(jnp.int32, (t, t), 1)
    strict = col < row
    jj = lax.broadcasted_iota(jnp.int32, (2 * cb, cb), 0) % cb
    ss = lax.broadcasted_iota(jnp.int32, (2 * cb, cb), 1)
    from_s2 = (jj >= ss).astype(BF16)

    def stream(hd):
        lanes = slice(hd * LANES, (hd + 1) * LANES)
        q = q_ref[:, lanes]

        def scores(ki):
            off = pl.multiple_of(ki * t, t)
            return lax.dot_general(q, k_ref[pl.ds(off, t), lanes], _NT, preferred_element_type=F32)

        def update(ki, zn_ref, acc, diagonal):
            off = pl.multiple_of(ki * t, t)
            tail = None
            new = None
            for j in reversed(range(t // cb)):
                cols = slice(j * cb, (j + 1) * cb)
                zn = zn_ref[:, cols]
                l1m = jnp.minimum(zn, 0.0) - jnp.log(1.0 + jnp.exp2(jnp.abs(zn) * (-LOG2E)))
                if diagonal:
                    l1m = jnp.where(strict[:, cols], l1m, 0.0)
                hi = l1m.astype(BF16)
                lo = (l1m - hi.astype(F32)).astype(BF16)
                x = jnp.dot(jnp.concatenate([hi, lo], axis=1), from_s2, preferred_element_type=F32)
                if tail is not None:
                    x = x + tail
                tail = x[:, 0:1]
                w = jnp.exp(x - zn)
                if diagonal:
                    w = jnp.where(strict[:, cols], w, 0.0)
                part = jnp.dot(w.astype(BF16), v_ref[pl.ds(off + j * cb, cb), lanes],
                               preferred_element_type=F32)
                new = part if new is None else new + part
            return acc * jnp.exp(tail) + new

        return (score_refs[2 * hd], score_refs[2 * hd + 1], scores, update, jnp.zeros((t, HEAD_DIM), F32))

    accs = _causal_sweep(qi, [stream(hd) for hd in range(heads)])
    for hd, acc in enumerate(accs):
        o_ref[:, hd * LANES:(hd + 1) * LANES] = acc.astype(o_ref.dtype)


def _sb_attn(qkv, B, S, *, t=512, heads=SB_HEADS_PER_STEP):
    nq = S // t
    G = N_HEADS // heads
    w = heads * LANES
    return pl.pallas_call(
        functools.partial(_sb_kernel, t=t, heads=heads),
        grid=(B, G, nq),
        in_specs=[pl.BlockSpec((t, w), lambda b, g, i: (b * nq + i, g)),
                  pl.BlockSpec((S, w), lambda b, g, i: (b, G + g)),
                  pl.BlockSpec((S, w), lambda b, g, i: (b, 2 * G + g))],
        out_specs=pl.BlockSpec((t, w), lambda b, g, i: (b * nq + i, g)),
        out_shape=jax.ShapeDtypeStruct((B * S, D_MODEL), BF16),
        scratch_shapes=[pltpu.VMEM((t, t), F32) for _ in range(2 * heads)],
        compiler_params=_params("parallel", "parallel", "arbitrary"),
        name="sb_attn",
    )(qkv, qkv, qkv)


MOBA_VETO = 2.0 ** 100


def _moba_kernel(q_ref, k_ref, v_ref, o_ref, kmean_ref, *score_refs, nblk, t, heads):
    bpt = t // MOBA_BLOCK
    qi = pl.program_id(2)

    @pl.when(qi == 0)
    def _():
        for hd in range(heads):
            k_all = k_ref[:, hd * LANES:(hd + 1) * LANES].astype(F32)
            kmean_ref[hd] = jnp.mean(k_all.reshape(nblk, MOBA_BLOCK, LANES), axis=1)

    blk = lax.broadcasted_iota(jnp.int32, (nblk, t), 0)
    own = qi * bpt + lax.broadcasted_iota(jnp.int32, (nblk, t), 1) // MOBA_BLOCK
    neg_inf = jnp.float32(-jnp.inf)
    lane_blk = lax.broadcasted_iota(jnp.int32, (t, LANES), 1)
    key_blk = lax.broadcasted_iota(jnp.int32, (t, LANES), 0) // MOBA_BLOCK

    def stream(hd):
        lanes = slice(hd * LANES, (hd + 1) * LANES)
        q = q_ref[:, lanes]

        gate = lax.dot_general(kmean_ref[hd], q.astype(F32), _NT, precision=lax.Precision.HIGHEST,
                               preferred_element_type=F32)
        gate = jnp.where(blk < own, gate, neg_inf)
        sel = (blk == own).astype(F32)
        for _ in range(MOBA_TOPK):
            mx = jnp.max(gate, axis=0, keepdims=True)
            first = jnp.min(jnp.where(gate == mx, blk, nblk), axis=0, keepdims=True)
            pick = jnp.logical_and(blk == first, mx > neg_inf)
            sel = jnp.where(pick, 1.0, sel)
            gate = jnp.where(pick, neg_inf, gate)

        veto = jnp.concatenate([sel - 1.0, jnp.zeros((LANES - nblk, t), F32)], axis=0)
        q_ext = jnp.concatenate([q, veto.T.astype(BF16)], axis=1)

        def scores(kc):
            off = pl.multiple_of(kc * t, t)
            hot = jnp.where(lane_blk == kc * bpt + key_blk, MOBA_VETO, 0.0).astype(BF16)
            k_ext = jnp.concatenate([k_ref[pl.ds(off, t), lanes], hot], axis=1)
            return lax.dot_general(q_ext, k_ext, _NT, preferred_element_type=F32)

        def update(kc, s_ref, state, diagonal):
            off = pl.multiple_of(kc * t, t)
            v_rows = lambda n: v_ref[pl.ds(off, n), lanes]
            if diagonal:
                return _softmax_update_diag(s_ref, state, v_rows, t)
            return _softmax_update(s_ref[...], state, v_rows(t))

        return (score_refs[2 * hd], score_refs[2 * hd + 1], scores, update, _softmax_init(t))

    states = _causal_sweep(qi, [stream(hd) for hd in range(heads)])
    for hd, state in enumerate(states):
        o_ref[:, hd * LANES:(hd + 1) * LANES] = _softmax_finish(state).astype(o_ref.dtype)


def _moba_attn(qkv, B, S, *, t=2 * MOBA_BLOCK, heads=SOFTMAX_HEADS_PER_STEP):
    assert t == 2 * MOBA_BLOCK
    nq = S // t
    nblk = S // MOBA_BLOCK
    G = N_HEADS // heads
    w = heads * LANES
    return pl.pallas_call(
        functools.partial(_moba_kernel, nblk=nblk, t=t, heads=heads),
        grid=(B, G, nq),
        in_specs=[pl.BlockSpec((t, w), lambda b, g, i: (b * nq + i, g)),
                  pl.BlockSpec((S, w), lambda b, g, i: (b, G + g)),
                  pl.BlockSpec((S, w), lambda b, g, i: (b, 2 * G + g))],
        out_specs=pl.BlockSpec((t, w), lambda b, g, i: (b * nq + i, g)),
        out_shape=jax.ShapeDtypeStruct((B * S, D_MODEL), BF16),
        scratch_shapes=[pltpu.VMEM((heads, nblk, LANES), F32)]
                       + [pltpu.VMEM((t, t), F32) for _ in range(2 * heads)],
        compiler_params=_params("parallel", "parallel", "arbitrary"),
        name="moba_attn",
    )(qkv, qkv, qkv)


def _qkv_proj(h, w_in, layer, q_scale):
    return _proj(h, w_in, BF16, layer=layer, scaled_cols=D_MODEL, scale=q_scale)


def _sb_mixer(h, B, S, w_in, layer=0):
    return _sb_attn(_qkv_proj(h, w_in, layer, -float(HEAD_DIM ** -0.5)), B, S)


def _moba_mixer(h, B, S, w_in, layer=0):
    return _moba_attn(_qkv_proj(h, w_in, layer, float(HEAD_DIM ** -0.5) * LOG2E), B, S)


def kernel(x, mla_w_in, mla_q_norm, mla_kv_norm, mla_w_uq, mla_w_ukv, mla_w_o, hgrn_w_in, hgrn_lb_logits, hgrn_o_norm, hgrn_w_o, sb_w_in, sb_w_o, moba_w_in, moba_w_o, ln_g, ln_b, mlp_w1, mlp_w2):
    B, S, D = x.shape
    assert D == D_MODEL and S % MOBA_BLOCK == 0 and S % 512 == 0
    h = x.reshape(B * S, D)
    hb = h
    n_mixers = 4
    for i in range(DEPTH):
        kind, slot = i % n_mixers, i // n_mixers
        if kind == 0:
            o = _mla_mixer(hb, B, S, mla_w_in[slot], mla_q_norm[slot], mla_kv_norm[slot],
                           mla_w_uq[slot], mla_w_ukv[slot])
            w_o = mla_w_o[slot]
        elif kind == 1:
            proj = _proj(hb, hgrn_w_in, F32, layer=slot)
            o = _hgrn_mixer_core(proj, hgrn_lb_logits, hgrn_o_norm[slot], B, S, i)
            w_o = hgrn_w_o[slot]
        elif kind == 2:
            o = _sb_mixer(hb, B, S, sb_w_in, slot)
            w_o = sb_w_o[slot]
        else:
            o = _moba_mixer(hb, B, S, moba_w_in, slot)
            w_o = moba_w_o[slot]
        h, hb = _proj_res_ln(o, w_o.astype(BF16), h, ln_g[i, 0], ln_b[i, 0])
        a, w2 = _proj(hb, mlp_w1, BF16, layer=i, act="relu2", cast_stack=mlp_w2)
        h, hb = _proj_res_ln(a, w2, h, ln_g[i, 1], ln_b[i, 1])
    return h.reshape(B, S, D)
```

```python
import functools

import jax
import jax.numpy as jnp
from jax import lax
from jax.experimental import pallas as pl
from jax.experimental.pallas import tpu as pltpu

F32 = jnp.float32
BF16 = jnp.bfloat16

D_MODEL = 2048
DEPTH = 4
N_HEADS = 16
HEAD_DIM = 128
MLA_Q_LORA = 512
MLA_KV_LORA = 512
MLA_NOPE = 128
MLA_ROPE = 64
MLA_V = 128
ROPE_THETA = 10000.0
HGRN_CHUNK = 64
HGRN_SUB = 8
HGRN_GROUP = 8
HGRN_HEADS_PER_STEP = 4
SB_HEADS_PER_STEP = 4
SOFTMAX_HEADS_PER_STEP = 4
MOBA_BLOCK = 256
MOBA_TOPK = 3
ALPHA = float((2 * DEPTH) ** 0.25)
LN_EPS = 1e-5
RMS_EPS = 1e-6
LN_ROWS = 16

V7X_VMEM_BYTES = 64 * 1024 * 1024
VMEM_LIMIT = V7X_VMEM_BYTES - 8 * 1024 * 1024
PROJ_VMEM_BUDGET = VMEM_LIMIT - 8 * 1024 * 1024
LANES = 128
MASKED = -1e30
LOG2E = 1.4426950408889634

_NT = (((1,), (1,)), ((), ()))


def _params(*sem):
    return pltpu.CompilerParams(dimension_semantics=sem, vmem_limit_bytes=VMEM_LIMIT)


def _layer_norm_rows(y, g, b):
    mu = jnp.mean(y, axis=-1, keepdims=True)
    d = y - mu
    var = jnp.mean(d * d, axis=-1, keepdims=True)
    return d * lax.rsqrt(var + LN_EPS) * g + b


def _rms_rows(x, g):
    return x * lax.rsqrt(jnp.mean(x * x, axis=-1, keepdims=True) + RMS_EPS) * g


def _proj_kernel(*refs, act, scaled_tiles, scale, side_cast):
    if side_cast:
        x_ref, w_ref, c_ref, o_ref, co_ref, wbf_ref = refs
        co_ref[...] = c_ref[...].astype(BF16)
    else:
        x_ref, w_ref, o_ref, wbf_ref = refs

    @pl.when(pl.program_id(1) == 0)
    def _():
        wbf_ref[...] = w_ref[...].astype(BF16)

    acc = jnp.dot(x_ref[...].astype(BF16), wbf_ref[...], preferred_element_type=F32)
    if act == "relu2":
        r = jnp.maximum(acc, 0.0)
        acc = r * r
    if scaled_tiles:
        acc = acc * jnp.where(pl.program_id(0) < scaled_tiles, scale, 1.0)
    o_ref[...] = acc.astype(o_ref.dtype)


def _proj_row_tile(M, K, N, tn, x_dtype, w_dtype, out_dtype, cast_stack):
    size = lambda dt: jnp.dtype(dt).itemsize
    fixed = 2 * K * tn * size(w_dtype) + K * tn * size(BF16)
    for tm in (2048, 1024, 512, 256):
        if M % tm:
            continue
        need = fixed + 2 * tm * (K * size(x_dtype) + tn * size(out_dtype))
        if cast_stack is not None:
            slab = cast_stack.shape[1] * cast_stack.shape[2] // ((N // tn) * (M // tm))
            need += 2 * slab * (size(cast_stack.dtype) + size(BF16))
        if need <= PROJ_VMEM_BUDGET:
            return tm
    raise ValueError("no row tile fits")


def _proj(x, w, out_dtype, *, layer=0, tn=1024, act=None, scaled_cols=0, scale=1.0, cast_stack=None):
    M, K = x.shape
    N = w.shape[-1]
    tn = min(tn, N)
    tm = _proj_row_tile(M, K, N, tn, x.dtype, w.dtype, out_dtype, cast_stack)
    assert scaled_cols % tn == 0 and N % tn == 0
    ni = M // tm
    if w.ndim == 3:
        w_spec = pl.BlockSpec((None, K, tn), lambda j, i: (layer, 0, j))
    else:
        w_spec = pl.BlockSpec((K, tn), lambda j, i: (0, j))
    in_specs = [pl.BlockSpec((tm, K), lambda j, i: (i, 0)), w_spec]
    out_specs = [pl.BlockSpec((tm, tn), lambda j, i: (i, j))]
    out_shape = [jax.ShapeDtypeStruct((M, N), out_dtype)]
    operands = [x, w]
    if cast_stack is not None:
        _, R, C = cast_stack.shape
        steps = (N // tn) * ni
        assert R % steps == 0
        in_specs.append(pl.BlockSpec((None, R // steps, C), lambda j, i: (layer, j * ni + i, 0)))
        out_specs.append(pl.BlockSpec((R // steps, C), lambda j, i: (j * ni + i, 0)))
        out_shape.append(jax.ShapeDtypeStruct((R, C), BF16))
        operands.append(cast_stack)
    outs = pl.pallas_call(
        functools.partial(_proj_kernel, act=act, scaled_tiles=scaled_cols // tn, scale=scale,
                          side_cast=cast_stack is not None),
        grid=(N // tn, ni),
        in_specs=in_specs,
        out_specs=out_specs,
        out_shape=out_shape,
        scratch_shapes=[pltpu.VMEM((K, tn), BF16)],
        compiler_params=_params("parallel", "arbitrary"),
        name="proj",
    )(*operands)
    return outs[0] if cast_stack is None else outs


def _proj_res_ln_kernel(x_ref, w_ref, h_ref, g_ref, b_ref, o_ref, ob_ref, acc_ref, *, nk):
    k = pl.program_id(1)

    def part():
        return jnp.dot(x_ref[...], w_ref[...], preferred_element_type=F32)

    def finish(y):
        g, b = g_ref[...], b_ref[...]
        for r in range(0, y.shape[0], LN_ROWS):
            rows = slice(r, r + LN_ROWS)
            out = _layer_norm_rows(ALPHA * h_ref[rows, :] + y[rows], g, b)
            o_ref[rows, :] = out
            ob_ref[rows, :] = out.astype(BF16)

    if nk == 1:
        finish(part())
    else:
        @pl.when(k == 0)
        def _():
            acc_ref[...] = part()

        @pl.when(jnp.logical_and(k > 0, k < nk - 1))
        def _():
            acc_ref[...] += part()

        @pl.when(k == nk - 1)
        def _():
            finish(acc_ref[...] + part())


def _proj_res_ln(x, w, h, g, b, *, layer=0, tm=512, tk=2048):
    M, K = x.shape
    N = w.shape[-1]
    tm, tk = min(tm, M), min(tk, K)
    nk = K // tk
    if w.ndim == 3:
        w_spec = pl.BlockSpec((None, tk, N), lambda i, k: (layer, k, 0))
    else:
        w_spec = pl.BlockSpec((tk, N), lambda i, k: (k, 0))
    return pl.pallas_call(
        functools.partial(_proj_res_ln_kernel, nk=nk),
        grid=(M // tm, nk),
        in_specs=[pl.BlockSpec((tm, tk), lambda i, k: (i, k)),
                  w_spec,
                  pl.BlockSpec((tm, N), lambda i, k: (i, 0)),
                  pl.BlockSpec((1, N), lambda i, k: (0, 0)),
                  pl.BlockSpec((1, N), lambda i, k: (0, 0))],
        out_specs=[pl.BlockSpec((tm, N), lambda i, k: (i, 0)),
                   pl.BlockSpec((tm, N), lambda i, k: (i, 0))],
        out_shape=[jax.ShapeDtypeStruct((M, N), F32), jax.ShapeDtypeStruct((M, N), BF16)],
        scratch_shapes=[pltpu.VMEM((tm, N), F32)],
        compiler_params=_params("parallel", "arbitrary"),
        name="proj_res_ln",
    )(x, w, h, g.reshape(1, N), b.reshape(1, N))


def _mla_in_kernel(h_ref, w_ref, qg_ref, kvg_ref, ct_ref, st_ref, cq_ref, ckv_ref, kr_ref):
    acc = jnp.dot(h_ref[...].astype(BF16), w_ref[...], preferred_element_type=F32)
    ql, kvl = MLA_Q_LORA, MLA_KV_LORA
    cq_ref[...] = _rms_rows(acc[:, :ql], qg_ref[...]).astype(cq_ref.dtype)
    ckv_ref[...] = _rms_rows(acc[:, ql:ql + kvl], kvg_ref[...]).astype(ckv_ref.dtype)
    a = acc[:, ql + kvl:ql + kvl + LANES]
    a_sw = acc[:, ql + kvl + LANES:]
    kr_ref[...] = (a * ct_ref[...] + a_sw * st_ref[...]).astype(kr_ref.dtype)


def _mla_in(h, w_ext, qg, kvg, ct, st, S, *, tm=1024):
    M, K = h.shape
    N = w_ext.shape[1]
    tm = min(tm, S)
    ns = S // tm
    return pl.pallas_call(
        _mla_in_kernel,
        grid=(M // tm,),
        in_specs=[pl.BlockSpec((tm, K), lambda i: (i, 0)),
                  pl.BlockSpec((K, N), lambda i: (0, 0)),
                  pl.BlockSpec((1, MLA_Q_LORA), lambda i: (0, 0)),
                  pl.BlockSpec((1, MLA_KV_LORA), lambda i: (0, 0)),
                  pl.BlockSpec((tm, LANES), lambda i: (i % ns, 0)),
                  pl.BlockSpec((tm, LANES), lambda i: (i % ns, 0))],
        out_specs=[pl.BlockSpec((tm, MLA_Q_LORA), lambda i: (i, 0)),
                   pl.BlockSpec((tm, MLA_KV_LORA), lambda i: (i, 0)),
                   pl.BlockSpec((tm, LANES), lambda i: (i, 0))],
        out_shape=[jax.ShapeDtypeStruct((M, MLA_Q_LORA), BF16),
                   jax.ShapeDtypeStruct((M, MLA_KV_LORA), BF16),
                   jax.ShapeDtypeStruct((M, LANES), BF16)],
        compiler_params=_params("parallel"),
        name="mla_in",
    )(h, w_ext, qg.reshape(1, -1), kvg.reshape(1, -1), ct, st)


def _mla_uq_kernel(cq_ref, wm_ref, ws_ref, ct_ref, st_ref, q_ref, *, heads, scale):
    x = cq_ref[...]
    a = jnp.dot(x, wm_ref[...], preferred_element_type=F32)
    a_sw = jnp.dot(x, ws_ref[...], preferred_element_type=F32)
    ct = ct_ref[...]
    st = st_ref[...]
    for hh in range(heads):
        lo = hh * 2 * LANES
        q_ref[:, lo:lo + LANES] = (a[:, lo:lo + LANES] * scale).astype(q_ref.dtype)
        rot = a[:, lo + LANES:lo + 2 * LANES] * ct + a_sw[:, hh * LANES:(hh + 1) * LANES] * st
        q_ref[:, lo + LANES:lo + 2 * LANES] = (rot * scale).astype(q_ref.dtype)


def _mla_uq(cq, w_main, w_sw, ct, st, S, scale, *, tm=1024, heads_per_step=8):
    M, K = cq.shape
    hp = heads_per_step
    tm = min(tm, S)
    ns = S // tm
    return pl.pallas_call(
        functools.partial(_mla_uq_kernel, heads=hp, scale=scale),
        grid=(M // tm, N_HEADS // hp),
        in_specs=[pl.BlockSpec((tm, K), lambda i, j: (i, 0)),
                  pl.BlockSpec((K, hp * 2 * LANES), lambda i, j: (0, j)),
                  pl.BlockSpec((K, hp * LANES), lambda i, j: (0, j)),
                  pl.BlockSpec((tm, LANES), lambda i, j: (i % ns, 0)),
                  pl.BlockSpec((tm, LANES), lambda i, j: (i % ns, 0))],
        out_specs=pl.BlockSpec((tm, hp * 2 * LANES), lambda i, j: (i, j)),
        out_shape=jax.ShapeDtypeStruct((M, N_HEADS * 2 * LANES), BF16),
        compiler_params=_params("parallel", "parallel"),
        name="mla_uq",
    )(cq, w_main, w_sw, ct, st)


def _causal_sweep(qi, streams):
    def fill(which, ki):
        for stream in streams:
            stream[which][...] = stream[2](ki)

    def step(which, ki, states, diagonal):
        return tuple(stream[3](ki, stream[which], st, diagonal) for stream, st in zip(streams, states))

    fill(0, 0)

    def pair(p, states):
        k0 = 2 * p
        fill(1, k0 + 1)
        states = step(0, k0, states, False)
        fill(0, k0 + 2)
        return step(1, k0 + 1, states, False)

    states = lax.fori_loop(0, qi // 2, pair, tuple(stream[4] for stream in streams))

    def odd(states):
        fill(1, qi)
        states = step(0, qi - 1, states, False)
        return step(1, qi, states, True)

    def even(states):
        return step(0, qi, states, True)

    return lax.cond(qi % 2 == 1, odd, even, states)


def _softmax_update(s, state, v):
    m, acc = state
    m_new = jnp.maximum(m, jnp.max(s, axis=-1, keepdims=True))
    alpha = jnp.exp2(m - m_new)
    p = jnp.exp2(s - m_new).astype(BF16)
    v_ones = jnp.concatenate([v, jnp.ones_like(v)], axis=1)
    acc = alpha * acc + jnp.dot(p, v_ones, preferred_element_type=F32)
    return m_new, acc


def _softmax_update_diag(s_ref, state, v_rows, t):
    h = t // 2
    m, acc = state
    keep = (lax.broadcasted_iota(jnp.int32, (h, h), 1) <= lax.broadcasted_iota(jnp.int32, (h, h), 0))
    s_top = jnp.where(keep, s_ref[0:h, 0:h], -jnp.inf)
    top = _softmax_update(s_top, (m[:h], acc[:h]), v_rows(h))
    s_bot = jnp.concatenate([s_ref[h:t, 0:h], jnp.where(keep, s_ref[h:t, h:t], -jnp.inf)], axis=1)
    bot = _softmax_update(s_bot, (m[h:], acc[h:]), v_rows(t))
    return jnp.concatenate([top[0], bot[0]], axis=0), jnp.concatenate([top[1], bot[1]], axis=0)


def _softmax_init(t):
    return (jnp.full((t, 1), MASKED, F32), jnp.zeros((t, 2 * LANES), F32))


def _softmax_finish(state):
    _, acc = state
    return acc[:, :LANES] / acc[:, LANES:]


def _mla_attn_kernel(q_ref, kn_ref, kr_ref, v_ref, o_ref, *score_refs, t, heads):
    qi = pl.program_id(2)

    def stream(hd):
        lanes = slice(hd * LANES, (hd + 1) * LANES)
        q = q_ref[:, 2 * hd * LANES:2 * (hd + 1) * LANES]

        def scores(ki):
            off = pl.multiple_of(ki * t, t)
            k = jnp.concatenate([kn_ref[pl.ds(off, t), lanes], kr_ref[pl.ds(off, t), :]], axis=1)
            return lax.dot_general(q, k, _NT, preferred_element_type=F32)

        def update(ki, s_ref, state, diagonal):
            off = pl.multiple_of(ki * t, t)
            v_rows = lambda n: v_ref[pl.ds(off, n), lanes]
            if diagonal:
                return _softmax_update_diag(s_ref, state, v_rows, t)
            return _softmax_update(s_ref[...], state, v_rows(t))

        return (score_refs[2 * hd], score_refs[2 * hd + 1], scores, update, _softmax_init(t))

    states = _causal_sweep(qi, [stream(hd) for hd in range(heads)])
    for hd, state in enumerate(states):
        o_ref[:, hd * LANES:(hd + 1) * LANES] = _softmax_finish(state).astype(o_ref.dtype)


def _mla_attn(q, kv, kr, B, S, *, t=512, heads=SOFTMAX_HEADS_PER_STEP):
    nq = S // t
    G = N_HEADS // heads
    w = heads * LANES
    return pl.pallas_call(
        functools.partial(_mla_attn_kernel, t=t, heads=heads),
        grid=(B, G, nq),
        in_specs=[pl.BlockSpec((t, 2 * w), lambda b, g, i: (b * nq + i, g)),
                  pl.BlockSpec((S, w), lambda b, g, i: (b, g)),
                  pl.BlockSpec((S, LANES), lambda b, g, i: (b, 0)),
                  pl.BlockSpec((S, w), lambda b, g, i: (b, G + g))],
        out_specs=pl.BlockSpec((t, w), lambda b, g, i: (b * nq + i, g)),
        out_shape=jax.ShapeDtypeStruct((B * S, N_HEADS * MLA_V), BF16),
        scratch_shapes=[pltpu.VMEM((t, t), F32) for _ in range(2 * heads)],
        compiler_params=_params("parallel", "parallel", "arbitrary"),
        name="mla_attn",
    )(q, kv, kr, kv)


def _rope_tables(S):
    half = MLA_ROPE // 2
    inv = 1.0 / (ROPE_THETA ** (jnp.arange(0, MLA_ROPE, 2, dtype=F32) / MLA_ROPE))
    ang = jnp.arange(S, dtype=F32)[:, None] * inv[None, :]
    cos, sin = jnp.cos(ang), jnp.sin(ang)
    zeros = jnp.zeros((S, LANES - 2 * half), F32)
    return (jnp.concatenate([cos, cos, zeros], axis=1), jnp.concatenate([-sin, sin, zeros], axis=1))


def _mla_weights(w_in, w_uq, w_ukv):
    D = w_in.shape[0]
    half = MLA_ROPE // 2
    base = MLA_Q_LORA + MLA_KV_LORA
    x1, x2 = w_in[:, base:base + half], w_in[:, base + half:base + 2 * half]
    pad = jnp.zeros((D, LANES - 2 * half), w_in.dtype)
    w_in_ext = jnp.concatenate([w_in[:, :base], x1, x2, pad, x2, x1, pad], axis=1).astype(BF16)

    wq = w_uq.reshape(MLA_Q_LORA, N_HEADS, MLA_NOPE + MLA_ROPE)
    nope, r1, r2 = wq[..., :MLA_NOPE], wq[..., MLA_NOPE:MLA_NOPE + half], wq[..., MLA_NOPE + half:]
    padq = jnp.zeros((MLA_Q_LORA, N_HEADS, LANES - 2 * half), w_uq.dtype)
    w_main = jnp.concatenate([nope, r1, r2, padq], axis=-1).reshape(MLA_Q_LORA, N_HEADS * 2 * LANES).astype(BF16)
    w_sw = jnp.concatenate([r2, r1, padq], axis=-1).reshape(MLA_Q_LORA, N_HEADS * LANES).astype(BF16)

    wkv = w_ukv.reshape(MLA_KV_LORA, N_HEADS, MLA_NOPE + MLA_V)
    w_kv = jnp.concatenate([wkv[..., :MLA_NOPE].reshape(MLA_KV_LORA, -1),
                            wkv[..., MLA_NOPE:].reshape(MLA_KV_LORA, -1)], axis=1).astype(BF16)
    return w_in_ext, w_main, w_sw, w_kv


def _mla_mixer(h, B, S, w_in, q_norm, kv_norm, w_uq, w_ukv):
    w_in_ext, w_main, w_sw, w_kv = _mla_weights(w_in, w_uq, w_ukv)
    ct, st = _rope_tables(S)
    cq, ckv, kr = _mla_in(h, w_in_ext, q_norm, kv_norm, ct, st, S)
    scale = float((MLA_NOPE + MLA_ROPE) ** -0.5) * LOG2E
    q = _mla_uq(cq, w_main, w_sw, ct, st, S, scale)
    kv = _proj(ckv, w_kv, BF16)
    return _mla_attn(q, kv, kr, B, S)


def _hgrn_kernel(q_ref, f_ref, i_ref, g_ref, lbl_ref, on_ref, o_ref, state_ref, b_scr, k_scr,
                 *, layer, tile, chunk, heads):
    C, SB = chunk, HGRN_SUB
    nb = C // SB

    @pl.when(pl.program_id(2) == 0)
    def _():
        state_ref[...] = jnp.zeros_like(state_ref)

    lg = lbl_ref[...]
    e = jnp.exp(lg - jnp.max(lg, axis=0, keepdims=True))
    p = e / jnp.sum(e, axis=0, keepdims=True)
    cs = p[0:1]
    for r in range(1, layer + 1):
        cs = cs + p[r:r + 1]
    lb_all = cs - p[0:1]
    log_lb_all = jnp.log(lb_all)
    log1m_lb_all = jnp.log1p(-lb_all)
    one_m_lb_all = 1.0 - lb_all
    onorm_all = on_ref[...]

    row = lax.broadcasted_iota(jnp.int32, (C, 3 * C), 0)
    col = lax.broadcasted_iota(jnp.int32, (C, 3 * C), 1) % C
    blk0 = (row // SB) * SB
    tri = jnp.concatenate([col <= row, col <= blk0, col <= jnp.minimum(blk0 + SB, C - 1)],
                          axis=0).astype(BF16)
    brow = lax.broadcasted_iota(jnp.int32, (C, C), 0) // SB
    bcol = lax.broadcasted_iota(jnp.int32, (C, C), 1) // SB
    bdiff = brow - bcol
    sub = lax.broadcasted_iota(jnp.int32, (SB, LANES), 0)
    causal_cap = [jnp.where(sub >= s, 0.0, -jnp.inf).astype(F32) for s in range(SB)]

    def chunk_step(c, slot, hd):
        lanes = slice(hd * LANES, (hd + 1) * LANES)
        log_lb, log1m_lb, one_m_lb = log_lb_all[:, lanes], log1m_lb_all[:, lanes], one_m_lb_all[:, lanes]
        onorm = onorm_all[:, lanes]
        slot = hd * HGRN_GROUP + slot
        off = pl.multiple_of(c * C, C)
        q = q_ref[pl.ds(off, C), lanes]
        fp = f_ref[pl.ds(off, C), lanes]
        v = i_ref[pl.ds(off, C), lanes]
        g = g_ref[pl.ds(off, C), lanes]

        ls = jnp.minimum(fp, 0.0) - jnp.log(1.0 + jnp.exp(-jnp.abs(fp)))
        cc = log1m_lb + ls
        lf = jnp.maximum(log_lb, cc) + jnp.log(1.0 + jnp.exp(-jnp.abs(log_lb - cc)))
        kk = one_m_lb * jax.nn.sigmoid(-fp)

        p1 = lf.astype(BF16)
        r1 = lf - p1.astype(F32)
        p2 = r1.astype(BF16)
        p3 = (r1 - p2.astype(F32)).astype(BF16)
        cums = jnp.dot(tri, jnp.concatenate([p1, p2, p3], axis=0), preferred_element_type=F32)
        b, r, r_next = cums[:C], cums[C:2 * C], cums[2 * C:]
        bend = b[C - 1:C, :]
        b_scr[slot] = b
        k_scr[slot] = kk

        st_t = state_ref[hd]
        qe = q * jnp.exp(b)
        o = lax.dot_general(qe.astype(BF16), st_t.astype(BF16), _NT, preferred_element_type=F32)
        kd = kk * jnp.exp(bend - b)
        state_ref[hd] = st_t * jnp.exp(bend) + jnp.dot(v.T.astype(BF16), kd.astype(BF16),
                                                      preferred_element_type=F32)

        kt = kk * jnp.exp(r_next - b)
        gdec = jnp.exp(r_next - r)
        ql = q * jnp.exp(b - r)
        levels = [ql]
        for lvl in range(1, nb - 1):
            fac = jnp.concatenate([jnp.zeros((lvl * SB, LANES), F32), gdec[:C - lvl * SB]], axis=0)
            ql = ql * fac
            levels.append(ql)
        qs = jnp.concatenate(levels, axis=0).astype(BF16)
        rl = lax.dot_general(qs, kt.astype(BF16), _NT, preferred_element_type=F32)
        a = jnp.zeros((C, C), F32)
        for lvl in range(1, nb):
            a = a + jnp.where(bdiff == lvl, rl[(lvl - 1) * C:lvl * C, :], 0.0)
        o = o + jnp.dot(a.astype(BF16), v.astype(BF16), preferred_element_type=F32)

        diag = []
        for blk in range(nb):
            bq = b[blk * SB:(blk + 1) * SB]
            qq = q[blk * SB:(blk + 1) * SB]
            acc = jnp.zeros((SB, LANES), F32)
            for s in range(SB):
                rr = blk * SB + s
                dec = jnp.exp(jnp.minimum(bq - b_scr[slot, pl.ds(rr, 1), :], causal_cap[s]))
                a_ts = jnp.sum(qq * k_scr[slot, pl.ds(rr, 1), :] * dec, axis=-1, keepdims=True)
                acc = acc + a_ts * i_ref[pl.ds(off + rr, 1), :][:, lanes]
            diag.append(acc)
        o = o + jnp.concatenate(diag, axis=0)

        y = _rms_rows(o, onorm)
        o_ref[pl.ds(off, C), lanes] = (y * (g * jax.nn.sigmoid(g))).astype(o_ref.dtype)

    def chunk_group(grp, carry):
        for slot in range(HGRN_GROUP):
            for hd in range(heads):
                chunk_step(HGRN_GROUP * grp + slot, slot, hd)
        return carry

    lax.fori_loop(0, tile // (HGRN_GROUP * C), chunk_group, 0)


def _hgrn_mixer_core(proj, lb_logits, o_norm, B, S, layer, *, tile=512, heads=HGRN_HEADS_PER_STEP):
    nt = S // tile
    G = N_HEADS // heads
    w = heads * LANES
    blk = lambda sec: pl.BlockSpec((tile, w), lambda b, g, t, sec=sec: (b * nt + t, sec * G + g))
    return pl.pallas_call(
        functools.partial(_hgrn_kernel, layer=layer, tile=tile, chunk=HGRN_CHUNK, heads=heads),
        grid=(B, G, nt),
        in_specs=[blk(0), blk(1), blk(2), blk(3),
                  pl.BlockSpec((DEPTH, w), lambda b, g, t: (0, g)),
                  pl.BlockSpec((1, w), lambda b, g, t: (0, g))],
        out_specs=pl.BlockSpec((tile, w), lambda b, g, t: (b * nt + t, g)),
        out_shape=jax.ShapeDtypeStruct((B * S, D_MODEL), BF16),
        scratch_shapes=[pltpu.VMEM((heads, LANES, LANES), F32),
                        pltpu.VMEM((heads * HGRN_GROUP, HGRN_CHUNK, LANES), F32),
                        pltpu.VMEM((heads * HGRN_GROUP, HGRN_CHUNK, LANES), F32)],
        compiler_params=_params("parallel", "parallel", "arbitrary"),
        name="hgrn",
    )(proj, proj, proj, proj, lb_logits, o_norm.reshape(1, -1))


SB_CUMSUM_BLOCK = 256


def _sb_kernel(q_ref, k_ref, v_ref, o_ref, *score_refs, t, heads):
    cb = SB_CUMSUM_BLOCK
    qi = pl.program_id(2)
    row = lax.broadcasted_iota(jnp.int32, (t, t), 0)
    col = lax.broadcasted_iota(jnp.int32, (t, t), 1)
    strict = col < row
    jj = lax.broadcasted_iota(jnp.int32, (2 * cb, cb), 0) % cb
    ss = lax.broadcasted_iota(jnp.int32, (2 * cb, cb), 1)
    from_s2 = (jj >= ss).astype(BF16)

    def stream(hd):
        lanes = slice(hd * LANES, (hd + 1) * LANES)
        q = q_ref[:, lanes]

        def scores(ki):
            off = pl.multiple_of(ki * t, t)
            return lax.dot_general(q, k_ref[pl.ds(off, t), lanes], _NT, preferred_element_type=F32)

        def update(ki, zn_ref, acc, diagonal):
            off = pl.multiple_of(ki * t, t)
            tail = None
            new = None
            for j in reversed(range(t // cb)):
                cols = slice(j * cb, (j + 1) * cb)
                zn = zn_ref[:, cols]
                l1m = jnp.minimum(zn, 0.0) - jnp.log(1.0 + jnp.exp2(jnp.abs(zn) * (-LOG2E)))
                if diagonal:
                    l1m = jnp.where(strict[:, cols], l1m, 0.0)
                hi = l1m.astype(BF16)
                lo = (l1m - hi.astype(F32)).astype(BF16)
                x = jnp.dot(jnp.concatenate([hi, lo], axis=1), from_s2, preferred_element_type=F32)
                if tail is not None:
                    x = x + tail
                tail = x[:, 0:1]
                w = jnp.exp(x - zn)
                if diagonal:
                    w = jnp.where(strict[:, cols], w, 0.0)
                part = jnp.dot(w.astype(BF16), v_ref[pl.ds(off + j * cb, cb), lanes],
                               preferred_element_type=F32)
                new = part if new is None else new + part
            return acc * jnp.exp(tail) + new

        return (score_refs[2 * hd], score_refs[2 * hd + 1], scores, update, jnp.zeros((t, HEAD_DIM), F32))

    accs = _causal_sweep(qi, [stream(hd) for hd in range(heads)])
    for hd, acc in enumerate(accs):
        o_ref[:, hd * LANES:(hd + 1) * LANES] = acc.astype(o_ref.dtype)


def _sb_attn(qkv, B, S, *, t=512, heads=SB_HEADS_PER_STEP):
    nq = S // t
    G = N_HEADS // heads
    w = heads * LANES
    return pl.pallas_call(
        functools.partial(_sb_kernel, t=t, heads=heads),
        grid=(B, G, nq),
        in_specs=[pl.BlockSpec((t, w), lambda b, g, i: (b * nq + i, g)),
                  pl.BlockSpec((S, w), lambda b, g, i: (b, G + g)),
                  pl.BlockSpec((S, w), lambda b, g, i: (b, 2 * G + g))],
        out_specs=pl.BlockSpec((t, w), lambda b, g, i: (b * nq + i, g)),
        out_shape=jax.ShapeDtypeStruct((B * S, D_MODEL), BF16),
        scratch_shapes=[pltpu.VMEM((t, t), F32) for _ in range(2 * heads)],
        compiler_params=_params("parallel", "parallel", "arbitrary"),
        name="sb_attn",
    )(qkv, qkv, qkv)


MOBA_VETO = 2.0 ** 100


def _moba_kernel(q_ref, k_ref, v_ref, o_ref, kmean_ref, *score_refs, nblk, t, heads):
    bpt = t // MOBA_BLOCK
    qi = pl.program_id(2)

    @pl.when(qi == 0)
    def _():
        for hd in range(heads):
            k_all = k_ref[:, hd * LANES:(hd + 1) * LANES].astype(F32)
            kmean_ref[hd] = jnp.mean(k_all.reshape(nblk, MOBA_BLOCK, LANES), axis=1)

    blk = lax.broadcasted_iota(jnp.int32, (nblk, t), 0)
    own = qi * bpt + lax.broadcasted_iota(jnp.int32, (nblk, t), 1) // MOBA_BLOCK
    neg_inf = jnp.float32(-jnp.inf)
    lane_blk = lax.broadcasted_iota(jnp.int32, (t, LANES), 1)
    key_blk = lax.broadcasted_iota(jnp.int32, (t, LANES), 0) // MOBA_BLOCK

    def stream(hd):
        lanes = slice(hd * LANES, (hd + 1) * LANES)
        q = q_ref[:, lanes]

        gate = lax.dot_general(kmean_ref[hd], q.astype(F32), _NT, precision=lax.Precision.HIGHEST,
                               preferred_element_type=F32)
        gate = jnp.where(blk < own, gate, neg_inf)
        sel = (blk == own).astype(F32)
        for _ in range(MOBA_TOPK):
            mx = jnp.max(gate, axis=0, keepdims=True)
            first = jnp.min(jnp.where(gate == mx, blk, nblk), axis=0, keepdims=True)
            pick = jnp.logical_and(blk == first, mx > neg_inf)
            sel = jnp.where(pick, 1.0, sel)
            gate = jnp.where(pick, neg_inf, gate)

        veto = jnp.concatenate([sel - 1.0, jnp.zeros((LANES - nblk, t), F32)], axis=0)
        q_ext = jnp.concatenate([q, veto.T.astype(BF16)], axis=1)

        def scores(kc):
            off = pl.multiple_of(kc * t, t)
            hot = jnp.where(lane_blk == kc * bpt + key_blk, MOBA_VETO, 0.0).astype(BF16)
            k_ext = jnp.concatenate([k_ref[pl.ds(off, t), lanes], hot], axis=1)
            return lax.dot_general(q_ext, k_ext, _NT, preferred_element_type=F32)

        def update(kc, s_ref, state, diagonal):
            off = pl.multiple_of(kc * t, t)
            v_rows = lambda n: v_ref[pl.ds(off, n), lanes]
            if diagonal:
                return _softmax_update_diag(s_ref, state, v_rows, t)
            return _softmax_update(s_ref[...], state, v_rows(t))

        return (score_refs[2 * hd], score_refs[2 * hd + 1], scores, update, _softmax_init(t))

    states = _causal_sweep(qi, [stream(hd) for hd in range(heads)])
    for hd, state in enumerate(states):
        o_ref[:, hd * LANES:(hd + 1) * LANES] = _softmax_finish(state).astype(o_ref.dtype)


def _moba_attn(qkv, B, S, *, t=2 * MOBA_BLOCK, heads=SOFTMAX_HEADS_PER_STEP):
    assert t == 2 * MOBA_BLOCK
    nq = S // t
    nblk = S // MOBA_BLOCK
    G = N_HEADS // heads
    w = heads * LANES
    return pl.pallas_call(
        functools.partial(_moba_kernel, nblk=nblk, t=t, heads=heads),
        grid=(B, G, nq),
        in_specs=[pl.BlockSpec((t, w), lambda b, g, i: (b * nq + i, g)),
                  pl.BlockSpec((S, w), lambda b, g, i: (b, G + g)),
                  pl.BlockSpec((S, w), lambda b, g, i: (b, 2 * G + g))],
        out_specs=pl.BlockSpec((t, w), lambda b, g, i: (b * nq + i, g)),
        out_shape=jax.ShapeDtypeStruct((B * S, D_MODEL), BF16),
        scratch_shapes=[pltpu.VMEM((heads, nblk, LANES), F32)]
                       + [pltpu.VMEM((t, t), F32) for _ in range(2 * heads)],
        compiler_params=_params("parallel", "parallel", "arbitrary"),
        name="moba_attn",
    )(qkv, qkv, qkv)


def _qkv_proj(h, w_in, layer, q_scale):
    return _proj(h, w_in, BF16, layer=layer, scaled_cols=D_MODEL, scale=q_scale)


def _sb_mixer(h, B, S, w_in, layer=0):
    return _sb_attn(_qkv_proj(h, w_in, layer, -float(HEAD_DIM ** -0.5)), B, S)


def _moba_mixer(h, B, S, w_in, layer=0):
    return _moba_attn(_qkv_proj(h, w_in, layer, float(HEAD_DIM ** -0.5) * LOG2E), B, S)


def kernel(x, mla_w_in, mla_q_norm, mla_kv_norm, mla_w_uq, mla_w_ukv, mla_w_o, hgrn_w_in, hgrn_lb_logits, hgrn_o_norm, hgrn_w_o, sb_w_in, sb_w_o, moba_w_in, moba_w_o, ln_g, ln_b, mlp_w1, mlp_w2):
    B, S, D = x.shape
    assert D == D_MODEL and S % MOBA_BLOCK == 0 and S % 512 == 0
    h = x.reshape(B * S, D)
    hb = h
    n_mixers = 4
    for i in range(DEPTH):
        kind, slot = i % n_mixers, i // n_mixers
        if kind == 0:
            o = _mla_mixer(hb, B, S, mla_w_in[slot], mla_q_norm[slot], mla_kv_norm[slot],
                           mla_w_uq[slot], mla_w_ukv[slot])
            w_o = mla_w_o[slot]
        elif kind == 1:
            proj = _proj(hb, hgrn_w_in, F32, layer=slot)
            o = _hgrn_mixer_core(proj, hgrn_lb_logits, hgrn_o_norm[slot], B, S, i)
            w_o = hgrn_w_o[slot]
        elif kind == 2:
            o = _sb_mixer(hb, B, S, sb_w_in, slot)
            w_o = sb_w_o[slot]
        else:
            o = _moba_mixer(hb, B, S, moba_w_in, slot)
            w_o = moba_w_o[slot]
        h, hb = _proj_res_ln(o, w_o.astype(BF16), h, ln_g[i, 0], ln_b[i, 0])
        a, w2 = _proj(hb, mlp_w1, BF16, layer=i, act="relu2", cast_stack=mlp_w2)
        h, hb = _proj_res_ln(a, w2, h, ln_g[i, 1], ln_b[i, 1])
    return h.reshape(B, S, D)
```
